```python
import math
import jax, jax.numpy as jnp
from jax import lax
import numpy as np

D_MODEL = 4096
BATCH = 8
SEQ = 2048
DEPTH = 4
DEC_BATCH = 16
DEC_SEQ = 32
PAST_LEN = 2048

CHUNK = 64
N_MIXERS = 3
N_S5_LAYERS = (DEPTH + 2) // 3
N_ATTN_LAYERS = (DEPTH + 1) // 3
N_HGRN_LAYERS = DEPTH // 3
S5_GROUP = 16
S5_GROUPS = D_MODEL // S5_GROUP
S5_STATE = 64
ATTN_HEADS = 32
ATTN_HEAD_DIM = D_MODEL // ATTN_HEADS
BAND_CHUNKS = 8
WINDOW = BAND_CHUNKS * CHUNK
REL_CLIP = 128
HGRN_HEADS = 32
HGRN_EXPAND = 128
HGRN_FDIM = HGRN_HEADS * HGRN_EXPAND
HGRN_VDIM = D_MODEL // HGRN_HEADS
HGRN_BLOCK = 16
D_FF = -(-8 * D_MODEL // (3 * 256)) * 256
RMS_EPS = 1e-6

kernel_name = 'hybrid_streaming_s5_chunkattn_hgrn2_step'


def rms_norm(x, gain):
    x32 = x.astype(jnp.float32)
    y = x32 * lax.rsqrt(jnp.mean(x32 * x32, axis=-1, keepdims=True) + RMS_EPS)
    return (y * gain.astype(jnp.float32)).astype(x.dtype)


def swiglu(h, w_gate_up, w_down):
    gate, up = jnp.split(h @ w_gate_up, 2, axis=-1)
    return (jax.nn.silu(gate) * up) @ w_down


def _complex_affine_combine(left, right):
    a1r, a1i, b1r, b1i = left
    a2r, a2i, b2r, b2i = right
    return (a2r * a1r - a2i * a1i,
            a2r * a1i + a2i * a1r,
            a2r * b1r - a2i * b1i + b2r,
            a2r * b1i + a2i * b1r + b2i)


def s5_mixer(h, h0_re, h0_im, lam_re, lam_im, log_step, b_re, b_im, c_re, c_im, d_skip, w_glu, n_blocks):
    f32 = jnp.float32
    bsz, seq, _ = h.shape
    lam_re = lam_re.astype(f32)
    lam_im = lam_im.astype(f32)
    step = jnp.exp(log_step.astype(f32))[:, None]
    mag = jnp.exp(lam_re * step)
    lbar_re = mag * jnp.cos(lam_im * step)
    lbar_im = mag * jnp.sin(lam_im * step)
    denom = lam_re * lam_re + lam_im * lam_im
    num_re = lbar_re - 1.0
    coef_re = (num_re * lam_re + lbar_im * lam_im) / denom
    coef_im = (lbar_im * lam_re - num_re * lam_im) / denom
    b_re = b_re.astype(f32)
    b_im = b_im.astype(f32)
    bbar_re = coef_re[..., None] * b_re - coef_im[..., None] * b_im
    bbar_im = coef_re[..., None] * b_im + coef_im[..., None] * b_re
    c_re = c_re.astype(f32)
    c_im = c_im.astype(f32)
    u = h.astype(f32).reshape(bsz, n_blocks, seq // n_blocks, S5_GROUPS, S5_GROUP)
    u = jnp.swapaxes(u, 0, 1)

    def block_step(carry, u_blk):
        hr, hi = carry
        bu_re = jnp.einsum('gpj,btgj->btgp', bbar_re, u_blk)
        bu_im = jnp.einsum('gpj,btgj->btgp', bbar_im, u_blk)
        bu_re = bu_re.at[:, 0].add(lbar_re * hr - lbar_im * hi)
        bu_im = bu_im.at[:, 0].add(lbar_re * hi + lbar_im * hr)
        a_re = jnp.broadcast_to(lbar_re, bu_re.shape)
        a_im = jnp.broadcast_to(lbar_im, bu_im.shape)
        _, _, xr, xi = lax.associative_scan(_complex_affine_combine, (a_re, a_im, bu_re, bu_im), axis=1)
        y = jnp.einsum('gjp,btgp->btgj', c_re, xr) - jnp.einsum('gjp,btgp->btgj', c_im, xi)
        return (xr[:, -1], xi[:, -1]), y

    (hr, hi), ys = lax.scan(block_step, (h0_re.astype(f32), h0_im.astype(f32)), u)
    y = jnp.swapaxes(ys, 0, 1).reshape(bsz, seq, D_MODEL)
    y = y + d_skip.astype(f32) * h.astype(f32)
    z = jax.nn.gelu(y).astype(h.dtype)
    val, gate = jnp.split(z @ w_glu, 2, axis=-1)
    return val * jax.nn.sigmoid(gate), hr.astype(h0_re.dtype), hi.astype(h0_im.dtype)


def band_attention(q, k, v, q_pos, k_pos, rel_bias):
    scores = jnp.einsum('bqhd,bkhd->bhqk', q, k, preferred_element_type=jnp.float32) * (ATTN_HEAD_DIM ** -0.5)
    rel = jnp.clip(q_pos[:, None] - k_pos[None, :], -REL_CLIP, REL_CLIP) + REL_CLIP
    scores = scores + rel_bias.astype(jnp.float32)[:, rel][None]
    q_chunk = q_pos // CHUNK
    k_chunk = k_pos // CHUNK
    allowed = ((k_pos[None, :] >= 0)
               & (k_chunk[None, :] <= q_chunk[:, None])
               & (k_chunk[None, :] >= q_chunk[:, None] - BAND_CHUNKS))
    scores = jnp.where(allowed[None, None], scores, -1e30)
    probs = jax.nn.softmax(scores, axis=-1).astype(v.dtype)
    return jnp.einsum('bhqk,bkhd->bqhd', probs, v)


def attn_mixer(h, cache_k, cache_v, w_qkv, rel_bias, w_o, prompt):
    bsz, seq, _ = h.shape
    qkv = (h @ w_qkv).reshape(bsz, seq, 3, ATTN_HEADS, ATTN_HEAD_DIM)
    q, k, v = qkv[:, :, 0], qkv[:, :, 1], qkv[:, :, 2]
    if prompt:
        n_chunks = seq // CHUNK
        widths = ((0, 0), (WINDOW, 0), (0, 0), (0, 0))
        k_pad = jnp.pad(k, widths)
        v_pad = jnp.pad(v, widths)

        def one_chunk(c):
            start = c * CHUNK
            q_c = lax.dynamic_slice_in_dim(q, start, CHUNK, axis=1)
            k_c = lax.dynamic_slice_in_dim(k_pad, start, WINDOW + CHUNK, axis=1)
            v_c = lax.dynamic_slice_in_dim(v_pad, start, WINDOW + CHUNK, axis=1)
            q_pos = start + jnp.arange(CHUNK)
            k_pos = start - WINDOW + jnp.arange(WINDOW + CHUNK)
            return band_attention(q_c, k_c, v_c, q_pos, k_pos, rel_bias)

        out = lax.map(one_chunk, jnp.arange(n_chunks))
        out = jnp.swapaxes(out, 0, 1).reshape(bsz, seq, D_MODEL)
        keep = min(WINDOW, seq)
        new_k = k[:, seq - keep:]
        new_v = v[:, seq - keep:]
    else:
        n_cached = cache_k.shape[1]
        k_all = jnp.concatenate([cache_k.astype(k.dtype), k], axis=1)
        v_all = jnp.concatenate([cache_v.astype(v.dtype), v], axis=1)
        q_pos = PAST_LEN + jnp.arange(seq)
        k_pos = PAST_LEN - n_cached + jnp.arange(n_cached + seq)
        out = band_attention(q, k_all, v_all, q_pos, k_pos, rel_bias).reshape(bsz, seq, D_MODEL)
        new_k = k_all[:, seq:]
        new_v = v_all[:, seq:]
    return out @ w_o, new_k, new_v


def _hgrn_block(state, blk):
    q, k, v, g = blk
    t = q.shape[1]
    b = jnp.cumsum(g, axis=1)
    o_inter = jnp.einsum('bthk,bhkv->bthv', q * jnp.exp(b), state)
    causal = jnp.tril(jnp.ones((t, t), dtype=bool))
    diff = b[:, :, None] - b[:, None, :]
    decay = jnp.exp(jnp.where(causal[None, :, :, None, None], diff, -jnp.inf))
    scores = jnp.einsum('bthk,bshk,btshk->bhts', q, k, decay)
    o_intra = jnp.einsum('bhts,bshv->bthv', scores, v)
    b_last = b[:, -1]
    new_state = (jnp.exp(b_last)[..., None] * state
                 + jnp.einsum('bshk,bshv->bhkv', k * jnp.exp(b_last[:, None] - b), v))
    return new_state, o_intra + o_inter


def hgrn_mixer(h, s0, w_in, lower_bound, norm_gain, w_o):
    f32 = jnp.float32
    bsz, seq, _ = h.shape
    q, f_logit, inp, gate = jnp.split(h @ w_in, [HGRN_FDIM, 2 * HGRN_FDIM, 2 * HGRN_FDIM + D_MODEL], axis=-1)
    lb = lower_bound.astype(f32)
    log_f = jnp.logaddexp(jnp.log(lb), jnp.log1p(-lb) + jax.nn.log_sigmoid(f_logit.astype(f32)))
    k_in = -jnp.expm1(log_f)
    q = jax.nn.silu(q.astype(f32)) * (HGRN_EXPAND ** -0.5)

    def heads(a, d):
        return a.reshape(bsz, seq, HGRN_HEADS, d)

    q = heads(q, HGRN_EXPAND)
    k_in = heads(k_in, HGRN_EXPAND)
    log_f = heads(log_f, HGRN_EXPAND)
    v = heads(inp.astype(f32), HGRN_VDIM)
    pad = (-seq) % HGRN_BLOCK
    if pad:
        widths = ((0, 0), (0, pad), (0, 0), (0, 0))
        q, k_in, v, log_f = [jnp.pad(a, widths) for a in (q, k_in, v, log_f)]
    n_blocks = (seq + pad) // HGRN_BLOCK

    def to_blocks(a):
        return jnp.swapaxes(a.reshape(bsz, n_blocks, HGRN_BLOCK, *a.shape[2:]), 0, 1)

    s_new, o = lax.scan(_hgrn_block, s0.astype(f32), (to_blocks(q), to_blocks(k_in), to_blocks(v), to_blocks(log_f)))
    o = jnp.swapaxes(o, 0, 1).reshape(bsz, seq + pad, HGRN_HEADS, HGRN_VDIM)[:, :seq]
    o = o * lax.rsqrt(jnp.mean(o * o, axis=-1, keepdims=True) + RMS_EPS)
    o = o.reshape(bsz, seq, D_MODEL) * norm_gain.astype(f32) * jax.nn.silu(gate.astype(f32))
    return o.astype(h.dtype) @ w_o, s_new.astype(s0.dtype)


def _trunk(x, prompt, s5_re0, s5_im0, attn_k0, attn_v0, hgrn0,
           norm_mixer, norm_ffn, norm_final, ffn_w_gate_up, ffn_w_down,
           s5_lambda_re, s5_lambda_im, s5_log_step, s5_b_re, s5_b_im, s5_c_re, s5_c_im, s5_d, s5_w_glu,
           attn_w_qkv, attn_rel_bias, attn_w_o,
           hgrn_w_in, hgrn_lower_bounds, hgrn_norm, hgrn_w_o):
    lbs = jnp.cumsum(jax.nn.softmax(hgrn_lower_bounds.astype(jnp.float32), axis=0), axis=0)
    lbs = lbs - lbs[0]
    n_blocks = x.shape[1] // CHUNK if prompt else 1
    new_re, new_im, new_k, new_v, new_s = [], [], [], [], []
    for layer in range(DEPTH):
        h = rms_norm(x, norm_mixer[layer])
        kind = layer % N_MIXERS
        j = layer // N_MIXERS
        if kind == 0:
            out, hr, hi = s5_mixer(h, s5_re0[j], s5_im0[j], s5_lambda_re[j], s5_lambda_im[j], s5_log_step[j],
                                   s5_b_re[j], s5_b_im[j], s5_c_re[j], s5_c_im[j], s5_d[j], s5_w_glu[j], n_blocks)
            new_re.append(hr)
            new_im.append(hi)
        elif kind == 1:
            ck = None if prompt else attn_k0[j]
            cv = None if prompt else attn_v0[j]
            out, kn, vn = attn_mixer(h, ck, cv, attn_w_qkv[j], attn_rel_bias[j], attn_w_o[j], prompt)
            new_k.append(kn)
            new_v.append(vn)
        else:
            out, sn = hgrn_mixer(h, hgrn0[j], hgrn_w_in[j], lbs[layer], hgrn_norm[j], hgrn_w_o[j])
            new_s.append(sn)
        x = x + out
        x = x + swiglu(rms_norm(x, norm_ffn[layer]), ffn_w_gate_up[layer], ffn_w_down[layer])
    y = rms_norm(x, norm_final)
    return y, jnp.stack(new_re), jnp.stack(new_im), jnp.stack(new_k), jnp.stack(new_v), jnp.stack(new_s)


def setup_inputs(seed: int = 0) -> dict:
    key = jax.random.key(seed)
    ks = jax.random.split(key, 32)
    f32 = jnp.float32

    def nrm(k, shape, scale):
        return jax.random.normal(k, shape, f32) * scale

    w_cache = min(WINDOW, PAST_LEN)
    n_idx = jnp.arange(S5_STATE, dtype=f32)
    return {
        'x_prompt': nrm(ks[0], (BATCH, SEQ, D_MODEL), 1.0),
        'x_sample': nrm(ks[1], (DEC_BATCH, DEC_SEQ, D_MODEL), 1.0),
        'state_s5_re': nrm(ks[2], (N_S5_LAYERS, DEC_BATCH, S5_GROUPS, S5_STATE), 0.5),
        'state_s5_im': nrm(ks[3], (N_S5_LAYERS, DEC_BATCH, S5_GROUPS, S5_STATE), 0.5),
        'cache_attn_k': nrm(ks[4], (N_ATTN_LAYERS, DEC_BATCH, w_cache, ATTN_HEADS, ATTN_HEAD_DIM), 1.0),
        'cache_attn_v': nrm(ks[5], (N_ATTN_LAYERS, DEC_BATCH, w_cache, ATTN_HEADS, ATTN_HEAD_DIM), 1.0),
        'state_hgrn': nrm(ks[6], (N_HGRN_LAYERS, DEC_BATCH, HGRN_HEADS, HGRN_EXPAND, HGRN_VDIM), 0.5),
        'norm_mixer': 1.0 + nrm(ks[7], (DEPTH, D_MODEL), 0.01),
        'norm_ffn': 1.0 + nrm(ks[8], (DEPTH, D_MODEL), 0.01),
        'norm_final': 1.0 + nrm(ks[9], (D_MODEL,), 0.01),
        'ffn_w_gate_up': nrm(ks[10], (DEPTH, D_MODEL, 2 * D_FF), D_MODEL ** -0.5),
        'ffn_w_down': nrm(ks[11], (DEPTH, D_FF, D_MODEL), D_FF ** -0.5),
        's5_lambda_re': -0.5 + nrm(ks[12], (N_S5_LAYERS, S5_GROUPS, S5_STATE), 0.01),
        's5_lambda_im': math.pi * n_idx + nrm(ks[13], (N_S5_LAYERS, S5_GROUPS, S5_STATE), 0.01),
        's5_log_step': jax.random.uniform(ks[14], (N_S5_LAYERS, S5_GROUPS), f32, math.log(1e-3), math.log(1e-1)),
        's5_b_re': nrm(ks[15], (N_S5_LAYERS, S5_GROUPS, S5_STATE, S5_GROUP), (2 * S5_GROUP) ** -0.5),
        's5_b_im': nrm(ks[16], (N_S5_LAYERS, S5_GROUPS, S5_STATE, S5_GROUP), (2 * S5_GROUP) ** -0.5),
        's5_c_re': nrm(ks[17], (N_S5_LAYERS, S5_GROUPS, S5_GROUP, S5_STATE), S5_STATE ** -0.5),
        's5_c_im': nrm(ks[18], (N_S5_LAYERS, S5_GROUPS, S5_GROUP, S5_STATE), S5_STATE ** -0.5),
        's5_d': nrm(ks[19], (N_S5_LAYERS, D_MODEL), 1.0),
        's5_w_glu': nrm(ks[20], (N_S5_LAYERS, D_MODEL, 2 * D_MODEL), D_MODEL ** -0.5),
        'attn_w_qkv': nrm(ks[21], (N_ATTN_LAYERS, D_MODEL, 3 * D_MODEL), D_MODEL ** -0.5),
        'attn_rel_bias': nrm(ks[22], (N_ATTN_LAYERS, ATTN_HEADS, 2 * REL_CLIP + 1), 0.1),
        'attn_w_o': nrm(ks[23], (N_ATTN_LAYERS, D_MODEL, D_MODEL), D_MODEL ** -0.5),
        'hgrn_w_in': nrm(ks[24], (N_HGRN_LAYERS, D_MODEL, 2 * HGRN_FDIM + 2 * D_MODEL), D_MODEL ** -0.5),
        'hgrn_lower_bounds': nrm(ks[25], (DEPTH, HGRN_FDIM), 0.1),
        'hgrn_norm': 1.0 + nrm(ks[26], (N_HGRN_LAYERS, D_MODEL), 0.01),
        'hgrn_w_o': nrm(ks[27], (N_HGRN_LAYERS, D_MODEL, D_MODEL), D_MODEL ** -0.5),
    }


def reference(x_prompt, x_sample, state_s5_re, state_s5_im, cache_attn_k, cache_attn_v, state_hgrn,
              norm_mixer, norm_ffn, norm_final, ffn_w_gate_up, ffn_w_down,
              s5_lambda_re, s5_lambda_im, s5_log_step, s5_b_re, s5_b_im, s5_c_re, s5_c_im, s5_d, s5_w_glu,
              attn_w_qkv, attn_rel_bias, attn_w_o,
              hgrn_w_in, hgrn_lower_bounds, hgrn_norm, hgrn_w_o):
    bsz = x_prompt.shape[0]
    zeros_s5 = jnp.zeros((N_S5_LAYERS, bsz, S5_GROUPS, S5_STATE), x_prompt.dtype)
    zeros_hgrn = jnp.zeros((N_HGRN_LAYERS, bsz, HGRN_HEADS, HGRN_EXPAND, HGRN_VDIM), x_prompt.dtype)
    y_prompt, p_s5_re, p_s5_im, p_k, p_v, p_hgrn = _trunk(
        x_prompt, True, zeros_s5, zeros_s5, None, None, zeros_hgrn,
        norm_mixer, norm_ffn, norm_final, ffn_w_gate_up, ffn_w_down,
        s5_lambda_re, s5_lambda_im, s5_log_step, s5_b_re, s5_b_im, s5_c_re, s5_c_im, s5_d, s5_w_glu,
        attn_w_qkv, attn_rel_bias, attn_w_o,
        hgrn_w_in, hgrn_lower_bounds, hgrn_norm, hgrn_w_o)
    y_sample, s_s5_re, s_s5_im, s_k, s_v, s_hgrn = _trunk(
        x_sample, False, state_s5_re, state_s5_im, cache_attn_k, cache_attn_v, state_hgrn,
        norm_mixer, norm_ffn, norm_final, ffn_w_gate_up, ffn_w_down,
        s5_lambda_re, s5_lambda_im, s5_log_step, s5_b_re, s5_b_im, s5_c_re, s5_c_im, s5_d, s5_w_glu,
        attn_w_qkv, attn_rel_bias, attn_w_o,
        hgrn_w_in, hgrn_lower_bounds, hgrn_norm, hgrn_w_o)
    return (y_prompt, y_sample, p_s5_re, p_s5_im, p_k, p_v, p_hgrn, s_s5_re, s_s5_im, s_k, s_v, s_hgrn)
```

```python
import functools
import math

import numpy as np
import jax
import jax.numpy as jnp
from jax import lax
from jax.experimental import pallas as pl
from jax.experimental.pallas import tpu as pltpu

F32 = jnp.float32
BF16 = jnp.bfloat16

PAST_LEN = 2048
CHUNK = 64
BAND_CHUNKS = 8
WINDOW = BAND_CHUNKS * CHUNK
REL_CLIP = 128
N_MIXERS = 3
S5_GROUP = 16
S5_BLOCK = 16
HEAD_DIM = 128
RMS_EPS = 1e-6
NEG_INF = -1e30

V7X_LANES = 128
V7X_SUBLANES = 8
V7X_VMEM_LIMIT = 56 * 1024 * 1024


def _pick(n, candidates):
    for c in candidates:
        if n % c == 0:
            return c
    return n


def _cparams(n_axes):
    return pltpu.CompilerParams(
        dimension_semantics=("parallel",) * n_axes,
        vmem_limit_bytes=V7X_VMEM_LIMIT)


def _sigmoid(x):
    return 1.0 / (1.0 + jnp.exp(-x))


def _gelu_tanh(x):
    c = math.sqrt(2.0 / math.pi)
    return 0.5 * x * (1.0 + jnp.tanh(c * (x + 0.044715 * (x * x * x))))


def _rmsnorm_kernel(x_ref, g_ref, o_ref):
    x = x_ref[...]
    ms = jnp.mean(x * x, axis=-1, keepdims=True)
    o_ref[...] = (x * lax.rsqrt(ms + RMS_EPS) * g_ref[...]).astype(o_ref.dtype)


def rmsnorm(x, gain, out_dtype):
    m, d = x.shape
    bm = _pick(m, (256, 128, 64, 32, 16, 8))
    return pl.pallas_call(
        _rmsnorm_kernel,
        out_shape=jax.ShapeDtypeStruct((m, d), out_dtype),
        grid=(m // bm,),
        in_specs=[pl.BlockSpec((bm, d), lambda i: (i, 0)),
                  pl.BlockSpec((1, d), lambda i: (0, 0))],
        out_specs=pl.BlockSpec((bm, d), lambda i: (i, 0)),
        compiler_params=_cparams(1),
        name="rmsnorm",
    )(x, gain.reshape(1, d).astype(F32))


def _mm_kernel(*refs, n_w, epi, has_res):
    a_ref = refs[0]
    w_refs = refs[1:1 + n_w]
    res_ref = refs[1 + n_w] if has_res else None
    o_ref = refs[1 + n_w + int(has_res)]
    a = a_ref[...]
    vals = [jnp.dot(a, w[...], preferred_element_type=F32) for w in w_refs]
    if epi == "none":
        o = vals[0]
    elif epi == "swiglu":
        o = vals[0] * _sigmoid(vals[0]) * vals[1]
    elif epi == "res":
        o = res_ref[...] + vals[0]
    elif epi == "glu_res":
        o = res_ref[...] + vals[0] * _sigmoid(vals[1])
    else:
        raise ValueError(epi)
    o_ref[...] = o.astype(o_ref.dtype)


def matmul(a, w, *, n_out, w_col_offsets, epi, out_dtype, res=None, bm, bn, w_resident=False, name):
    m, k = a.shape
    n_w = len(w_col_offsets)
    assert m % bm == 0 and n_out % bn == 0
    assert all(off % bn == 0 for off in w_col_offsets)
    if w_resident:
        grid = (n_out // bn, m // bm)
        ij = lambda g0, g1: (g1, g0)
    else:
        grid = (m // bm, n_out // bn)
        ij = lambda g0, g1: (g0, g1)

    def a_map(g0, g1):
        return (ij(g0, g1)[0], 0)

    def w_map(off_blocks):
        return lambda g0, g1: (0, ij(g0, g1)[1] + off_blocks)

    def o_map(g0, g1):
        return ij(g0, g1)

    in_specs = [pl.BlockSpec((bm, k), a_map)]
    args = [a]
    for off in w_col_offsets:
        in_specs.append(pl.BlockSpec((k, bn), w_map(off // bn)))
        args.append(w)
    if res is not None:
        in_specs.append(pl.BlockSpec((bm, bn), o_map))
        args.append(res)
    return pl.pallas_call(
        functools.partial(_mm_kernel, n_w=n_w, epi=epi, has_res=res is not None),
        out_shape=jax.ShapeDtypeStruct((m, n_out), out_dtype),
        grid=grid,
        in_specs=in_specs,
        out_specs=pl.BlockSpec((bm, bn), o_map),
        compiler_params=_cparams(2),
        name=name,
    )(*args)


def _s5_operators(lam_re, lam_im, log_step, b_re, b_im, c_re, c_im):
    g, p = lam_re.shape
    j = b_re.shape[-1]
    t = S5_BLOCK
    step = jnp.exp(log_step.astype(F32))[:, None]
    lr = lam_re.astype(F32) * step
    li = lam_im.astype(F32) * step
    tau = jnp.arange(t + 1, dtype=F32)[:, None, None]
    mag = jnp.exp(tau * lr[None])
    e_re = mag * jnp.cos(tau * li[None])
    e_im = mag * jnp.sin(tau * li[None])
    lbar_re, lbar_im = e_re[1], e_im[1]
    lam_re = lam_re.astype(F32)
    lam_im = lam_im.astype(F32)
    denom = lam_re * lam_re + lam_im * lam_im
    num_re = lbar_re - 1.0
    coef_re = (num_re * lam_re + lbar_im * lam_im) / denom
    coef_im = (lbar_im * lam_re - num_re * lam_im) / denom
    b_re = b_re.astype(F32)
    b_im = b_im.astype(F32)
    bb_re = coef_re[..., None] * b_re - coef_im[..., None] * b_im
    bb_im = coef_re[..., None] * b_im + coef_im[..., None] * b_re
    c_re = c_re.astype(F32)
    c_im = c_im.astype(F32)
    hp = lax.Precision.HIGHEST
    ce_re = c_re[None] * e_re[:, :, None, :] - c_im[None] * e_im[:, :, None, :]
    ce_im = c_re[None] * e_im[:, :, None, :] + c_im[None] * e_re[:, :, None, :]
    k_lag = (jnp.einsum("tgop,gpi->tgoi", ce_re[:t], bb_re, precision=hp)
             - jnp.einsum("tgop,gpi->tgoi", ce_im[:t], bb_im, precision=hp))
    s_idx = np.arange(t)[:, None]
    t_idx = np.arange(t)[None, :]
    lag = np.clip(t_idx - s_idx, 0, t - 1)
    toep = k_lag[lag]
    toep = jnp.where((t_idx >= s_idx)[:, :, None, None, None], toep, 0.0)
    toep = jnp.transpose(toep, (2, 0, 4, 1, 3)).reshape(g, t * j, t * j)
    pw_re = e_re[:t][::-1]
    pw_im = e_im[:t][::-1]
    st_re = pw_re[:, :, :, None] * bb_re[None] - pw_im[:, :, :, None] * bb_im[None]
    st_im = pw_re[:, :, :, None] * bb_im[None] + pw_im[:, :, :, None] * bb_re[None]
    st_re = jnp.transpose(st_re, (1, 0, 3, 2)).reshape(g, t * j, p)
    st_im = jnp.transpose(st_im, (1, 0, 3, 2)).reshape(g, t * j, p)
    w_in = jnp.concatenate([toep, st_re, st_im, st_im, st_re], axis=-1)
    ro_re = jnp.transpose(ce_re[1:], (1, 3, 0, 2)).reshape(g, p, t * j)
    ro_im = jnp.transpose(ce_im[1:], (1, 3, 0, 2)).reshape(g, p, t * j)
    w_out = jnp.concatenate([ro_re, -ro_im], axis=1)
    a1 = jnp.concatenate([e_re[t], e_re[t]], axis=-1)[:, None, :]
    a2 = jnp.concatenate([-e_im[t], e_im[t]], axis=-1)[:, None, :]
    return w_in.astype(BF16), w_out.astype(BF16), a1, a2


def _s5_kernel(u_ref, win_ref, wout_ref, a1_ref, a2_ref, h0_ref, h0s_ref,
               y_ref, xf_ref, ys_ref, xs_ref, *, gb, nb, nchunks, p2):
    tj = y_ref.shape[-1]
    for g in range(gb):
        ys_ref[g] = jnp.dot(u_ref[g], win_ref[g], preferred_element_type=F32)
    a1 = [a1_ref[g] for g in range(gb)]
    a2 = [a2_ref[g] for g in range(gb)]

    def step(c, carry):
        r0 = pl.multiple_of(c * nb, nb)
        new = []
        for g in range(gb):
            x, xs = carry[g]
            xs_ref[g, pl.ds(r0, nb), :] = x
            s = ys_ref[g, pl.ds(r0, nb), tj:tj + p2]
            ss = ys_ref[g, pl.ds(r0, nb), tj + p2:tj + 2 * p2]
            new.append((a1[g] * x + a2[g] * xs + s, a1[g] * xs - a2[g] * x + ss))
        return tuple(new)

    carry = tuple((h0_ref[g], h0s_ref[g]) for g in range(gb))
    if nchunks <= 4:
        for c in range(nchunks):
            carry = step(c, carry)
    else:
        carry = lax.fori_loop(0, nchunks, step, carry, unroll=4)
    for g in range(gb):
        xf_ref[g] = carry[g][0]
        y_ref[g] = ys_ref[g, :, 0:tj] + jnp.dot(
            xs_ref[g].astype(BF16), wout_ref[g], preferred_element_type=F32)


def s5_scan(h_bf, h0_re, h0_im, ops):
    w_in, w_out, a1, a2 = ops
    bsz, seq, d = h_bf.shape
    g = w_in.shape[0]
    j = d // g
    p2 = w_out.shape[1]
    p = p2 // 2
    t = S5_BLOCK
    nc = seq // t
    rows = nc * bsz
    tj = t * j
    u = h_bf.reshape(bsz, nc, t, g, j).transpose(3, 1, 0, 2, 4).reshape(g, rows, tj)
    h0 = jnp.concatenate([h0_re, h0_im], axis=-1).astype(F32).transpose(1, 0, 2)
    h0s = jnp.concatenate([h0_im, h0_re], axis=-1).astype(F32).transpose(1, 0, 2)
    gb = _pick(g, (4, 2, 1))
    wcols = w_in.shape[-1]
    y, xf = pl.pallas_call(
        functools.partial(_s5_kernel, gb=gb, nb=bsz, nchunks=nc, p2=p2),
        out_shape=(jax.ShapeDtypeStruct((g, rows, tj), F32),
                   jax.ShapeDtypeStruct((g, bsz, p2), F32)),
        grid=(g // gb,),
        in_specs=[pl.BlockSpec((gb, rows, tj), lambda i: (i, 0, 0)),
                  pl.BlockSpec((gb, tj, wcols), lambda i: (i, 0, 0)),
                  pl.BlockSpec((gb, p2, tj), lambda i: (i, 0, 0)),
                  pl.BlockSpec((gb, 1, p2), lambda i: (i, 0, 0)),
                  pl.BlockSpec((gb, 1, p2), lambda i: (i, 0, 0)),
                  pl.BlockSpec((gb, bsz, p2), lambda i: (i, 0, 0)),
                  pl.BlockSpec((gb, bsz, p2), lambda i: (i, 0, 0))],
        out_specs=(pl.BlockSpec((gb, rows, tj), lambda i: (i, 0, 0)),
                   pl.BlockSpec((gb, bsz, p2), lambda i: (i, 0, 0))),
        scratch_shapes=[pltpu.VMEM((gb, rows, wcols), F32),
                        pltpu.VMEM((gb, rows, p2), F32)],
        compiler_params=_cparams(1),
        name="s5_scan",
    )(u, w_in, w_out, a1, a2, h0, h0s)
    y = y.reshape(g, nc, bsz, t, j).transpose(2, 1, 3, 0, 4).reshape(bsz, seq, d)
    new_re = xf[..., :p].transpose(1, 0, 2)
    new_im = xf[..., p:].transpose(1, 0, 2)
    return y, new_re, new_im


def _s5_gate_kernel(x_ref, y_ref, g_ref, d_ref, o_ref):
    x = x_ref[...]
    ms = jnp.mean(x * x, axis=-1, keepdims=True)
    h = x * lax.rsqrt(ms + RMS_EPS) * g_ref[...]
    o_ref[...] = _gelu_tanh(y_ref[...] + d_ref[...] * h).astype(o_ref.dtype)


def s5_gate(x, y, gain, d_skip):
    m, d = x.shape
    bm = _pick(m, (256, 128, 64, 32, 16, 8))
    row = pl.BlockSpec((bm, d), lambda i: (i, 0))
    vec = pl.BlockSpec((1, d), lambda i: (0, 0))
    return pl.pallas_call(
        _s5_gate_kernel,
        out_shape=jax.ShapeDtypeStruct((m, d), BF16),
        grid=(m // bm,),
        in_specs=[row, row, vec, vec],
        out_specs=row,
        compiler_params=_cparams(1),
        name="s5_gate",
    )(x, y, gain.reshape(1, d).astype(F32), d_skip.reshape(1, d).astype(F32))


QBLK = 2 * CHUNK
KWIN = WINDOW + QBLK


def _attn_bias_tiles(rel_bias):
    qi = np.arange(QBLK)[:, None]
    kj = np.arange(KWIN + WINDOW)[None, :]
    idx = np.clip(WINDOW + qi - kj, -REL_CLIP, REL_CLIP) + REL_CLIP
    return rel_bias.astype(F32)[:, idx]


def _attn_prompt_kernel(q_ref, k_ref, v_ref, b_ref, o_ref, *, seq):
    scale = HEAD_DIM ** -0.5
    for i in range(seq // QBLK):
        q0 = i * QBLK
        start = max(q0 - WINDOW, 0)
        kw = min(KWIN, seq - start)
        shift = start - (q0 - WINDOW)
        q = q_ref[0, q0:q0 + QBLK, :].astype(BF16)
        k = k_ref[0, start:start + kw, :].astype(BF16)
        v = v_ref[0, start:start + kw, :].astype(BF16)
        s = lax.dot_general(q, k, (((1,), (1,)), ((), ())), preferred_element_type=F32)
        s = s * scale + b_ref[0, :, shift:shift + kw]
        qc = (q0 + lax.broadcasted_iota(jnp.int32, (QBLK, kw), 0)) // CHUNK
        kc = (start + lax.broadcasted_iota(jnp.int32, (QBLK, kw), 1)) // CHUNK
        s = jnp.where(kc <= qc, jnp.where(kc >= qc - BAND_CHUNKS, s, NEG_INF), NEG_INF)
        m = jnp.max(s, axis=-1, keepdims=True)
        e = jnp.exp(s - m)
        l = jnp.sum(e, axis=-1, keepdims=True)
        p = (e / l).astype(BF16)
        o = jnp.dot(p, v, preferred_element_type=F32)
        o_ref[0, q0:q0 + QBLK, :] = o.astype(o_ref.dtype)


def attn_prompt(qkv, bias_tiles):
    bsz, seq, d3 = qkv.shape
    d = d3 // 3
    nh = d // HEAD_DIM
    assert seq % QBLK == 0
    blk = lambda off: pl.BlockSpec((1, seq, HEAD_DIM), lambda b, h: (b, 0, h + off))
    return pl.pallas_call(
        functools.partial(_attn_prompt_kernel, seq=seq),
        out_shape=jax.ShapeDtypeStruct((bsz, seq, d), BF16),
        grid=(bsz, nh),
        in_specs=[blk(0), blk(nh), blk(2 * nh),
                  pl.BlockSpec((1, QBLK, KWIN + WINDOW), lambda b, h: (h, 0, 0))],
        out_specs=pl.BlockSpec((1, seq, HEAD_DIM), lambda b, h: (b, 0, h)),
        compiler_params=_cparams(2),
        name="attn_prompt",
    )(qkv, qkv, qkv, bias_tiles)


def _attn_sample_kernel(q_ref, k_ref, v_ref, ck_ref, cv_ref, b_ref, o_ref, *, seq, ncache, mask):
    scale = HEAD_DIM ** -0.5
    q = q_ref[0].astype(BF16)
    dn = (((1,), (1,)), ((), ()))
    s1 = lax.dot_general(q, ck_ref[0].astype(BF16), dn, preferred_element_type=F32)
    s2 = lax.dot_general(q, k_ref[0].astype(BF16), dn, preferred_element_type=F32)
    off = WINDOW - ncache
    s1 = s1 * scale + b_ref[0, 0:seq, off:off + ncache]
    s2 = s2 * scale + b_ref[0, 0:seq, WINDOW:WINDOW + seq]
    if mask is not None:
        s1 = jnp.where(jnp.asarray(mask[:, :ncache]), s1, NEG_INF)
        s2 = jnp.where(jnp.asarray(mask[:, ncache:]), s2, NEG_INF)
    m = jnp.maximum(jnp.max(s1, axis=-1, keepdims=True), jnp.max(s2, axis=-1, keepdims=True))
    e1 = jnp.exp(s1 - m)
    e2 = jnp.exp(s2 - m)
    l = jnp.sum(e1, axis=-1, keepdims=True) + jnp.sum(e2, axis=-1, keepdims=True)
    o = (jnp.dot((e1 / l).astype(BF16), cv_ref[0].astype(BF16), preferred_element_type=F32)
         + jnp.dot((e2 / l).astype(BF16), v_ref[0].astype(BF16), preferred_element_type=F32))
    o_ref[0] = o.astype(o_ref.dtype)


def attn_sample(qkv, cache_k, cache_v, bias_tiles):
    bsz, seq, d3 = qkv.shape
    d = d3 // 3
    nh = d // HEAD_DIM
    ncache = cache_k.shape[1]
    assert seq <= QBLK and ncache <= WINDOW
    q_pos = PAST_LEN + np.arange(seq)
    k_pos = PAST_LEN - ncache + np.arange(ncache + seq)
    qc, kc = q_pos // CHUNK, k_pos // CHUNK
    allowed = ((k_pos[None, :] >= 0) & (kc[None, :] <= qc[:, None])
               & (kc[None, :] >= qc[:, None] - BAND_CHUNKS))
    mask = None if allowed.all() else allowed
    ck = cache_k.reshape(bsz, ncache, d)
    cv = cache_v.reshape(bsz, ncache, d)
    blk = lambda off: pl.BlockSpec((1, seq, HEAD_DIM), lambda b, h: (b, 0, h + off))
    cblk = pl.BlockSpec((1, ncache, HEAD_DIM), lambda b, h: (b, 0, h))
    return pl.pallas_call(
        functools.partial(_attn_sample_kernel, seq=seq, ncache=ncache, mask=mask),
        out_shape=jax.ShapeDtypeStruct((bsz, seq, d), BF16),
        grid=(bsz, nh),
        in_specs=[blk(0), blk(nh), blk(2 * nh), cblk, cblk,
                  pl.BlockSpec((1, QBLK, KWIN + WINDOW), lambda b, h: (h, 0, 0))],
        out_specs=pl.BlockSpec((1, seq, HEAD_DIM), lambda b, h: (b, 0, h)),
        compiler_params=_cparams(2),
        name="attn_sample",
    )(qkv, qkv, qkv, ck, cv, bias_tiles)


def _cumsum_rows(g):
    c = g.shape[0]
    row = lax.broadcasted_iota(jnp.int32, g.shape, 0)
    x = g
    for sh in (1, 2, 4):
        x = x + jnp.where((row & (V7X_SUBLANES - 1)) >= sh, pltpu.roll(x, sh, 0), 0.0)
    pieces = []
    carry = None
    for j in range(c // V7X_SUBLANES):
        blk = x[j * V7X_SUBLANES:(j + 1) * V7X_SUBLANES]
        if carry is not None:
            blk = blk + carry
        pieces.append(blk)
        carry = blk[V7X_SUBLANES - 1:V7X_SUBLANES]
    return jnp.concatenate(pieces, axis=0)


def _hgrn_kernel(*refs, seq, csz, has_state):
    if has_state:
        q_ref, f_ref, v_ref, z_ref, llb_ref, l1m_ref, gain_ref, s0_ref, o_ref, sn_ref, st_ref = refs
    else:
        q_ref, f_ref, v_ref, z_ref, llb_ref, l1m_ref, gain_ref, o_ref, sn_ref, st_ref = refs
    kdim = HEAD_DIM
    scale = kdim ** -0.5
    dn_t = (((1,), (1,)), ((), ()))
    if has_state:
        st_ref[...] = s0_ref[0, 0].T
    else:
        st_ref[...] = jnp.zeros_like(st_ref)
    llb = llb_ref[...]
    l1m = l1m_ref[...]
    gain = gain_ref[...]
    row = lax.broadcasted_iota(jnp.int32, (csz, kdim), 0)
    ti = lax.broadcasted_iota(jnp.int32, (csz, csz), 0)
    si = lax.broadcasted_iota(jnp.int32, (csz, csz), 1)

    def chunk(ci, _):
        r0 = pl.multiple_of(ci * csz, csz)
        qz = q_ref[0, pl.ds(r0, csz), :]
        fz = f_ref[0, pl.ds(r0, csz), :]
        v = v_ref[0, pl.ds(r0, csz), :]
        gz = z_ref[0, pl.ds(r0, csz), :]
        ls = jnp.minimum(fz, 0.0) - jnp.log1p(jnp.exp(-jnp.abs(fz)))
        cc = l1m + ls
        g = jnp.maximum(llb, cc) + jnp.log1p(jnp.exp(-jnp.abs(llb - cc)))
        kk = jnp.exp(cc - fz)
        q = qz * _sigmoid(qz) * scale
        b = _cumsum_rows(g)
        v_bf = v.astype(BF16)
        sc = lax.dot_general(q.astype(BF16), kk.astype(BF16), dn_t, preferred_element_type=F32)
        scores = jnp.where(ti == si, sc, 0.0)
        e = b
        w = 1
        while w < csz:
            bit = (row & w) != 0
            e_prev = pltpu.roll(e, w, 0)
            ex = jnp.exp(jnp.where(bit, b - e_prev, e - b))
            qw = jnp.where(bit, q * ex, 0.0).astype(BF16)
            kw = jnp.where(bit, 0.0, kk * ex).astype(BF16)
            sc = lax.dot_general(qw, kw, dn_t, preferred_element_type=F32)
            lg = int(math.log2(2 * w))
            scores = scores + jnp.where((ti >> lg) == (si >> lg), sc, 0.0)
            if 2 * w < csz:
                e = jnp.where(bit, e, pltpu.roll(e, csz - w, 0))
            w *= 2
        o = jnp.dot(scores.astype(BF16), v_bf, preferred_element_type=F32)
        st = st_ref[...]
        o = o + lax.dot_general((q * jnp.exp(b)).astype(BF16), st.astype(BF16), dn_t,
                                preferred_element_type=F32)
        b_last = b[csz - 1:csz]
        kd = (kk * jnp.exp(b_last - b)).astype(BF16)
        st_ref[...] = jnp.exp(b_last) * st + jnp.dot(v_bf.T, kd, preferred_element_type=F32)
        o = o * lax.rsqrt(jnp.mean(o * o, axis=-1, keepdims=True) + RMS_EPS)
        o = o * gain * (gz * _sigmoid(gz))
        o_ref[0, pl.ds(r0, csz), :] = o.astype(o_ref.dtype)
        return 0

    nchunks = seq // csz
    if nchunks == 1:
        chunk(0, 0)
    else:
        lax.fori_loop(0, nchunks, chunk, 0)
    sn_ref[0, 0] = st_ref[...].T


def hgrn_core(proj, log_lb, log_1m_lb, norm_gain, state):
    bsz, seq, d4 = proj.shape
    d = d4 // 4
    nh = d // HEAD_DIM
    csz = _pick(seq, (128, 64, 32, 16, 8))
    has_state = state is not None
    blk = lambda off: pl.BlockSpec((1, seq, HEAD_DIM), lambda b, h: (b, 0, h + off))
    vec = pl.BlockSpec((1, HEAD_DIM), lambda b, h: (0, h))
    sblk = pl.BlockSpec((1, 1, HEAD_DIM, HEAD_DIM), lambda b, h: (b, h, 0, 0))
    in_specs = [blk(0), blk(nh), blk(2 * nh), blk(3 * nh), vec, vec, vec]
    args = [proj, proj, proj, proj, log_lb.reshape(1, d), log_1m_lb.reshape(1, d),
            norm_gain.reshape(1, d).astype(F32)]
    if has_state:
        in_specs.append(sblk)
        args.append(state.astype(F32))
    return pl.pallas_call(
        functools.partial(_hgrn_kernel, seq=seq, csz=csz, has_state=has_state),
        out_shape=(jax.ShapeDtypeStruct((bsz, seq, d), BF16),
                   jax.ShapeDtypeStruct((bsz, nh, HEAD_DIM, HEAD_DIM), F32)),
        grid=(bsz, nh),
        in_specs=in_specs,
        out_specs=(pl.BlockSpec((1, seq, HEAD_DIM), lambda b, h: (b, 0, h)), sblk),
        scratch_shapes=[pltpu.VMEM((HEAD_DIM, HEAD_DIM), F32)],
        compiler_params=_cparams(2),
        name="hgrn_core",
    )(*args)


def _mm_blocks(m):
    return _pick(m, (1536, 768, 512, 256, 128, 64, 32, 16, 8))


def kernel(x_prompt, x_sample, state_s5_re, state_s5_im, cache_attn_k, cache_attn_v, state_hgrn,
           norm_mixer, norm_ffn, norm_final, ffn_w_gate_up, ffn_w_down,
           s5_lambda_re, s5_lambda_im, s5_log_step, s5_b_re, s5_b_im, s5_c_re, s5_c_im, s5_d, s5_w_glu,
           attn_w_qkv, attn_rel_bias, attn_w_o,
           hgrn_w_in, hgrn_lower_bounds, hgrn_norm, hgrn_w_o):
    bp, sp, d = x_prompt.shape
    bs, ss, _ = x_sample.shape
    depth = norm_mixer.shape[0]
    mp = bp * sp
    ms = bs * ss
    m = mp + ms
    d_ff = ffn_w_down.shape[1]
    s5_groups = s5_lambda_re.shape[1]
    s5_p = s5_lambda_re.shape[2]
    nh = d // HEAD_DIM
    ncache = cache_attn_k.shape[2]

    x = jnp.concatenate([x_prompt.reshape(mp, d), x_sample.reshape(ms, d)], axis=0)
    bm = _mm_blocks(m)
    bm_small = _pick(m, (768, 512, 256, 128, 64, 32, 16, 8))
    bn = _pick(d, (512, 256, 128))

    lbs = jnp.cumsum(jax.nn.softmax(hgrn_lower_bounds.astype(F32), axis=0), axis=0)
    lbs = lbs - lbs[0]

    new_re_p, new_im_p, new_re_s, new_im_s = [], [], [], []
    new_k_p, new_v_p, new_k_s, new_v_s = [], [], [], []
    new_h_p, new_h_s = [], []

    for layer in range(depth):
        kind = layer % N_MIXERS
        j = layer // N_MIXERS
        h = rmsnorm(x, norm_mixer[layer], BF16)
        if kind == 0:
            ops = _s5_operators(s5_lambda_re[j], s5_lambda_im[j], s5_log_step[j],
                                s5_b_re[j], s5_b_im[j], s5_c_re[j], s5_c_im[j])
            zeros = jnp.zeros((bp, s5_groups, s5_p), F32)
            y_p, re_p, im_p = s5_scan(h[:mp].reshape(bp, sp, d), zeros, zeros, ops)
            y_s, re_s, im_s = s5_scan(h[mp:].reshape(bs, ss, d), state_s5_re[j], state_s5_im[j], ops)
            new_re_p.append(re_p)
            new_im_p.append(im_p)
            new_re_s.append(re_s)
            new_im_s.append(im_s)
            y = jnp.concatenate([y_p.reshape(mp, d), y_s.reshape(ms, d)], axis=0)
            z = s5_gate(x, y, norm_mixer[layer], s5_d[j])
            x = matmul(z, s5_w_glu[j].astype(BF16), n_out=d, w_col_offsets=(0, d), epi="glu_res",
                       out_dtype=F32, res=x, bm=bm_small, bn=bn, name="s5_glu")
        elif kind == 1:
            qkv = matmul(h, attn_w_qkv[j].astype(BF16), n_out=3 * d, w_col_offsets=(0,), epi="none",
                         out_dtype=F32, bm=bm, bn=bn, name="attn_qkv")
            qkv_p = qkv[:mp].reshape(bp, sp, 3 * d)
            qkv_s = qkv[mp:].reshape(bs, ss, 3 * d)
            tiles = _attn_bias_tiles(attn_rel_bias[j])
            o_p = attn_prompt(qkv_p, tiles)
            o_s = attn_sample(qkv_s, cache_attn_k[j], cache_attn_v[j], tiles)
            keep = min(WINDOW, sp)
            new_k_p.append(qkv_p[:, sp - keep:, d:2 * d].reshape(bp, keep, nh, HEAD_DIM))
            new_v_p.append(qkv_p[:, sp - keep:, 2 * d:].reshape(bp, keep, nh, HEAD_DIM))
            k_all = jnp.concatenate([cache_attn_k[j], qkv_s[:, :, d:2 * d].reshape(bs, ss, nh, HEAD_DIM)], axis=1)
            v_all = jnp.concatenate([cache_attn_v[j], qkv_s[:, :, 2 * d:].reshape(bs, ss, nh, HEAD_DIM)], axis=1)
            new_k_s.append(k_all[:, ss:])
            new_v_s.append(v_all[:, ss:])
            o = jnp.concatenate([o_p.reshape(mp, d), o_s.reshape(ms, d)], axis=0)
            x = matmul(o, attn_w_o[j].astype(BF16), n_out=d, w_col_offsets=(0,), epi="res",
                       out_dtype=F32, res=x, bm=bm, bn=bn, name="attn_wo")
        else:
            proj = matmul(h, hgrn_w_in[j].astype(BF16), n_out=4 * d, w_col_offsets=(0,), epi="none",
                          out_dtype=F32, bm=bm, bn=bn, name="hgrn_win")
            lb = lbs[layer]
            log_lb = jnp.log(lb)
            log_1m_lb = jnp.log1p(-lb)
            o_p, h_p = hgrn_core(proj[:mp].reshape(bp, sp, 4 * d), log_lb, log_1m_lb, hgrn_norm[j], None)
            o_s, h_s = hgrn_core(proj[mp:].reshape(bs, ss, 4 * d), log_lb, log_1m_lb, hgrn_norm[j],
                                 state_hgrn[j])
            new_h_p.append(h_p)
            new_h_s.append(h_s)
            o = jnp.concatenate([o_p.reshape(mp, d), o_s.reshape(ms, d)], axis=0)
            x = matmul(o, hgrn_w_o[j].astype(BF16), n_out=d, w_col_offsets=(0,), epi="res",
                       out_dtype=F32, res=x, bm=bm, bn=bn, name="hgrn_wo")
        hf = rmsnorm(x, norm_ffn[layer], BF16)
        bn_ff = _pick(d_ff, (256, 128))
        act = matmul(hf, ffn_w_gate_up[layer].astype(BF16), n_out=d_ff, w_col_offsets=(0, d_ff),
                     epi="swiglu", out_dtype=BF16, bm=bm, bn=bn_ff, name="ffn_gate_up")
        x = matmul(act, ffn_w_down[layer].astype(BF16), n_out=d, w_col_offsets=(0,), epi="res",
                   out_dtype=F32, res=x, bm=_pick(m, (512, 256, 128, 64, 32, 16, 8)), bn=bn,
                   w_resident=True, name="ffn_down")

    y = rmsnorm(x, norm_final, F32)
    y_p = y[:mp].reshape(bp, sp, d)
    y_s = y[mp:].reshape(bs, ss, d)
    return (y_p, y_s,
            jnp.stack(new_re_p), jnp.stack(new_im_p),
            jnp.stack(new_k_p), jnp.stack(new_v_p), jnp.stack(new_h_p),
            jnp.stack(new_re_s), jnp.stack(new_im_s),
            jnp.stack(new_k_s), jnp.stack(new_v_s), jnp.stack(new_h_s))
```

```python
import functools
import math

import numpy as np
import jax
import jax.numpy as jnp
from jax import lax
from jax.experimental import pallas as pl
from jax.experimental.pallas import tpu as pltpu

F32 = jnp.float32
BF16 = jnp.bfloat16

PAST_LEN = 2048
CHUNK = 64
BAND_CHUNKS = 8
WINDOW = BAND_CHUNKS * CHUNK
REL_CLIP = 128
N_MIXERS = 3
S5_GROUP = 16
S5_BLOCK = 16
HEAD_DIM = 128
RMS_EPS = 1e-6
NEG_INF = -1e30

V7X_LANES = 128
V7X_SUBLANES = 8
V7X_BF16_ROWS = 16
V7X_MXU = 256
V7X_VMEM_LIMIT = 56 * 1024 * 1024


def _pick(n, candidates):
    for c in candidates:
        if n % c == 0:
            return c
    return n


def _cparams(n_axes):
    return pltpu.CompilerParams(
        dimension_semantics=("parallel",) * n_axes,
        vmem_limit_bytes=V7X_VMEM_LIMIT)


def _sigmoid(x):
    return 1.0 / (1.0 + jnp.exp(-x))


def _gelu_tanh(x):
    c = math.sqrt(2.0 / math.pi)
    return 0.5 * x * (1.0 + jnp.tanh(c * (x + 0.044715 * (x * x * x))))


def _rmsnorm_kernel(x_ref, g_ref, o_ref):
    x = x_ref[...]
    ms = jnp.mean(x * x, axis=-1, keepdims=True)
    o_ref[...] = (x * lax.rsqrt(ms + RMS_EPS) * g_ref[...]).astype(o_ref.dtype)


def rmsnorm(x, gain, out_dtype, row0=0, rows=None):
    m, d = x.shape
    rows = m if rows is None else rows
    bm = _pick(math.gcd(rows, row0) if row0 else rows, (256, 128, 64, 32, 16, 8))
    blk0 = row0 // bm
    return pl.pallas_call(
        _rmsnorm_kernel,
        out_shape=jax.ShapeDtypeStruct((rows, d), out_dtype),
        grid=(rows // bm,),
        in_specs=[pl.BlockSpec((bm, d), lambda i: (i + blk0, 0)),
                  pl.BlockSpec((1, d), lambda i: (0, 0))],
        out_specs=pl.BlockSpec((bm, d), lambda i: (i, 0)),
        compiler_params=_cparams(1),
        name="rmsnorm",
    )(x, gain.reshape(1, d).astype(F32))


def _mm_kernel(*refs, n_w, epi, has_res):
    a_ref = refs[0]
    w_refs = refs[1:1 + n_w]
    res_ref = refs[1 + n_w] if has_res else None
    o_ref = refs[1 + n_w + int(has_res)]
    a = a_ref[...]
    vals = [jnp.dot(a, w[...], preferred_element_type=F32) for w in w_refs]
    if epi == "none":
        o = vals[0]
    elif epi == "swiglu":
        o = vals[0] * _sigmoid(vals[0]) * vals[1]
    elif epi == "res":
        o = res_ref[...] + vals[0]
    elif epi == "glu_res":
        o = res_ref[...] + vals[0] * _sigmoid(vals[1])
    else:
        raise ValueError(epi)
    o_ref[...] = o.astype(o_ref.dtype)


def matmul(a, w, layer, *, n_out, w_col_offsets, epi, out_dtype, res=None, bm, bn, w_resident=False, name):
    m, k = a.shape
    n_w = len(w_col_offsets)
    assert m % bm == 0 and n_out % bn == 0
    assert all(off % bn == 0 for off in w_col_offsets)
    if w_resident:
        grid = (n_out // bn, m // bm)
        ij = lambda g0, g1: (g1, g0)
    else:
        grid = (m // bm, n_out // bn)
        ij = lambda g0, g1: (g0, g1)

    def a_map(g0, g1):
        return (ij(g0, g1)[0], 0)

    def w_map(off_blocks):
        return lambda g0, g1: (layer, 0, ij(g0, g1)[1] + off_blocks)

    def o_map(g0, g1):
        return ij(g0, g1)

    in_specs = [pl.BlockSpec((bm, k), a_map)]
    args = [a]
    for off in w_col_offsets:
        in_specs.append(pl.BlockSpec((None, k, bn), w_map(off // bn)))
        args.append(w)
    if res is not None:
        in_specs.append(pl.BlockSpec((bm, bn), o_map))
        args.append(res)
    return pl.pallas_call(
        functools.partial(_mm_kernel, n_w=n_w, epi=epi, has_res=res is not None),
        out_shape=jax.ShapeDtypeStruct((m, n_out), out_dtype),
        grid=grid,
        in_specs=in_specs,
        out_specs=pl.BlockSpec((bm, bn), o_map),
        compiler_params=_cparams(2),
        name=name,
    )(*args)


S5_GPT = V7X_LANES // S5_GROUP


def _s5_operators(lam_re, lam_im, log_step, b_re, b_im, c_re, c_im):
    g, p = lam_re.shape
    j = b_re.shape[-1]
    t = S5_BLOCK
    nl = g // S5_GPT
    hp = lax.Precision.HIGHEST
    step = jnp.exp(log_step.astype(F32))[:, None]
    lam_re = lam_re.astype(F32)
    lam_im = lam_im.astype(F32)
    lr = lam_re * step
    li = lam_im * step
    tau = jnp.arange(t + 1, dtype=F32)[:, None, None]
    mag = jnp.exp(tau * lr[None])
    e_re = mag * jnp.cos(tau * li[None])
    e_im = mag * jnp.sin(tau * li[None])
    lbar_re, lbar_im = e_re[1], e_im[1]
    denom = lam_re * lam_re + lam_im * lam_im
    num_re = lbar_re - 1.0
    coef_re = (num_re * lam_re + lbar_im * lam_im) / denom
    coef_im = (lbar_im * lam_re - num_re * lam_im) / denom
    bt_re = jnp.swapaxes(b_re.astype(F32), 1, 2)
    bt_im = jnp.swapaxes(b_im.astype(F32), 1, 2)
    bb_re = coef_re[:, None, :] * bt_re - coef_im[:, None, :] * bt_im
    bb_im = coef_re[:, None, :] * bt_im + coef_im[:, None, :] * bt_re
    c_re = c_re.astype(F32)
    c_im = c_im.astype(F32)
    eye = jnp.eye(S5_GPT, dtype=F32)

    a_re = e_re[:, :, None, :] * bb_re[None] - e_im[:, :, None, :] * bb_im[None]
    a_im = e_re[:, :, None, :] * bb_im[None] + e_im[:, :, None, :] * bb_re[None]

    al_re = a_re[:t].reshape(t, nl, S5_GPT * j, p)
    al_im = a_im[:t].reshape(t, nl, S5_GPT * j, p)
    cl_re = c_re.reshape(nl, S5_GPT * j, p)
    cl_im = c_im.reshape(nl, S5_GPT * j, p)
    lagfull = (jnp.einsum("tlip,lop->ltio", al_re, cl_re, precision=hp)
               - jnp.einsum("tlip,lop->ltio", al_im, cl_im, precision=hp))
    blockmask = jnp.kron(eye, jnp.ones((j, j), F32))
    bd = (lagfull * blockmask).astype(BF16)

    inj = jnp.stack([a_re[:t][::-1], a_im[:t][::-1]], axis=0)
    inj = inj.reshape(2, t, nl, S5_GPT, j, p)
    w_st = jnp.einsum("cslgip,gh->lsgichp", inj, eye).reshape(nl, t * S5_GPT * j, 2 * S5_GPT * p)

    ce_re = c_re[None] * e_re[1:, :, None, :] - c_im[None] * e_im[1:, :, None, :]
    ce_im = c_re[None] * e_im[1:, :, None, :] + c_im[None] * e_re[1:, :, None, :]
    ro = jnp.stack([ce_re, -ce_im], axis=0).reshape(2, t, nl, S5_GPT, j, p)
    w_out = jnp.einsum("ctlhop,hg->lchptgo", ro, eye).reshape(nl, 2 * S5_GPT * p, t * S5_GPT * j)

    a_pow = jnp.stack([e_re[t].reshape(nl, S5_GPT * p), e_im[t].reshape(nl, S5_GPT * p)], axis=1)
    return bd, w_st.astype(BF16), w_out.astype(BF16), a_pow


def _s5_kernel(x_ref, bd_ref, wst_ref, wout_ref, apow_ref, h0_ref, d_ref,
               z_ref, xfp_ref, xfs_ref,
               tbig_ref, ublk_ref, sst_ref, ybuf_ref, *,
               bp, sp, bs, ss, pitch):
    t = S5_BLOCK
    ln = V7X_LANES
    ncp = sp // t
    ncs = ss // t
    mp = bp * sp
    prow = bp * pitch
    rows = ublk_ref.shape[0]
    nslab = sst_ref.shape[0]
    half = nslab // 2

    for s in range(t):
        for tt in range(s, t):
            tbig_ref[s * ln:(s + 1) * ln, tt * ln:(tt + 1) * ln] = bd_ref[0, tt - s]
    for tt in range(0, t, 2):
        tbig_ref[(tt + 1) * ln:(tt + 2) * ln, tt * ln:(tt + 1) * ln] = jnp.zeros((ln, ln), BF16)

    def gather(b, _):
        r0 = pl.multiple_of(b * pitch, V7X_BF16_ROWS)
        for s in range(t):
            ublk_ref[pl.ds(r0, ncp), s * ln:(s + 1) * ln] = (
                x_ref[pl.ds(b * sp + s, ncp, stride=t), :].astype(BF16))
        ublk_ref[pl.ds(r0 + ncp, pitch - ncp), :] = jnp.zeros((pitch - ncp, t * ln), BF16)
        return 0

    lax.fori_loop(0, bp, gather, 0)
    for s in range(t):
        ublk_ref[prow:prow + bs * ncs, s * ln:(s + 1) * ln] = (
            x_ref[pl.ds(mp + s, bs * ncs, stride=t), :].astype(BF16))
    if rows > prow + bs * ncs:
        ublk_ref[prow + bs * ncs:rows, :] = jnp.zeros((rows - prow - bs * ncs, t * ln), BF16)

    rc = rows // 2
    for r0 in (0, rc):
        inj = jnp.dot(ublk_ref[r0:r0 + rc, :], wst_ref[0], preferred_element_type=F32)
        for k in range(nslab):
            sst_ref[k, r0:r0 + rc, :] = inj[:, k * ln:(k + 1) * ln]

    ar = [apow_ref[0, 0:1, k * ln:(k + 1) * ln] for k in range(half)]
    ai = [apow_ref[0, 1:2, k * ln:(k + 1) * ln] for k in range(half)]

    def advance(xr, xi, loc):
        nr, ni = [], []
        for k in range(half):
            s_r = sst_ref[k, loc, :]
            s_i = sst_ref[half + k, loc, :]
            sst_ref[k, loc, :] = xr[k]
            sst_ref[half + k, loc, :] = xi[k]
            nr.append(ar[k] * xr[k] - ai[k] * xi[k] + s_r)
            ni.append(ar[k] * xi[k] + ai[k] * xr[k] + s_i)
        return tuple(nr), tuple(ni)

    for b0 in range(0, bp, V7X_SUBLANES):
        def pstep(c, carry, b0=b0):
            return advance(carry[0], carry[1], pl.ds(b0 * pitch + c, V7X_SUBLANES, stride=pitch))

        zero = tuple(jnp.zeros((V7X_SUBLANES, ln), F32) for _ in range(half))
        xr, xi = lax.fori_loop(0, ncp, pstep, (zero, zero), unroll=2)
        for k in range(half):
            xfp_ref[0, b0:b0 + V7X_SUBLANES, k * ln:(k + 1) * ln] = xr[k]
            xfp_ref[0, b0:b0 + V7X_SUBLANES, (half + k) * ln:(half + k + 1) * ln] = xi[k]
    for b0 in range(0, bs, V7X_SUBLANES):
        xr = tuple(h0_ref[0, b0:b0 + V7X_SUBLANES, k * ln:(k + 1) * ln] for k in range(half))
        xi = tuple(h0_ref[0, b0:b0 + V7X_SUBLANES, (half + k) * ln:(half + k + 1) * ln] for k in range(half))
        for c in range(ncs):
            xr, xi = advance(xr, xi, pl.ds(prow + b0 * ncs + c, V7X_SUBLANES, stride=ncs))
        for k in range(half):
            xfs_ref[0, b0:b0 + V7X_SUBLANES, k * ln:(k + 1) * ln] = xr[k]
            xfs_ref[0, b0:b0 + V7X_SUBLANES, (half + k) * ln:(half + k + 1) * ln] = xi[k]

    d_skip = d_ref[...]
    npair = bp // 2
    for pr in range(npair):
        last = pr == npair - 1
        r0 = pr * 2 * pitch
        nr = (rows - r0) if last else 2 * pitch
        xs = jnp.concatenate([sst_ref[k, r0:r0 + nr, :] for k in range(nslab)], axis=1).astype(BF16)
        for tp in range(t * ln // V7X_MXU):
            c0 = tp * V7X_MXU
            kk = c0 + V7X_MXU
            yc = (jnp.dot(ublk_ref[r0:r0 + nr, 0:kk], tbig_ref[0:kk, c0:c0 + V7X_MXU],
                          preferred_element_type=F32)
                  + jnp.dot(xs, wout_ref[0, :, c0:c0 + V7X_MXU], preferred_element_type=F32))
            for sub in range(V7X_MXU // ln):
                tok = tp * (V7X_MXU // ln) + sub
                for q in range(2):
                    ybuf_ref[pl.ds(q * sp + tok, ncp, stride=t), :] = (
                        yc[q * pitch:q * pitch + ncp, sub * ln:(sub + 1) * ln])
                if last:
                    ybuf_ref[pl.ds(2 * sp + tok, bs * ncs, stride=t), :] = (
                        yc[2 * pitch:2 * pitch + bs * ncs, sub * ln:(sub + 1) * ln])
        tok0 = pr * 2 * sp
        ntok = 2 * sp + (bs * ss if last else 0)
        z_ref[tok0:tok0 + ntok, :] = _gelu_tanh(
            ybuf_ref[0:ntok, :] + d_skip * x_ref[tok0:tok0 + ntok, :]).astype(z_ref.dtype)


def s5_mixer_core(hn, ops, d_skip, h0_re, h0_im, *, bp, sp, bs, ss):
    bd, w_st, w_out, a_pow = ops
    m, d = hn.shape
    ln = V7X_LANES
    nl = d // ln
    t = S5_BLOCK
    p8 = a_pow.shape[-1]
    p = p8 // S5_GPT
    assert bp % V7X_SUBLANES == 0 and bs % V7X_SUBLANES == 0 and bp % 2 == 0
    assert sp % t == 0 and ss % t == 0 and m == bp * sp + bs * ss and (2 * p8) % ln == 0
    ncp, ncs = sp // t, ss // t
    pitch = ncp + V7X_BF16_ROWS - ncp % V7X_BF16_ROWS
    rows = bp * pitch + bs * ncs
    rows += (-rows) % (2 * V7X_BF16_ROWS)
    def slabbed(a):
        return a.astype(F32).reshape(bs, nl, p8).transpose(1, 0, 2)
    h0 = jnp.concatenate([slabbed(h0_re), slabbed(h0_im)], axis=-1)
    nslab = 2 * p8 // ln
    z, xfp, xfs = pl.pallas_call(
        functools.partial(_s5_kernel, bp=bp, sp=sp, bs=bs, ss=ss, pitch=pitch),
        out_shape=(jax.ShapeDtypeStruct((m, d), BF16),
                   jax.ShapeDtypeStruct((nl, bp, 2 * p8), F32),
                   jax.ShapeDtypeStruct((nl, bs, 2 * p8), F32)),
        grid=(nl,),
        in_specs=[pl.BlockSpec((m, ln), lambda l: (0, l), pipeline_mode=pl.Buffered(1)),
                  pl.BlockSpec((1, t, ln, ln), lambda l: (l, 0, 0, 0)),
                  pl.BlockSpec((1, t * ln, 2 * p8), lambda l: (l, 0, 0), pipeline_mode=pl.Buffered(1)),
                  pl.BlockSpec((1, 2 * p8, t * ln), lambda l: (l, 0, 0), pipeline_mode=pl.Buffered(1)),
                  pl.BlockSpec((1, 2, p8), lambda l: (l, 0, 0)),
                  pl.BlockSpec((1, bs, 2 * p8), lambda l: (l, 0, 0)),
                  pl.BlockSpec((1, ln), lambda l: (0, l))],
        out_specs=(pl.BlockSpec((m, ln), lambda l: (0, l)),
                   pl.BlockSpec((1, bp, 2 * p8), lambda l: (l, 0, 0)),
                   pl.BlockSpec((1, bs, 2 * p8), lambda l: (l, 0, 0))),
        scratch_shapes=[pltpu.VMEM((t * ln, t * ln), BF16),
                        pltpu.VMEM((rows, t * ln), BF16),
                        pltpu.VMEM((nslab, rows, ln), F32),
                        pltpu.VMEM((2 * sp + bs * ss, ln), F32)],
        compiler_params=_cparams(1),
        name="s5_core",
    )(hn, bd, w_st, w_out, a_pow, h0, d_skip.reshape(1, d).astype(F32))

    def unslab(a, nb):
        re = a[..., :p8].transpose(1, 0, 2).reshape(nb, nl * S5_GPT, p)
        im = a[..., p8:].transpose(1, 0, 2).reshape(nb, nl * S5_GPT, p)
        return re, im

    return (z,) + unslab(xfp, bp) + unslab(xfs, bs)


QBLK = 2 * CHUNK
KWIN = WINDOW + QBLK
BIAS_W = KWIN + WINDOW
BIAS_LEN = BIAS_W + QBLK


def _attn_bias_vectors(rel_bias):
    n = np.arange(BIAS_LEN)
    delta = np.where(n < BIAS_LEN - (QBLK - 1), n, n - BIAS_LEN)
    idx = np.clip(WINDOW - delta, -REL_CLIP, REL_CLIP) + REL_CLIP
    return rel_bias.astype(F32)[:, idx][:, None, :]


def _bias_tile(e_row, nq):
    return pltpu.roll(jnp.broadcast_to(e_row, (nq, BIAS_LEN)), 0, 1, stride=1, stride_axis=0)


def _attn_prompt_kernel(q_ref, k_ref, v_ref, e_ref, o_ref, bias_ref, *, seq):
    scale = HEAD_DIM ** -0.5
    bias_ref[...] = _bias_tile(e_ref[0], QBLK)
    for i in range(seq // QBLK):
        q0 = i * QBLK
        start = max(q0 - WINDOW, 0)
        kw = min(KWIN, seq - start)
        shift = start - (q0 - WINDOW)
        q = q_ref[q0:q0 + QBLK, :].astype(BF16)
        k = k_ref[start:start + kw, :].astype(BF16)
        v = v_ref[start:start + kw, :].astype(BF16)
        s = lax.dot_general(q, k, (((1,), (1,)), ((), ())), preferred_element_type=F32)
        s = s * scale + bias_ref[:, shift:shift + kw]
        qc = (q0 + lax.broadcasted_iota(jnp.int32, (QBLK, kw), 0)) // CHUNK
        kc = (start + lax.broadcasted_iota(jnp.int32, (QBLK, kw), 1)) // CHUNK
        s = jnp.where(kc <= qc, jnp.where(kc >= qc - BAND_CHUNKS, s, NEG_INF), NEG_INF)
        m = jnp.max(s, axis=-1, keepdims=True)
        e = jnp.exp(s - m)
        l = jnp.sum(e, axis=-1, keepdims=True)
        p = (e / l).astype(BF16)
        o = jnp.dot(p, v, preferred_element_type=F32)
        o_ref[q0:q0 + QBLK, :] = o.astype(o_ref.dtype)


def attn_prompt(qkv, bias_vec, *, bp, sp):
    m, d3 = qkv.shape
    d = d3 // 3
    nh = d // HEAD_DIM
    assert sp % QBLK == 0
    blk = lambda off: pl.BlockSpec((sp, HEAD_DIM), lambda b, h: (b, h + off))
    return pl.pallas_call(
        functools.partial(_attn_prompt_kernel, seq=sp),
        out_shape=jax.ShapeDtypeStruct((m, d), BF16),
        grid=(bp, nh),
        in_specs=[blk(0), blk(nh), blk(2 * nh),
                  pl.BlockSpec((1, 1, BIAS_LEN), lambda b, h: (h, 0, 0))],
        out_specs=pl.BlockSpec((sp, HEAD_DIM), lambda b, h: (b, h)),
        scratch_shapes=[pltpu.VMEM((QBLK, BIAS_LEN), F32)],
        compiler_params=_cparams(2),
        name="attn_prompt",
    )(qkv, qkv, qkv, bias_vec)


def _attn_sample_kernel(qkv_ref, ck_ref, cv_ref, e_ref, prev_ref, o_ref, *, seq, ncache, nh, mask):
    del prev_ref
    scale = HEAD_DIM ** -0.5
    d = nh * HEAD_DIM
    dn = (((1,), (1,)), ((), ()))
    off = WINDOW - ncache
    for h in range(nh):
        c0 = h * HEAD_DIM
        q = qkv_ref[:, c0:c0 + HEAD_DIM].astype(BF16)
        k = qkv_ref[:, d + c0:d + c0 + HEAD_DIM].astype(BF16)
        v = qkv_ref[:, 2 * d + c0:2 * d + c0 + HEAD_DIM].astype(BF16)
        ck = ck_ref[0, pl.ds(h, ncache, stride=nh), :].astype(BF16)
        cv = cv_ref[0, pl.ds(h, ncache, stride=nh), :].astype(BF16)
        bias = _bias_tile(e_ref[h], seq)
        s1 = lax.dot_general(q, ck, dn, preferred_element_type=F32) * scale + bias[:, off:off + ncache]
        s2 = lax.dot_general(q, k, dn, preferred_element_type=F32) * scale + bias[:, WINDOW:WINDOW + seq]
        if mask is not None:
            s1 = jnp.where(jnp.asarray(mask[:, :ncache]), s1, NEG_INF)
            s2 = jnp.where(jnp.asarray(mask[:, ncache:]), s2, NEG_INF)
        mx = jnp.maximum(jnp.max(s1, axis=-1, keepdims=True), jnp.max(s2, axis=-1, keepdims=True))
        e1 = jnp.exp(s1 - mx)
        e2 = jnp.exp(s2 - mx)
        l = jnp.sum(e1, axis=-1, keepdims=True) + jnp.sum(e2, axis=-1, keepdims=True)
        o = (jnp.dot((e1 / l).astype(BF16), cv, preferred_element_type=F32)
             + jnp.dot((e2 / l).astype(BF16), v, preferred_element_type=F32))
        o_ref[:, c0:c0 + HEAD_DIM] = o.astype(o_ref.dtype)


def attn_sample(qkv, cache_k, cache_v, bias_vec, o_prev, *, row0, bs, ss):
    m, d3 = qkv.shape
    d = d3 // 3
    nh = d // HEAD_DIM
    ncache = cache_k.shape[1]
    assert ss <= QBLK and ss % V7X_SUBLANES == 0 and ncache <= WINDOW and row0 % ss == 0
    q_pos = PAST_LEN + np.arange(ss)
    k_pos = PAST_LEN - ncache + np.arange(ncache + ss)
    qc, kc = q_pos // CHUNK, k_pos // CHUNK
    allowed = ((k_pos[None, :] >= 0) & (kc[None, :] <= qc[:, None])
               & (kc[None, :] >= qc[:, None] - BAND_CHUNKS))
    mask = None if allowed.all() else allowed
    ck = cache_k.reshape(bs, ncache * nh, HEAD_DIM)
    cv = cache_v.reshape(bs, ncache * nh, HEAD_DIM)
    blk0 = row0 // ss
    cblk = pl.BlockSpec((1, ncache * nh, HEAD_DIM), lambda b: (b, 0, 0))
    return pl.pallas_call(
        functools.partial(_attn_sample_kernel, seq=ss, ncache=ncache, nh=nh, mask=mask),
        out_shape=jax.ShapeDtypeStruct((m, d), BF16),
        grid=(bs,),
        in_specs=[pl.BlockSpec((ss, d3), lambda b: (b + blk0, 0)), cblk, cblk,
                  pl.BlockSpec((nh, 1, BIAS_LEN), lambda b: (0, 0, 0)),
                  pl.BlockSpec(memory_space=pl.ANY)],
        out_specs=pl.BlockSpec((ss, d), lambda b: (b + blk0, 0)),
        input_output_aliases={4: 0},
        compiler_params=_cparams(1),
        name="attn_sample",
    )(qkv, ck, cv, bias_vec, o_prev)


HGRN_HEADS_PER_STEP = 2


def _cumsum_rows(g):
    c = g.shape[0]
    row = lax.broadcasted_iota(jnp.int32, g.shape, 0)
    x = g
    for sh in (1, 2, 4):
        x = x + jnp.where((row & (V7X_SUBLANES - 1)) >= sh, pltpu.roll(x, sh, 0), 0.0)
    pieces = []
    carry = None
    for j in range(c // V7X_SUBLANES):
        blk = x[j * V7X_SUBLANES:(j + 1) * V7X_SUBLANES]
        if carry is not None:
            blk = blk + carry
        pieces.append(blk)
        carry = blk[V7X_SUBLANES - 1:V7X_SUBLANES]
    return jnp.concatenate(pieces, axis=0)


def _hgrn_kernel(*refs, seq, csz, has_state, aliased, hp):
    refs = list(refs)
    q_ref, f_ref, v_ref, z_ref, llb_ref, l1m_ref, gain_ref = refs[:7]
    pos = 7
    s0_ref = None
    if has_state:
        s0_ref = refs[pos]
        pos += 1
    if aliased:
        pos += 1
    o_ref, sn_ref, st_ref = refs[pos:pos + 3]
    kdim = HEAD_DIM
    scale = kdim ** -0.5
    dn_t = (((1,), (1,)), ((), ()))
    for hh in range(hp):
        if has_state:
            st_ref[hh] = s0_ref[0, hh].T
        else:
            st_ref[hh] = jnp.zeros((kdim, kdim), F32)
    row = lax.broadcasted_iota(jnp.int32, (csz, kdim), 0)
    tx = (lax.broadcasted_iota(jnp.int32, (csz, csz), 0)
          ^ lax.broadcasted_iota(jnp.int32, (csz, csz), 1))

    def one_head(r0, hh):
        sl = slice(hh * kdim, (hh + 1) * kdim)
        llb = llb_ref[:, sl]
        l1m = l1m_ref[:, sl]
        qz = q_ref[pl.ds(r0, csz), sl]
        fz = f_ref[pl.ds(r0, csz), sl]
        v = v_ref[pl.ds(r0, csz), sl]
        gz = z_ref[pl.ds(r0, csz), sl]
        ls = jnp.minimum(fz, 0.0) - jnp.log1p(jnp.exp(-jnp.abs(fz)))
        cc = l1m + ls
        g = jnp.maximum(llb, cc) + jnp.log1p(jnp.exp(-jnp.abs(llb - cc)))
        kk = jnp.exp(cc - fz)
        q = qz * _sigmoid(qz) * scale
        b = _cumsum_rows(g)
        v_bf = v.astype(BF16)
        sc = lax.dot_general(q.astype(BF16), kk.astype(BF16), dn_t, preferred_element_type=F32)
        scores = jnp.where(tx == 0, sc, 0.0)
        e = b
        w = 1
        while w < csz:
            bit = (row & w) != 0
            e_prev = pltpu.roll(e, w, 0)
            ex = jnp.exp(jnp.where(bit, b - e_prev, e - b))
            qw = jnp.where(bit, q * ex, 0.0).astype(BF16)
            kw = jnp.where(bit, 0.0, kk * ex).astype(BF16)
            sc = lax.dot_general(qw, kw, dn_t, preferred_element_type=F32)
            if 2 * w < csz:
                scores = scores + jnp.where(tx < 2 * w, sc, 0.0)
                e = jnp.where(bit, e, pltpu.roll(e, csz - w, 0))
            else:
                scores = scores + sc
            w *= 2
        o = jnp.dot(scores.astype(BF16), v_bf, preferred_element_type=F32)
        st = st_ref[hh]
        o = o + lax.dot_general((q * jnp.exp(b)).astype(BF16), st.astype(BF16), dn_t,
                                preferred_element_type=F32)
        b_last = b[csz - 1:csz]
        kd = (kk * jnp.exp(b_last - b)).astype(BF16)
        st_ref[hh] = jnp.exp(b_last) * st + lax.dot_general(
            v_bf, kd, (((0,), (0,)), ((), ())), preferred_element_type=F32)
        o = o * lax.rsqrt(jnp.mean(o * o, axis=-1, keepdims=True) + RMS_EPS)
        o = o * gain_ref[:, sl] * (gz * _sigmoid(gz))
        o_ref[pl.ds(r0, csz), sl] = o.astype(o_ref.dtype)

    def chunk(ci, _):
        r0 = pl.multiple_of(ci * csz, csz)
        for hh in range(hp):
            one_head(r0, hh)
        return 0

    nchunks = seq // csz
    if nchunks == 1:
        chunk(0, 0)
    else:
        lax.fori_loop(0, nchunks, chunk, 0)
    for hh in range(hp):
        sn_ref[0, hh] = st_ref[hh].T


def hgrn_core(proj, log_lb, log_1m_lb, norm_gain, state, o_prev, *, row0, nb, seq):
    m, d4 = proj.shape
    d = d4 // 4
    nh = d // HEAD_DIM
    hp = HGRN_HEADS_PER_STEP if nh % HGRN_HEADS_PER_STEP == 0 else 1
    ng = nh // hp
    w = hp * HEAD_DIM
    csz = _pick(seq, (128, 64, 32, 16, 8))
    assert row0 % seq == 0
    blk0 = row0 // seq
    has_state = state is not None
    aliased = o_prev is not None
    blk = lambda off: pl.BlockSpec((seq, w), lambda b, h: (b + blk0, h + off * ng))
    vec = pl.BlockSpec((1, w), lambda b, h: (0, h))
    sblk = pl.BlockSpec((1, hp, HEAD_DIM, HEAD_DIM), lambda b, h: (b, h, 0, 0))
    in_specs = [blk(0), blk(1), blk(2), blk(3), vec, vec, vec]
    args = [proj, proj, proj, proj, log_lb.reshape(1, d), log_1m_lb.reshape(1, d),
            norm_gain.reshape(1, d).astype(F32)]
    if has_state:
        in_specs.append(sblk)
        args.append(state.astype(F32))
    aliases = {}
    if aliased:
        aliases = {len(args): 0}
        in_specs.append(pl.BlockSpec(memory_space=pl.ANY))
        args.append(o_prev)
    return pl.pallas_call(
        functools.partial(_hgrn_kernel, seq=seq, csz=csz, has_state=has_state, aliased=aliased, hp=hp),
        out_shape=(jax.ShapeDtypeStruct((m, d), BF16),
                   jax.ShapeDtypeStruct((nb, nh, HEAD_DIM, HEAD_DIM), F32)),
        grid=(nb, ng),
        in_specs=in_specs,
        out_specs=(pl.BlockSpec((seq, w), lambda b, h: (b + blk0, h)), sblk),
        scratch_shapes=[pltpu.VMEM((hp, HEAD_DIM, HEAD_DIM), F32)],
        input_output_aliases=aliases,
        compiler_params=_cparams(2),
        name="hgrn_core",
    )(*args)


def kernel(x_prompt, x_sample, state_s5_re, state_s5_im, cache_attn_k, cache_attn_v, state_hgrn,
           norm_mixer, norm_ffn, norm_final, ffn_w_gate_up, ffn_w_down,
           s5_lambda_re, s5_lambda_im, s5_log_step, s5_b_re, s5_b_im, s5_c_re, s5_c_im, s5_d, s5_w_glu,
           attn_w_qkv, attn_rel_bias, attn_w_o,
           hgrn_w_in, hgrn_lower_bounds, hgrn_norm, hgrn_w_o):
    bp, sp, d = x_prompt.shape
    bs, ss, _ = x_sample.shape
    depth = norm_mixer.shape[0]
    mp = bp * sp
    ms = bs * ss
    m = mp + ms
    d_ff = ffn_w_down.shape[1]
    nh = d // HEAD_DIM

    x = jnp.concatenate([x_prompt.reshape(mp, d), x_sample.reshape(ms, d)], axis=0)
    bm = _pick(m, (1536, 768, 512, 256, 128, 64, 32, 16, 8))
    bm_glu = _pick(m, (768, 512, 256, 128, 64, 32, 16, 8))
    bm_down = _pick(m, (512, 256, 128, 64, 32, 16, 8))
    bn = _pick(d, (512, 256, 128))
    bn_ff = _pick(d_ff, (256, 128))

    w_gate_up = ffn_w_gate_up.astype(BF16)
    w_down = ffn_w_down.astype(BF16)
    w_glu = s5_w_glu.astype(BF16)
    w_qkv = attn_w_qkv.astype(BF16)
    w_ao = attn_w_o.astype(BF16)
    w_hin = hgrn_w_in.astype(BF16)
    w_ho = hgrn_w_o.astype(BF16)

    lbs = jnp.cumsum(jax.nn.softmax(hgrn_lower_bounds.astype(F32), axis=0), axis=0)
    lbs = lbs - lbs[0]

    new_re_p, new_im_p, new_re_s, new_im_s = [], [], [], []
    new_k_p, new_v_p, new_k_s, new_v_s = [], [], [], []
    new_h_p, new_h_s = [], []

    for layer in range(depth):
        kind = layer % N_MIXERS
        j = layer // N_MIXERS
        if kind == 0:
            hn = rmsnorm(x, norm_mixer[layer], F32)
            ops = _s5_operators(s5_lambda_re[j], s5_lambda_im[j], s5_log_step[j],
                                s5_b_re[j], s5_b_im[j], s5_c_re[j], s5_c_im[j])
            z, re_p, im_p, re_s, im_s = s5_mixer_core(
                hn, ops, s5_d[j], state_s5_re[j], state_s5_im[j], bp=bp, sp=sp, bs=bs, ss=ss)
            new_re_p.append(re_p)
            new_im_p.append(im_p)
            new_re_s.append(re_s)
            new_im_s.append(im_s)
            x = matmul(z, w_glu, j, n_out=d, w_col_offsets=(0, d), epi="glu_res",
                       out_dtype=F32, res=x, bm=bm_glu, bn=bn, name="s5_glu")
        elif kind == 1:
            h = rmsnorm(x, norm_mixer[layer], BF16)
            qkv = matmul(h, w_qkv, j, n_out=3 * d, w_col_offsets=(0,), epi="none",
                         out_dtype=F32, bm=bm, bn=bn, name="attn_qkv")
            bias_vec = _attn_bias_vectors(attn_rel_bias[j])
            o = attn_prompt(qkv, bias_vec, bp=bp, sp=sp)
            o = attn_sample(qkv, cache_attn_k[j], cache_attn_v[j], bias_vec, o, row0=mp, bs=bs, ss=ss)
            qkv_p = qkv[:mp].reshape(bp, sp, 3 * d)
            qkv_s = qkv[mp:].reshape(bs, ss, 3 * d)
            keep = min(WINDOW, sp)
            new_k_p.append(qkv_p[:, sp - keep:, d:2 * d].reshape(bp, keep, nh, HEAD_DIM))
            new_v_p.append(qkv_p[:, sp - keep:, 2 * d:].reshape(bp, keep, nh, HEAD_DIM))
            k_all = jnp.concatenate([cache_attn_k[j], qkv_s[:, :, d:2 * d].reshape(bs, ss, nh, HEAD_DIM)], axis=1)
            v_all = jnp.concatenate([cache_attn_v[j], qkv_s[:, :, 2 * d:].reshape(bs, ss, nh, HEAD_DIM)], axis=1)
            new_k_s.append(k_all[:, ss:])
            new_v_s.append(v_all[:, ss:])
            x = matmul(o, w_ao, j, n_out=d, w_col_offsets=(0,), epi="res",
                       out_dtype=F32, res=x, bm=bm, bn=bn, name="attn_wo")
        else:
            h = rmsnorm(x, norm_mixer[layer], BF16)
            proj = matmul(h, w_hin, j, n_out=4 * d, w_col_offsets=(0,), epi="none",
                          out_dtype=F32, bm=bm, bn=bn, name="hgrn_win")
            lb = lbs[layer]
            log_lb = jnp.log(lb)
            log_1m_lb = jnp.log1p(-lb)
            o, h_p = hgrn_core(proj, log_lb, log_1m_lb, hgrn_norm[j], None, None, row0=0, nb=bp, seq=sp)
            o, h_s = hgrn_core(proj, log_lb, log_1m_lb, hgrn_norm[j], state_hgrn[j], o, row0=mp, nb=bs, seq=ss)
            new_h_p.append(h_p)
            new_h_s.append(h_s)
            x = matmul(o, w_ho, j, n_out=d, w_col_offsets=(0,), epi="res",
                       out_dtype=F32, res=x, bm=bm, bn=bn, name="hgrn_wo")
        hf = rmsnorm(x, norm_ffn[layer], BF16)
        act = matmul(hf, w_gate_up, layer, n_out=d_ff, w_col_offsets=(0, d_ff),
                     epi="swiglu", out_dtype=BF16, bm=bm, bn=bn_ff, name="ffn_gate_up")
        x = matmul(act, w_down, layer, n_out=d, w_col_offsets=(0,), epi="res",
                   out_dtype=F32, res=x, bm=bm_down, bn=bn, w_resident=True, name="ffn_down")

    y_p = rmsnorm(x, norm_final, F32, 0, mp).reshape(bp, sp, d)
    y_s = rmsnorm(x, norm_final, F32, mp, ms).reshape(bs, ss, d)
    return (y_p, y_s,
            jnp.stack(new_re_p), jnp.stack(new_im_p),
            jnp.stack(new_k_p), jnp.stack(new_v_p), jnp.stack(new_h_p),
            jnp.stack(new_re_s), jnp.stack(new_im_s),
            jnp.stack(new_k_s), jnp.stack(new_v_s), jnp.stack(new_h_s))
```

```python
import functools
import math

import numpy as np
import jax
import jax.numpy as jnp
from jax import lax
from jax.experimental import pallas as pl
from jax.experimental.pallas import tpu as pltpu

F32 = jnp.float32
BF16 = jnp.bfloat16

PAST_LEN = 2048
CHUNK = 64
BAND_CHUNKS = 8
WINDOW = BAND_CHUNKS * CHUNK
REL_CLIP = 128
N_MIXERS = 3
S5_GROUP = 16
S5_BLOCK = 16
HEAD_DIM = 128
RMS_EPS = 1e-6
NEG_INF = -1e30

V7X_LANES = 128
V7X_SUBLANES = 8
V7X_BF16_ROWS = 16
V7X_MXU = 256
V7X_VMEM_LIMIT = 56 * 1024 * 1024


def _pick(n, candidates):
    for c in candidates:
        if n % c == 0:
            return c
    return n


def _cparams(n_axes):
    return pltpu.CompilerParams(
        dimension_semantics=("parallel",) * n_axes,
        vmem_limit_bytes=V7X_VMEM_LIMIT)


def _sigmoid(x):
    return 1.0 / (1.0 + jnp.exp(-x))


def _gelu_tanh(x):
    c = math.sqrt(2.0 / math.pi)
    return 0.5 * x * (1.0 + jnp.tanh(c * (x + 0.044715 * (x * x * x))))


def _rmsnorm_kernel(x_ref, g_ref, o_ref):
    x = x_ref[...]
    ms = jnp.mean(x * x, axis=-1, keepdims=True)
    o_ref[...] = (x * lax.rsqrt(ms + RMS_EPS) * g_ref[...]).astype(o_ref.dtype)


def rmsnorm(x, gain, out_dtype, row0=0, rows=None):
    m, d = x.shape
    rows = m if rows is None else rows
    bm = _pick(math.gcd(rows, row0) if row0 else rows, (256, 128, 64, 32, 16, 8))
    blk0 = row0 // bm
    return pl.pallas_call(
        _rmsnorm_kernel,
        out_shape=jax.ShapeDtypeStruct((rows, d), out_dtype),
        grid=(rows // bm,),
        in_specs=[pl.BlockSpec((bm, d), lambda i: (i + blk0, 0)),
                  pl.BlockSpec((1, d), lambda i: (0, 0))],
        out_specs=pl.BlockSpec((bm, d), lambda i: (i, 0)),
        compiler_params=_cparams(1),
        name="rmsnorm",
    )(x, gain.reshape(1, d).astype(F32))


def _mm_kernel(*refs, n_w, epi, has_res):
    a_ref = refs[0]
    w_refs = refs[1:1 + n_w]
    res_ref = refs[1 + n_w] if has_res else None
    o_ref = refs[1 + n_w + int(has_res)]
    a = a_ref[...]
    vals = [jnp.dot(a, w[...].astype(BF16), preferred_element_type=F32) for w in w_refs]
    if epi == "none":
        o = vals[0]
    elif epi == "swiglu":
        o = vals[0] * _sigmoid(vals[0]) * vals[1]
    elif epi == "res":
        o = res_ref[...] + vals[0]
    elif epi == "glu_res":
        o = res_ref[...] + vals[0] * _sigmoid(vals[1])
    else:
        raise ValueError(epi)
    o_ref[...] = o.astype(o_ref.dtype)


def matmul(a, w, layer, *, n_out, w_col_offsets, epi, out_dtype, res=None, bm, bn, w_resident=False, name):
    m, k = a.shape
    n_w = len(w_col_offsets)
    assert m % bm == 0 and n_out % bn == 0
    assert all(off % bn == 0 for off in w_col_offsets)
    if w_resident:
        grid = (n_out // bn, m // bm)
        ij = lambda g0, g1: (g1, g0)
    else:
        grid = (m // bm, n_out // bn)
        ij = lambda g0, g1: (g0, g1)

    def a_map(g0, g1):
        return (ij(g0, g1)[0], 0)

    def w_map(off_blocks):
        return lambda g0, g1: (layer, 0, ij(g0, g1)[1] + off_blocks)

    def o_map(g0, g1):
        return ij(g0, g1)

    in_specs = [pl.BlockSpec((bm, k), a_map)]
    args = [a]
    for off in w_col_offsets:
        in_specs.append(pl.BlockSpec((None, k, bn), w_map(off // bn)))
        args.append(w)
    if res is not None:
        in_specs.append(pl.BlockSpec((bm, bn), o_map))
        args.append(res)
    return pl.pallas_call(
        functools.partial(_mm_kernel, n_w=n_w, epi=epi, has_res=res is not None),
        out_shape=jax.ShapeDtypeStruct((m, n_out), out_dtype),
        grid=grid,
        in_specs=in_specs,
        out_specs=pl.BlockSpec((bm, bn), o_map),
        compiler_params=_cparams(2),
        name=name,
    )(*args)


S5_GPT = V7X_LANES // S5_GROUP


def _s5_operators(lam_re, lam_im, log_step, b_re, b_im, c_re, c_im):
    g, p = lam_re.shape
    j = b_re.shape[-1]
    t = S5_BLOCK
    nl = g // S5_GPT
    hp = lax.Precision.HIGHEST
    step = jnp.exp(log_step.astype(F32))[:, None]
    lam_re = lam_re.astype(F32)
    lam_im = lam_im.astype(F32)
    lr = lam_re * step
    li = lam_im * step
    tau = jnp.arange(t + 1, dtype=F32)[:, None, None]
    mag = jnp.exp(tau * lr[None])
    e_re = mag * jnp.cos(tau * li[None])
    e_im = mag * jnp.sin(tau * li[None])
    lbar_re, lbar_im = e_re[1], e_im[1]
    denom = lam_re * lam_re + lam_im * lam_im
    num_re = lbar_re - 1.0
    coef_re = (num_re * lam_re + lbar_im * lam_im) / denom
    coef_im = (lbar_im * lam_re - num_re * lam_im) / denom
    bt_re = jnp.swapaxes(b_re.astype(F32), 1, 2)
    bt_im = jnp.swapaxes(b_im.astype(F32), 1, 2)
    bb_re = coef_re[:, None, :] * bt_re - coef_im[:, None, :] * bt_im
    bb_im = coef_re[:, None, :] * bt_im + coef_im[:, None, :] * bt_re
    c_re = c_re.astype(F32)
    c_im = c_im.astype(F32)
    eye = jnp.eye(S5_GPT, dtype=F32)

    a_re = e_re[:, :, None, :] * bb_re[None] - e_im[:, :, None, :] * bb_im[None]
    a_im = e_re[:, :, None, :] * bb_im[None] + e_im[:, :, None, :] * bb_re[None]

    al_re = a_re[:t].reshape(t, nl, S5_GPT * j, p)
    al_im = a_im[:t].reshape(t, nl, S5_GPT * j, p)
    cl_re = c_re.reshape(nl, S5_GPT * j, p)
    cl_im = c_im.reshape(nl, S5_GPT * j, p)
    lagfull = (jnp.einsum("tlip,lop->ltio", al_re, cl_re, precision=hp)
               - jnp.einsum("tlip,lop->ltio", al_im, cl_im, precision=hp))
    blockmask = jnp.kron(eye, jnp.ones((j, j), F32))
    bd = (lagfull * blockmask).astype(BF16)

    place = np.zeros((S5_GPT, 2 * p, 2 * S5_GPT * p), np.float32)
    for g8 in range(S5_GPT):
        for c in range(2):
            place[g8, c * p + np.arange(p), c * S5_GPT * p + g8 * p + np.arange(p)] = 1.0
    place = jnp.asarray(place, BF16)

    def expand(src_re, src_im):
        src = jnp.concatenate([src_re, src_im], axis=-1).astype(BF16)
        src = src.reshape(t, nl, S5_GPT, j, 2 * p)
        out = jnp.einsum("tlgik,gkn->ltgin", src, place, preferred_element_type=BF16)
        return out.reshape(nl, t * S5_GPT * j, 2 * S5_GPT * p)

    w_st = expand(a_re[:t][::-1], a_im[:t][::-1])
    ce_re = c_re[None] * e_re[1:, :, None, :] - c_im[None] * e_im[1:, :, None, :]
    ce_im = c_re[None] * e_im[1:, :, None, :] + c_im[None] * e_re[1:, :, None, :]
    w_ro = expand(ce_re, -ce_im)

    a_pow = jnp.stack([e_re[t].reshape(nl, S5_GPT * p), e_im[t].reshape(nl, S5_GPT * p)], axis=1)
    return bd, w_st, w_ro, a_pow


def _s5_kernel(x_ref, bd_ref, wst_ref, wro_ref, apow_ref, h0_ref, d_ref,
               z_ref, xfp_ref, xfs_ref,
               tbig_ref, ublk_ref, sst_ref, ybuf_ref, *,
               bp, sp, bs, ss, pitch):
    t = S5_BLOCK
    ln = V7X_LANES
    ncp = sp // t
    ncs = ss // t
    mp = bp * sp
    prow = bp * pitch
    rows = ublk_ref.shape[0]
    nslab = sst_ref.shape[0]
    half = nslab // 2

    for s in range(t):
        for tt in range(s, t):
            tbig_ref[s * ln:(s + 1) * ln, tt * ln:(tt + 1) * ln] = bd_ref[0, tt - s]
    for tt in range(0, t, 2):
        tbig_ref[(tt + 1) * ln:(tt + 2) * ln, tt * ln:(tt + 1) * ln] = jnp.zeros((ln, ln), BF16)

    def gather(b, _):
        r0 = pl.multiple_of(b * pitch, V7X_BF16_ROWS)
        for s in range(t):
            ublk_ref[pl.ds(r0, ncp), s * ln:(s + 1) * ln] = (
                x_ref[pl.ds(b * sp + s, ncp, stride=t), :].astype(BF16))
        ublk_ref[pl.ds(r0 + ncp, pitch - ncp), :] = jnp.zeros((pitch - ncp, t * ln), BF16)
        return 0

    lax.fori_loop(0, bp, gather, 0)
    for s in range(t):
        ublk_ref[prow:prow + bs * ncs, s * ln:(s + 1) * ln] = (
            x_ref[pl.ds(mp + s, bs * ncs, stride=t), :].astype(BF16))
    if rows > prow + bs * ncs:
        ublk_ref[prow + bs * ncs:rows, :] = jnp.zeros((rows - prow - bs * ncs, t * ln), BF16)

    rc = rows // 2
    for r0 in (0, rc):
        inj = jnp.dot(ublk_ref[r0:r0 + rc, :], wst_ref[0], preferred_element_type=F32)
        for k in range(nslab):
            sst_ref[k, r0:r0 + rc, :] = inj[:, k * ln:(k + 1) * ln]

    ar = [apow_ref[0, 0:1, k * ln:(k + 1) * ln] for k in range(half)]
    ai = [apow_ref[0, 1:2, k * ln:(k + 1) * ln] for k in range(half)]

    def advance(xr, xi, loc):
        nr, ni = [], []
        for k in range(half):
            s_r = sst_ref[k, loc, :]
            s_i = sst_ref[half + k, loc, :]
            sst_ref[k, loc, :] = xr[k]
            sst_ref[half + k, loc, :] = xi[k]
            nr.append(ar[k] * xr[k] - ai[k] * xi[k] + s_r)
            ni.append(ar[k] * xi[k] + ai[k] * xr[k] + s_i)
        return tuple(nr), tuple(ni)

    for b0 in range(0, bp, V7X_SUBLANES):
        def pstep(c, carry, b0=b0):
            return advance(carry[0], carry[1], pl.ds(b0 * pitch + c, V7X_SUBLANES, stride=pitch))

        zero = tuple(jnp.zeros((V7X_SUBLANES, ln), F32) for _ in range(half))
        xr, xi = lax.fori_loop(0, ncp, pstep, (zero, zero), unroll=2)
        for k in range(half):
            xfp_ref[0, b0:b0 + V7X_SUBLANES, k * ln:(k + 1) * ln] = xr[k]
            xfp_ref[0, b0:b0 + V7X_SUBLANES, (half + k) * ln:(half + k + 1) * ln] = xi[k]
    for b0 in range(0, bs, V7X_SUBLANES):
        xr = tuple(h0_ref[0, b0:b0 + V7X_SUBLANES, k * ln:(k + 1) * ln] for k in range(half))
        xi = tuple(h0_ref[0, b0:b0 + V7X_SUBLANES, (half + k) * ln:(half + k + 1) * ln] for k in range(half))
        for c in range(ncs):
            xr, xi = advance(xr, xi, pl.ds(prow + b0 * ncs + c, V7X_SUBLANES, stride=ncs))
        for k in range(half):
            xfs_ref[0, b0:b0 + V7X_SUBLANES, k * ln:(k + 1) * ln] = xr[k]
            xfs_ref[0, b0:b0 + V7X_SUBLANES, (half + k) * ln:(half + k + 1) * ln] = xi[k]

    d_skip = d_ref[...]
    npair = bp // 2
    for pr in range(npair):
        last = pr == npair - 1
        r0 = pr * 2 * pitch
        nr = (rows - r0) if last else 2 * pitch
        xs = jnp.concatenate([sst_ref[k, r0:r0 + nr, :] for k in range(nslab)], axis=1).astype(BF16)
        for tp in range(t * ln // V7X_MXU):
            c0 = tp * V7X_MXU
            kk = c0 + V7X_MXU
            yc = (jnp.dot(ublk_ref[r0:r0 + nr, 0:kk], tbig_ref[0:kk, c0:c0 + V7X_MXU],
                          preferred_element_type=F32)
                  + lax.dot_general(xs, wro_ref[0, c0:c0 + V7X_MXU, :], (((1,), (1,)), ((), ())),
                                    preferred_element_type=F32))
            for sub in range(V7X_MXU // ln):
                tok = tp * (V7X_MXU // ln) + sub
                for q in range(2):
                    ybuf_ref[pl.ds(q * sp + tok, ncp, stride=t), :] = (
                        yc[q * pitch:q * pitch + ncp, sub * ln:(sub + 1) * ln])
                if last:
                    ybuf_ref[pl.ds(2 * sp + tok, bs * ncs, stride=t), :] = (
                        yc[2 * pitch:2 * pitch + bs * ncs, sub * ln:(sub + 1) * ln])
        tok0 = pr * 2 * sp
        ntok = 2 * sp + (bs * ss if last else 0)
        z_ref[tok0:tok0 + ntok, :] = _gelu_tanh(
            ybuf_ref[0:ntok, :] + d_skip * x_ref[tok0:tok0 + ntok, :]).astype(z_ref.dtype)


def s5_mixer_core(hn, ops, d_skip, h0_re, h0_im, *, bp, sp, bs, ss):
    bd, w_st, w_ro, a_pow = ops
    m, d = hn.shape
    ln = V7X_LANES
    nl = d // ln
    t = S5_BLOCK
    p8 = a_pow.shape[-1]
    p = p8 // S5_GPT
    assert bp % V7X_SUBLANES == 0 and bs % V7X_SUBLANES == 0 and bp % 2 == 0
    assert sp % t == 0 and ss % t == 0 and m == bp * sp + bs * ss and (2 * p8) % ln == 0
    ncp, ncs = sp // t, ss // t
    pitch = ncp + V7X_BF16_ROWS - ncp % V7X_BF16_ROWS
    rows = bp * pitch + bs * ncs
    rows += (-rows) % (2 * V7X_BF16_ROWS)
    def slabbed(a):
        return a.astype(F32).reshape(bs, nl, p8).transpose(1, 0, 2)
    h0 = jnp.concatenate([slabbed(h0_re), slabbed(h0_im)], axis=-1)
    nslab = 2 * p8 // ln
    z, xfp, xfs = pl.pallas_call(
        functools.partial(_s5_kernel, bp=bp, sp=sp, bs=bs, ss=ss, pitch=pitch),
        out_shape=(jax.ShapeDtypeStruct((m, d), BF16),
                   jax.ShapeDtypeStruct((nl, bp, 2 * p8), F32),
                   jax.ShapeDtypeStruct((nl, bs, 2 * p8), F32)),
        grid=(nl,),
        in_specs=[pl.BlockSpec((m, ln), lambda l: (0, l), pipeline_mode=pl.Buffered(1)),
                  pl.BlockSpec((1, t, ln, ln), lambda l: (l, 0, 0, 0)),
                  pl.BlockSpec((1, t * ln, 2 * p8), lambda l: (l, 0, 0), pipeline_mode=pl.Buffered(1)),
                  pl.BlockSpec((1, t * ln, 2 * p8), lambda l: (l, 0, 0), pipeline_mode=pl.Buffered(1)),
                  pl.BlockSpec((1, 2, p8), lambda l: (l, 0, 0)),
                  pl.BlockSpec((1, bs, 2 * p8), lambda l: (l, 0, 0)),
                  pl.BlockSpec((1, ln), lambda l: (0, l))],
        out_specs=(pl.BlockSpec((m, ln), lambda l: (0, l)),
                   pl.BlockSpec((1, bp, 2 * p8), lambda l: (l, 0, 0)),
                   pl.BlockSpec((1, bs, 2 * p8), lambda l: (l, 0, 0))),
        scratch_shapes=[pltpu.VMEM((t * ln, t * ln), BF16),
                        pltpu.VMEM((rows, t * ln), BF16),
                        pltpu.VMEM((nslab, rows, ln), F32),
                        pltpu.VMEM((2 * sp + bs * ss, ln), F32)],
        compiler_params=_cparams(1),
        name="s5_core",
    )(hn, bd, w_st, w_ro, a_pow, h0, d_skip.reshape(1, d).astype(F32))

    def unslab(a, nb):
        re = a[..., :p8].transpose(1, 0, 2).reshape(nb, nl * S5_GPT, p)
        im = a[..., p8:].transpose(1, 0, 2).reshape(nb, nl * S5_GPT, p)
        return re, im

    return (z,) + unslab(xfp, bp) + unslab(xfs, bs)


QBLK = 2 * CHUNK
KWIN = WINDOW + QBLK
BIAS_W = KWIN + WINDOW
BIAS_LEN = BIAS_W + QBLK
ATTN_GROUP = 4


def _attn_bias_vectors(rel_bias):
    n = np.arange(BIAS_LEN)
    delta = np.where(n < BIAS_LEN - (QBLK - 1), n, n - BIAS_LEN)
    idx = np.clip(WINDOW - delta, -REL_CLIP, REL_CLIP) + REL_CLIP
    return rel_bias.astype(F32)[:, idx][:, None, :]


def _bias_tile(e_row, nq):
    return pltpu.roll(jnp.broadcast_to(e_row, (nq, BIAS_LEN)), 0, 1, stride=1, stride_axis=0)


def _attn_prompt_kernel(q_ref, k_ref, v_ref, e_ref, o_ref, nk_ref, nv_ref, bias_ref, *, seq, keep):
    scale = HEAD_DIM ** -0.5
    qc = lax.broadcasted_iota(jnp.int32, (QBLK, BIAS_LEN), 0) // CHUNK
    kc = lax.broadcasted_iota(jnp.int32, (QBLK, BIAS_LEN), 1) // CHUNK
    tile = _bias_tile(e_ref[0], QBLK)
    bias_ref[...] = jnp.where(kc >= qc, jnp.where(kc <= qc + BAND_CHUNKS, tile, NEG_INF), NEG_INF)
    nblk = seq // QBLK
    for g0 in range(0, nblk, ATTN_GROUP):
        blocks = []
        for i in range(g0, min(g0 + ATTN_GROUP, nblk)):
            q0 = i * QBLK
            start = max(q0 - WINDOW, 0)
            blocks.append((q0, start, min(KWIN, seq - start), start - (q0 - WINDOW)))
        ss = []
        for q0, start, kw, shift in blocks:
            q = q_ref[q0:q0 + QBLK, :].astype(BF16)
            k = k_ref[start:start + kw, :].astype(BF16)
            s = lax.dot_general(q, k, (((1,), (1,)), ((), ())), preferred_element_type=F32)
            ss.append(s * scale + bias_ref[:, shift:shift + kw])
        ms = [jnp.max(s, axis=-1, keepdims=True) for s in ss]
        es = [jnp.exp(s - m) for s, m in zip(ss, ms)]
        ls = [jnp.sum(e, axis=-1, keepdims=True) for e in es]
        for (q0, start, kw, shift), e, l in zip(blocks, es, ls):
            v = v_ref[start:start + kw, :].astype(BF16)
            o = jnp.dot(e.astype(BF16), v, preferred_element_type=F32) * (1.0 / l)
            o_ref[q0:q0 + QBLK, :] = o.astype(o_ref.dtype)
    nk_ref[0] = k_ref[seq - keep:seq, :]
    nv_ref[0] = v_ref[seq - keep:seq, :]


def attn_prompt(qkv, bias_vec, *, bp, sp):
    m, d3 = qkv.shape
    d = d3 // 3
    nh = d // HEAD_DIM
    keep = min(WINDOW, sp)
    assert sp % QBLK == 0
    blk = lambda off: pl.BlockSpec((sp, HEAD_DIM), lambda b, h: (b, h + off))
    tail = pl.BlockSpec((1, keep, HEAD_DIM), lambda b, h: (b, 0, h))
    return pl.pallas_call(
        functools.partial(_attn_prompt_kernel, seq=sp, keep=keep),
        out_shape=(jax.ShapeDtypeStruct((m, d), BF16),
                   jax.ShapeDtypeStruct((bp, keep, d), F32),
                   jax.ShapeDtypeStruct((bp, keep, d), F32)),
        grid=(bp, nh),
        in_specs=[blk(0), blk(nh), blk(2 * nh),
                  pl.BlockSpec((1, 1, BIAS_LEN), lambda b, h: (h, 0, 0))],
        out_specs=(pl.BlockSpec((sp, HEAD_DIM), lambda b, h: (b, h)), tail, tail),
        scratch_shapes=[pltpu.VMEM((QBLK, BIAS_LEN), F32)],
        compiler_params=_cparams(2),
        name="attn_prompt",
    )(qkv, qkv, qkv, bias_vec)


def _attn_sample_kernel(qkv_ref, ck_ref, cv_ref, e_ref, prev_ref, o_ref, *, seq, ncache, nh, mask):
    del prev_ref
    scale = HEAD_DIM ** -0.5
    d = nh * HEAD_DIM
    dn = (((1,), (1,)), ((), ()))
    off = WINDOW - ncache
    for h in range(nh):
        c0 = h * HEAD_DIM
        q = qkv_ref[:, c0:c0 + HEAD_DIM].astype(BF16)
        k = qkv_ref[:, d + c0:d + c0 + HEAD_DIM].astype(BF16)
        v = qkv_ref[:, 2 * d + c0:2 * d + c0 + HEAD_DIM].astype(BF16)
        ck = ck_ref[0, pl.ds(h, ncache, stride=nh), :].astype(BF16)
        cv = cv_ref[0, pl.ds(h, ncache, stride=nh), :].astype(BF16)
        bias = _bias_tile(e_ref[h], seq)
        s1 = lax.dot_general(q, ck, dn, preferred_element_type=F32) * scale + bias[:, off:off + ncache]
        s2 = lax.dot_general(q, k, dn, preferred_element_type=F32) * scale + bias[:, WINDOW:WINDOW + seq]
        if mask is not None:
            s1 = jnp.where(jnp.asarray(mask[:, :ncache]), s1, NEG_INF)
            s2 = jnp.where(jnp.asarray(mask[:, ncache:]), s2, NEG_INF)
        mx = jnp.maximum(jnp.max(s1, axis=-1, keepdims=True), jnp.max(s2, axis=-1, keepdims=True))
        e1 = jnp.exp(s1 - mx)
        e2 = jnp.exp(s2 - mx)
        l = jnp.sum(e1, axis=-1, keepdims=True) + jnp.sum(e2, axis=-1, keepdims=True)
        o = (jnp.dot(e1.astype(BF16), cv, preferred_element_type=F32)
             + jnp.dot(e2.astype(BF16), v, preferred_element_type=F32)) * (1.0 / l)
        o_ref[:, c0:c0 + HEAD_DIM] = o.astype(o_ref.dtype)


def attn_sample(qkv, cache_k, cache_v, bias_vec, o_prev, *, row0, bs, ss):
    m, d3 = qkv.shape
    d = d3 // 3
    nh = d // HEAD_DIM
    ncache = cache_k.shape[1]
    assert ss <= QBLK and ss % V7X_SUBLANES == 0 and ncache <= WINDOW and row0 % ss == 0
    q_pos = PAST_LEN + np.arange(ss)
    k_pos = PAST_LEN - ncache + np.arange(ncache + ss)
    qc, kc = q_pos // CHUNK, k_pos // CHUNK
    allowed = ((k_pos[None, :] >= 0) & (kc[None, :] <= qc[:, None])
               & (kc[None, :] >= qc[:, None] - BAND_CHUNKS))
    mask = None if allowed.all() else allowed
    ck = cache_k.reshape(bs, ncache * nh, HEAD_DIM)
    cv = cache_v.reshape(bs, ncache * nh, HEAD_DIM)
    blk0 = row0 // ss
    cblk = pl.BlockSpec((1, ncache * nh, HEAD_DIM), lambda b: (b, 0, 0))
    return pl.pallas_call(
        functools.partial(_attn_sample_kernel, seq=ss, ncache=ncache, nh=nh, mask=mask),
        out_shape=jax.ShapeDtypeStruct((m, d), BF16),
        grid=(bs,),
        in_specs=[pl.BlockSpec((ss, d3), lambda b: (b + blk0, 0)), cblk, cblk,
                  pl.BlockSpec((nh, 1, BIAS_LEN), lambda b: (0, 0, 0)),
                  pl.BlockSpec(memory_space=pl.ANY)],
        out_specs=pl.BlockSpec((ss, d), lambda b: (b + blk0, 0)),
        input_output_aliases={4: 0},
        compiler_params=_cparams(1),
        name="attn_sample",
    )(qkv, ck, cv, bias_vec, o_prev)


HGRN_HEADS_PER_STEP = 2


def _cumsum_rows(g):
    c = g.shape[0]
    row = lax.broadcasted_iota(jnp.int32, g.shape, 0)
    x = g
    for sh in (1, 2, 4):
        x = x + jnp.where((row & (V7X_SUBLANES - 1)) >= sh, pltpu.roll(x, sh, 0), 0.0)
    pieces = []
    carry = None
    for j in range(c // V7X_SUBLANES):
        blk = x[j * V7X_SUBLANES:(j + 1) * V7X_SUBLANES]
        if carry is not None:
            blk = blk + carry
        pieces.append(blk)
        carry = blk[V7X_SUBLANES - 1:V7X_SUBLANES]
    return jnp.concatenate(pieces, axis=0)


def _hgrn_kernel(*refs, seq, csz, has_state, aliased, hp):
    refs = list(refs)
    q_ref, f_ref, v_ref, z_ref, llb_ref, l1m_ref, gain_ref = refs[:7]
    pos = 7
    s0_ref = None
    if has_state:
        s0_ref = refs[pos]
        pos += 1
    if aliased:
        pos += 1
    o_ref, sn_ref, st_ref = refs[pos:pos + 3]
    kdim = HEAD_DIM
    scale = kdim ** -0.5
    dn_t = (((1,), (1,)), ((), ()))
    for hh in range(hp):
        if has_state:
            st_ref[hh] = s0_ref[0, hh].T
        else:
            st_ref[hh] = jnp.zeros((kdim, kdim), F32)
    row = lax.broadcasted_iota(jnp.int32, (csz, kdim), 0)
    tx = (lax.broadcasted_iota(jnp.int32, (csz, csz), 0)
          ^ lax.broadcasted_iota(jnp.int32, (csz, csz), 1))

    def one_head(r0, hh):
        sl = slice(hh * kdim, (hh + 1) * kdim)
        llb = llb_ref[:, sl]
        l1m = l1m_ref[:, sl]
        qz = q_ref[pl.ds(r0, csz), sl]
        fz = f_ref[pl.ds(r0, csz), sl]
        v = v_ref[pl.ds(r0, csz), sl]
        gz = z_ref[pl.ds(r0, csz), sl]
        ls = jnp.minimum(fz, 0.0) - jnp.log(1.0 + jnp.exp(-jnp.abs(fz)))
        cc = l1m + ls
        g = jnp.maximum(llb, cc) + jnp.log(1.0 + jnp.exp(-jnp.abs(llb - cc)))
        kk = jnp.exp(cc - fz)
        q = qz * _sigmoid(qz) * scale
        b = _cumsum_rows(g)
        v_bf = v.astype(BF16)
        sc = lax.dot_general(q.astype(BF16), kk.astype(BF16), dn_t, preferred_element_type=F32)
        scores = jnp.where(tx == 0, sc, 0.0)
        e = b
        w = 1
        while w < min(V7X_SUBLANES, csz):
            bit = (row & w) != 0
            e_prev = pltpu.roll(e, w, 0)
            ex = jnp.exp(jnp.where(bit, b - e_prev, e - b))
            qw = jnp.where(bit, q * ex, 0.0).astype(BF16)
            kw = jnp.where(bit, 0.0, kk * ex).astype(BF16)
            sc = lax.dot_general(qw, kw, dn_t, preferred_element_type=F32)
            scores = scores + jnp.where(tx < 2 * w, sc, 0.0)
            if 2 * w < V7X_SUBLANES:
                e = jnp.where(bit, e, pltpu.roll(e, csz - w, 0))
            w *= 2
        while w < csz:
            n2 = csz // (2 * w)
            b4 = b.reshape(n2, 2, w, kdim)
            bound = b4[:, 0, w - 1:w, :]
            kpart = kk.reshape(n2, 2, w, kdim)[:, 0] * jnp.exp(bound - b4[:, 0])
            qpart = q.reshape(n2, 2, w, kdim)[:, 1] * jnp.exp(b4[:, 1] - bound)
            zero = jnp.zeros((n2, 1, w, kdim), F32)
            kw = jnp.concatenate([kpart[:, None], zero], axis=1).reshape(csz, kdim).astype(BF16)
            qw = jnp.concatenate([zero, qpart[:, None]], axis=1).reshape(csz, kdim).astype(BF16)
            sc = lax.dot_general(qw, kw, dn_t, preferred_element_type=F32)
            if 2 * w < csz:
                scores = scores + jnp.where(tx < 2 * w, sc, 0.0)
            else:
                scores = scores + sc
            w *= 2
        o = jnp.dot(scores.astype(BF16), v_bf, preferred_element_type=F32)
        st = st_ref[hh]
        o = o + lax.dot_general((q * jnp.exp(b)).astype(BF16), st.astype(BF16), dn_t,
                                preferred_element_type=F32)
        b_last = b[csz - 1:csz]
        kd = (kk * jnp.exp(b_last - b)).astype(BF16)
        st_ref[hh] = jnp.exp(b_last) * st + lax.dot_general(
            v_bf, kd, (((0,), (0,)), ((), ())), preferred_element_type=F32)
        o = o * lax.rsqrt(jnp.mean(o * o, axis=-1, keepdims=True) + RMS_EPS)
        o = o * gain_ref[:, sl] * (gz * _sigmoid(gz))
        o_ref[pl.ds(r0, csz), sl] = o.astype(o_ref.dtype)

    def chunk(ci, _):
        r0 = pl.multiple_of(ci * csz, csz)
        for hh in range(hp):
            one_head(r0, hh)
        return 0

    nchunks = seq // csz
    if nchunks == 1:
        chunk(0, 0)
    else:
        lax.fori_loop(0, nchunks, chunk, 0)
    for hh in range(hp):
        sn_ref[0, hh] = st_ref[hh].T


def hgrn_core(proj, log_lb, log_1m_lb, norm_gain, state, o_prev, *, row0, nb, seq):
    m, d4 = proj.shape
    d = d4 // 4
    nh = d // HEAD_DIM
    hp = HGRN_HEADS_PER_STEP if nh % HGRN_HEADS_PER_STEP == 0 else 1
    ng = nh // hp
    w = hp * HEAD_DIM
    csz = _pick(seq, (128, 64, 32, 16, 8))
    assert row0 % seq == 0
    blk0 = row0 // seq
    has_state = state is not None
    aliased = o_prev is not None
    blk = lambda off: pl.BlockSpec((seq, w), lambda b, h: (b + blk0, h + off * ng))
    vec = pl.BlockSpec((1, w), lambda b, h: (0, h))
    sblk = pl.BlockSpec((1, hp, HEAD_DIM, HEAD_DIM), lambda b, h: (b, h, 0, 0))
    in_specs = [blk(0), blk(1), blk(2), blk(3), vec, vec, vec]
    args = [proj, proj, proj, proj, log_lb.reshape(1, d), log_1m_lb.reshape(1, d),
            norm_gain.reshape(1, d).astype(F32)]
    if has_state:
        in_specs.append(sblk)
        args.append(state.astype(F32))
    aliases = {}
    if aliased:
        aliases = {len(args): 0}
        in_specs.append(pl.BlockSpec(memory_space=pl.ANY))
        args.append(o_prev)
    return pl.pallas_call(
        functools.partial(_hgrn_kernel, seq=seq, csz=csz, has_state=has_state, aliased=aliased, hp=hp),
        out_shape=(jax.ShapeDtypeStruct((m, d), BF16),
                   jax.ShapeDtypeStruct((nb, nh, HEAD_DIM, HEAD_DIM), F32)),
        grid=(nb, ng),
        in_specs=in_specs,
        out_specs=(pl.BlockSpec((seq, w), lambda b, h: (b + blk0, h)), sblk),
        scratch_shapes=[pltpu.VMEM((hp, HEAD_DIM, HEAD_DIM), F32)],
        input_output_aliases=aliases,
        compiler_params=_cparams(2),
        name="hgrn_core",
    )(*args)


def kernel(x_prompt, x_sample, state_s5_re, state_s5_im, cache_attn_k, cache_attn_v, state_hgrn,
           norm_mixer, norm_ffn, norm_final, ffn_w_gate_up, ffn_w_down,
           s5_lambda_re, s5_lambda_im, s5_log_step, s5_b_re, s5_b_im, s5_c_re, s5_c_im, s5_d, s5_w_glu,
           attn_w_qkv, attn_rel_bias, attn_w_o,
           hgrn_w_in, hgrn_lower_bounds, hgrn_norm, hgrn_w_o):
    bp, sp, d = x_prompt.shape
    bs, ss, _ = x_sample.shape
    depth = norm_mixer.shape[0]
    mp = bp * sp
    ms = bs * ss
    m = mp + ms
    d_ff = ffn_w_down.shape[1]
    nh = d // HEAD_DIM

    x = jnp.concatenate([x_prompt.reshape(mp, d), x_sample.reshape(ms, d)], axis=0)
    bm = _pick(m, (1536, 768, 512, 256, 128, 64, 32, 16, 8))
    bm_glu = _pick(m, (768, 512, 256, 128, 64, 32, 16, 8))
    bm_down = _pick(m, (512, 256, 128, 64, 32, 16, 8))
    bn = _pick(d, (256, 128))
    bn_down = _pick(d, (512, 256, 128))
    bn_ff = _pick(d_ff, (256, 128))

    w_gate_up = ffn_w_gate_up
    w_down = ffn_w_down.astype(BF16)
    w_glu = s5_w_glu
    w_qkv = attn_w_qkv
    w_ao = attn_w_o
    w_hin = hgrn_w_in
    w_ho = hgrn_w_o

    lbs = jnp.cumsum(jax.nn.softmax(hgrn_lower_bounds.astype(F32), axis=0), axis=0)
    lbs = lbs - lbs[0]

    new_re_p, new_im_p, new_re_s, new_im_s = [], [], [], []
    new_k_p, new_v_p, new_k_s, new_v_s = [], [], [], []
    new_h_p, new_h_s = [], []

    for layer in range(depth):
        kind = layer % N_MIXERS
        j = layer // N_MIXERS
        if kind == 0:
            hn = rmsnorm(x, norm_mixer[layer], F32)
            ops = _s5_operators(s5_lambda_re[j], s5_lambda_im[j], s5_log_step[j],
                                s5_b_re[j], s5_b_im[j], s5_c_re[j], s5_c_im[j])
            z, re_p, im_p, re_s, im_s = s5_mixer_core(
                hn, ops, s5_d[j], state_s5_re[j], state_s5_im[j], bp=bp, sp=sp, bs=bs, ss=ss)
            new_re_p.append(re_p)
            new_im_p.append(im_p)
            new_re_s.append(re_s)
            new_im_s.append(im_s)
            x = matmul(z, w_glu, j, n_out=d, w_col_offsets=(0, d), epi="glu_res",
                       out_dtype=F32, res=x, bm=bm_glu, bn=bn, name="s5_glu")
        elif kind == 1:
            h = rmsnorm(x, norm_mixer[layer], BF16)
            qkv = matmul(h, w_qkv, j, n_out=3 * d, w_col_offsets=(0,), epi="none",
                         out_dtype=F32, bm=bm, bn=bn, name="attn_qkv")
            bias_vec = _attn_bias_vectors(attn_rel_bias[j])
            o, nk_p, nv_p = attn_prompt(qkv, bias_vec, bp=bp, sp=sp)
            o = attn_sample(qkv, cache_attn_k[j], cache_attn_v[j], bias_vec, o, row0=mp, bs=bs, ss=ss)
            qkv_s = qkv[mp:].reshape(bs, ss, 3 * d)
            new_k_p.append(nk_p.reshape(bp, -1, nh, HEAD_DIM))
            new_v_p.append(nv_p.reshape(bp, -1, nh, HEAD_DIM))
            k_all = jnp.concatenate([cache_attn_k[j], qkv_s[:, :, d:2 * d].reshape(bs, ss, nh, HEAD_DIM)], axis=1)
            v_all = jnp.concatenate([cache_attn_v[j], qkv_s[:, :, 2 * d:].reshape(bs, ss, nh, HEAD_DIM)], axis=1)
            new_k_s.append(k_all[:, ss:])
            new_v_s.append(v_all[:, ss:])
            x = matmul(o, w_ao, j, n_out=d, w_col_offsets=(0,), epi="res",
                       out_dtype=F32, res=x, bm=bm, bn=bn, name="attn_wo")
        else:
            h = rmsnorm(x, norm_mixer[layer], BF16)
            proj = matmul(h, w_hin, j, n_out=4 * d, w_col_offsets=(0,), epi="none",
                          out_dtype=F32, bm=bm, bn=bn, name="hgrn_win")
            lb = lbs[layer]
            log_lb = jnp.log(lb)
            log_1m_lb = jnp.log1p(-lb)
            o, h_p = hgrn_core(proj, log_lb, log_1m_lb, hgrn_norm[j], None, None, row0=0, nb=bp, seq=sp)
            o, h_s = hgrn_core(proj, log_lb, log_1m_lb, hgrn_norm[j], state_hgrn[j], o, row0=mp, nb=bs, seq=ss)
            new_h_p.append(h_p)
            new_h_s.append(h_s)
            x = matmul(o, w_ho, j, n_out=d, w_col_offsets=(0,), epi="res",
                       out_dtype=F32, res=x, bm=bm, bn=bn, name="hgrn_wo")
        hf = rmsnorm(x, norm_ffn[layer], BF16)
        act = matmul(hf, w_gate_up, layer, n_out=d_ff, w_col_offsets=(0, d_ff),
                     epi="swiglu", out_dtype=BF16, bm=bm, bn=bn_ff, name="ffn_gate_up")
        x = matmul(act, w_down, layer, n_out=d, w_col_offsets=(0,), epi="res",
                   out_dtype=F32, res=x, bm=bm_down, bn=bn_down, w_resident=True, name="ffn_down")

    y_p = rmsnorm(x, norm_final, F32, 0, mp).reshape(bp, sp, d)
    y_s = rmsnorm(x, norm_final, F32, mp, ms).reshape(bs, ss, d)
    return (y_p, y_s,
            jnp.stack(new_re_p), jnp.stack(new_im_p),
            jnp.stack(new_k_p), jnp.stack(new_v_p), jnp.stack(new_h_p),
            jnp.stack(new_re_s), jnp.stack(new_im_s),
            jnp.stack(new_k_s), jnp.stack(new_v_s), jnp.stack(new_h_s))
```

```python
import functools
import math

import numpy as np
import jax
import jax.numpy as jnp
from jax import lax
from jax.experimental import pallas as pl
from jax.experimental.pallas import tpu as pltpu

F32 = jnp.float32
BF16 = jnp.bfloat16

PAST_LEN = 2048
CHUNK = 64
BAND_CHUNKS = 8
WINDOW = BAND_CHUNKS * CHUNK
REL_CLIP = 128
N_MIXERS = 3
S5_GROUP = 16
S5_BLOCK = 16
HEAD_DIM = 128
RMS_EPS = 1e-6
NEG_INF = -1e30

V7X_LANES = 128
V7X_SUBLANES = 8
V7X_BF16_ROWS = 16
V7X_MXU = 256
V7X_VMEM_LIMIT = 56 * 1024 * 1024


def _pick(n, candidates):
    for c in candidates:
        if n % c == 0:
            return c
    return n


def _cparams(n_axes):
    return pltpu.CompilerParams(
        dimension_semantics=("parallel",) * n_axes,
        vmem_limit_bytes=V7X_VMEM_LIMIT)


def _sigmoid(x):
    return 1.0 / (1.0 + jnp.exp(-x))


def _gelu_tanh(x):
    c = math.sqrt(2.0 / math.pi)
    return 0.5 * x * (1.0 + jnp.tanh(c * (x + 0.044715 * (x * x * x))))


def _rmsnorm_kernel(x_ref, g_ref, o_ref):
    x = x_ref[...]
    ms = jnp.mean(x * x, axis=-1, keepdims=True)
    o_ref[...] = (x * lax.rsqrt(ms + RMS_EPS) * g_ref[...]).astype(o_ref.dtype)


def rmsnorm(x, gain, out_dtype, row0=0, rows=None):
    m, d = x.shape
    rows = m if rows is None else rows
    bm = _pick(math.gcd(rows, row0) if row0 else rows, (256, 128, 64, 32, 16, 8))
    blk0 = row0 // bm
    return pl.pallas_call(
        _rmsnorm_kernel,
        out_shape=jax.ShapeDtypeStruct((rows, d), out_dtype),
        grid=(rows // bm,),
        in_specs=[pl.BlockSpec((bm, d), lambda i: (i + blk0, 0)),
                  pl.BlockSpec((1, d), lambda i: (0, 0))],
        out_specs=pl.BlockSpec((bm, d), lambda i: (i, 0)),
        compiler_params=_cparams(1),
        name="rmsnorm",
    )(x, gain.reshape(1, d).astype(F32))


def _mm_kernel(*refs, n_w, epi, has_res, pre_norm, post_norm, k_dim):
    refs = list(refs)
    a_ref = refs.pop(0)
    w_refs = [refs.pop(0) for _ in range(n_w)]
    res_ref = refs.pop(0) if has_res else None
    ssq_in_ref = refs.pop(0) if pre_norm else None
    gain_ref = refs.pop(0) if post_norm else None
    o_ref = refs.pop(0)
    a = a_ref[...]
    vals = [jnp.dot(a, w[...].astype(BF16), preferred_element_type=F32) for w in w_refs]
    if pre_norm:
        rstd = lax.rsqrt(ssq_in_ref[...] * (1.0 / k_dim) + RMS_EPS)
        rstd = jnp.concatenate([rstd] * (vals[0].shape[1] // V7X_LANES), axis=1)
        vals = [v * rstd for v in vals]
    if epi == "none":
        o = vals[0]
    elif epi == "swiglu":
        o = vals[0] * _sigmoid(vals[0]) * vals[1]
    elif epi == "res":
        o = res_ref[...] + vals[0]
    elif epi == "glu_res":
        o = res_ref[...] + vals[0] * _sigmoid(vals[1])
    else:
        raise ValueError(epi)
    o_ref[...] = o.astype(o_ref.dtype)
    if post_norm:
        xg_ref, ssq_ref = refs
        xg_ref[...] = (o * gain_ref[...]).astype(xg_ref.dtype)
        part = jnp.broadcast_to(jnp.sum(o * o, axis=1, keepdims=True), ssq_ref.shape)

        @pl.when(pl.program_id(1) == 0)
        def _():
            ssq_ref[...] = part

        @pl.when(pl.program_id(1) != 0)
        def _():
            ssq_ref[...] += part


def matmul(a, w, layer, *, n_out, w_col_offsets, epi, out_dtype, res=None, bm, bn, w_resident=False,
           ssq=None, next_gain=None, name):
    m, k = a.shape
    n_w = len(w_col_offsets)
    post_norm = next_gain is not None
    assert m % bm == 0 and n_out % bn == 0 and bn % V7X_LANES == 0
    assert all(off % bn == 0 for off in w_col_offsets)
    assert not (w_resident and post_norm)
    if w_resident:
        grid = (n_out // bn, m // bm)
        ij = lambda g0, g1: (g1, g0)
    else:
        grid = (m // bm, n_out // bn)
        ij = lambda g0, g1: (g0, g1)

    def a_map(g0, g1):
        return (ij(g0, g1)[0], 0)

    def w_map(off_blocks):
        return lambda g0, g1: (layer, 0, ij(g0, g1)[1] + off_blocks)

    def o_map(g0, g1):
        return ij(g0, g1)

    in_specs = [pl.BlockSpec((bm, k), a_map)]
    args = [a]
    for off in w_col_offsets:
        in_specs.append(pl.BlockSpec((None, k, bn), w_map(off // bn)))
        args.append(w)
    if res is not None:
        in_specs.append(pl.BlockSpec((bm, bn), o_map))
        args.append(res)
    if ssq is not None:
        in_specs.append(pl.BlockSpec((bm, V7X_LANES), a_map))
        args.append(ssq)
    out_shape = [jax.ShapeDtypeStruct((m, n_out), out_dtype)]
    out_specs = [pl.BlockSpec((bm, bn), o_map)]
    if post_norm:
        in_specs.append(pl.BlockSpec((1, bn), lambda g0, g1: (0, ij(g0, g1)[1])))
        args.append(next_gain.reshape(1, n_out).astype(F32))
        out_shape += [jax.ShapeDtypeStruct((m, n_out), BF16), jax.ShapeDtypeStruct((m, V7X_LANES), F32)]
        out_specs += [pl.BlockSpec((bm, bn), o_map), pl.BlockSpec((bm, V7X_LANES), a_map)]
    sem = ("parallel", "arbitrary") if post_norm else ("parallel", "parallel")
    out = pl.pallas_call(
        functools.partial(_mm_kernel, n_w=n_w, epi=epi, has_res=res is not None,
                          pre_norm=ssq is not None, post_norm=post_norm, k_dim=k),
        out_shape=tuple(out_shape),
        grid=grid,
        in_specs=in_specs,
        out_specs=tuple(out_specs),
        compiler_params=pltpu.CompilerParams(dimension_semantics=sem, vmem_limit_bytes=V7X_VMEM_LIMIT),
        name=name,
    )(*args)
    return out if post_norm else out[0]


S5_GPT = V7X_LANES // S5_GROUP


def _s5_operators(lam_re, lam_im, log_step, b_re, b_im, c_re, c_im):
    g, p = lam_re.shape
    j = b_re.shape[-1]
    t = S5_BLOCK
    nl = g // S5_GPT
    hp = lax.Precision.HIGHEST
    step = jnp.exp(log_step.astype(F32))[:, None]
    lam_re = lam_re.astype(F32)
    lam_im = lam_im.astype(F32)
    lr = lam_re * step
    li = lam_im * step
    tau = jnp.arange(t + 1, dtype=F32)[:, None, None]
    mag = jnp.exp(tau * lr[None])
    e_re = mag * jnp.cos(tau * li[None])
    e_im = mag * jnp.sin(tau * li[None])
    lbar_re, lbar_im = e_re[1], e_im[1]
    denom = lam_re * lam_re + lam_im * lam_im
    num_re = lbar_re - 1.0
    coef_re = (num_re * lam_re + lbar_im * lam_im) / denom
    coef_im = (lbar_im * lam_re - num_re * lam_im) / denom
    bt_re = jnp.swapaxes(b_re.astype(F32), 1, 2)
    bt_im = jnp.swapaxes(b_im.astype(F32), 1, 2)
    bb_re = coef_re[:, None, :] * bt_re - coef_im[:, None, :] * bt_im
    bb_im = coef_re[:, None, :] * bt_im + coef_im[:, None, :] * bt_re
    c_re = c_re.astype(F32)
    c_im = c_im.astype(F32)
    eye = jnp.eye(S5_GPT, dtype=F32)

    a_re = e_re[:, :, None, :] * bb_re[None] - e_im[:, :, None, :] * bb_im[None]
    a_im = e_re[:, :, None, :] * bb_im[None] + e_im[:, :, None, :] * bb_re[None]

    al_re = a_re[:t].reshape(t, nl, S5_GPT * j, p)
    al_im = a_im[:t].reshape(t, nl, S5_GPT * j, p)
    cl_re = c_re.reshape(nl, S5_GPT * j, p)
    cl_im = c_im.reshape(nl, S5_GPT * j, p)
    lagfull = (jnp.einsum("tlip,lop->ltio", al_re, cl_re, precision=hp)
               - jnp.einsum("tlip,lop->ltio", al_im, cl_im, precision=hp))
    blockmask = jnp.kron(eye, jnp.ones((j, j), F32))
    bd = (lagfull * blockmask).astype(BF16)

    place = np.zeros((S5_GPT, 2 * p, 2 * S5_GPT * p), np.float32)
    for g8 in range(S5_GPT):
        for c in range(2):
            place[g8, c * p + np.arange(p), c * S5_GPT * p + g8 * p + np.arange(p)] = 1.0
    place = jnp.asarray(place, BF16)

    def expand(src_re, src_im):
        src = jnp.concatenate([src_re, src_im], axis=-1).astype(BF16)
        src = src.reshape(t, nl, S5_GPT, j, 2 * p)
        out = jnp.einsum("tlgik,gkn->ltgin", src, place, preferred_element_type=BF16)
        return out.reshape(nl, t * S5_GPT * j, 2 * S5_GPT * p)

    w_st = expand(a_re[:t][::-1], a_im[:t][::-1])
    ce_re = c_re[None] * e_re[1:, :, None, :] - c_im[None] * e_im[1:, :, None, :]
    ce_im = c_re[None] * e_im[1:, :, None, :] + c_im[None] * e_re[1:, :, None, :]
    w_ro = expand(ce_re, -ce_im)

    a_pow = jnp.stack([e_re[t].reshape(nl, S5_GPT * p), e_im[t].reshape(nl, S5_GPT * p)], axis=1)
    return bd, w_st, w_ro, a_pow


def _s5_kernel(x_ref, bd_ref, wst_ref, wro_ref, apow_ref, h0_ref, d_ref,
               z_ref, xfp_ref, xfs_ref,
               tbig_ref, ublk_ref, sst_ref, ybuf_ref, *,
               bp, sp, bs, ss, pitch):
    t = S5_BLOCK
    ln = V7X_LANES
    ncp = sp // t
    ncs = ss // t
    mp = bp * sp
    prow = bp * pitch
    rows = ublk_ref.shape[0]
    nslab = sst_ref.shape[0]
    half = nslab // 2

    for s in range(t):
        for tt in range(s, t):
            tbig_ref[s * ln:(s + 1) * ln, tt * ln:(tt + 1) * ln] = bd_ref[0, tt - s]
    for tt in range(0, t, 2):
        tbig_ref[(tt + 1) * ln:(tt + 2) * ln, tt * ln:(tt + 1) * ln] = jnp.zeros((ln, ln), BF16)

    def gather(b, _):
        r0 = pl.multiple_of(b * pitch, V7X_BF16_ROWS)
        for s in range(t):
            ublk_ref[pl.ds(r0, ncp), s * ln:(s + 1) * ln] = (
                x_ref[pl.ds(b * sp + s, ncp, stride=t), :].astype(BF16))
        ublk_ref[pl.ds(r0 + ncp, pitch - ncp), :] = jnp.zeros((pitch - ncp, t * ln), BF16)
        return 0

    lax.fori_loop(0, bp, gather, 0)
    for s in range(t):
        ublk_ref[prow:prow + bs * ncs, s * ln:(s + 1) * ln] = (
            x_ref[pl.ds(mp + s, bs * ncs, stride=t), :].astype(BF16))
    if rows > prow + bs * ncs:
        ublk_ref[prow + bs * ncs:rows, :] = jnp.zeros((rows - prow - bs * ncs, t * ln), BF16)

    rc = rows // 2
    for r0 in (0, rc):
        inj = jnp.dot(ublk_ref[r0:r0 + rc, :], wst_ref[0], preferred_element_type=F32)
        for k in range(nslab):
            sst_ref[k, r0:r0 + rc, :] = inj[:, k * ln:(k + 1) * ln]

    ar = [apow_ref[0, 0:1, k * ln:(k + 1) * ln] for k in range(half)]
    ai = [apow_ref[0, 1:2, k * ln:(k + 1) * ln] for k in range(half)]

    def advance(xr, xi, loc):
        nr, ni = [], []
        for k in range(half):
            s_r = sst_ref[k, loc, :]
            s_i = sst_ref[half + k, loc, :]
            sst_ref[k, loc, :] = xr[k]
            sst_ref[half + k, loc, :] = xi[k]
            nr.append(ar[k] * xr[k] - ai[k] * xi[k] + s_r)
            ni.append(ar[k] * xi[k] + ai[k] * xr[k] + s_i)
        return tuple(nr), tuple(ni)

    for b0 in range(0, bp, V7X_SUBLANES):
        def pstep(c, carry, b0=b0):
            return advance(carry[0], carry[1], pl.ds(b0 * pitch + c, V7X_SUBLANES, stride=pitch))

        zero = tuple(jnp.zeros((V7X_SUBLANES, ln), F32) for _ in range(half))
        xr, xi = lax.fori_loop(0, ncp, pstep, (zero, zero), unroll=2)
        for k in range(half):
            xfp_ref[0, b0:b0 + V7X_SUBLANES, k * ln:(k + 1) * ln] = xr[k]
            xfp_ref[0, b0:b0 + V7X_SUBLANES, (half + k) * ln:(half + k + 1) * ln] = xi[k]
    for b0 in range(0, bs, V7X_SUBLANES):
        xr = tuple(h0_ref[0, b0:b0 + V7X_SUBLANES, k * ln:(k + 1) * ln] for k in range(half))
        xi = tuple(h0_ref[0, b0:b0 + V7X_SUBLANES, (half + k) * ln:(half + k + 1) * ln] for k in range(half))
        for c in range(ncs):
            xr, xi = advance(xr, xi, pl.ds(prow + b0 * ncs + c, V7X_SUBLANES, stride=ncs))
        for k in range(half):
            xfs_ref[0, b0:b0 + V7X_SUBLANES, k * ln:(k + 1) * ln] = xr[k]
            xfs_ref[0, b0:b0 + V7X_SUBLANES, (half + k) * ln:(half + k + 1) * ln] = xi[k]

    d_skip = d_ref[...]
    npair = bp // 2
    for pr in range(npair):
        last = pr == npair - 1
        r0 = pr * 2 * pitch
        nr = (rows - r0) if last else 2 * pitch
        xs = jnp.concatenate([sst_ref[k, r0:r0 + nr, :] for k in range(nslab)], axis=1).astype(BF16)
        for tp in range(t * ln // V7X_MXU):
            c0 = tp * V7X_MXU
            kk = c0 + V7X_MXU
            yc = (jnp.dot(ublk_ref[r0:r0 + nr, 0:kk], tbig_ref[0:kk, c0:c0 + V7X_MXU],
                          preferred_element_type=F32)
                  + lax.dot_general(xs, wro_ref[0, c0:c0 + V7X_MXU, :], (((1,), (1,)), ((), ())),
                                    preferred_element_type=F32))
            for sub in range(V7X_MXU // ln):
                tok = tp * (V7X_MXU // ln) + sub
                for q in range(2):
                    ybuf_ref[pl.ds(q * sp + tok, ncp, stride=t), :] = (
                        yc[q * pitch:q * pitch + ncp, sub * ln:(sub + 1) * ln])
                if last:
                    ybuf_ref[pl.ds(2 * sp + tok, bs * ncs, stride=t), :] = (
                        yc[2 * pitch:2 * pitch + bs * ncs, sub * ln:(sub + 1) * ln])
        tok0 = pr * 2 * sp
        ntok = 2 * sp + (bs * ss if last else 0)
        z_ref[tok0:tok0 + ntok, :] = _gelu_tanh(
            ybuf_ref[0:ntok, :] + d_skip * x_ref[tok0:tok0 + ntok, :]).astype(z_ref.dtype)


def s5_mixer_core(hn, ops, d_skip, h0_re, h0_im, *, bp, sp, bs, ss):
    bd, w_st, w_ro, a_pow = ops
    m, d = hn.shape
    ln = V7X_LANES
    nl = d // ln
    t = S5_BLOCK
    p8 = a_pow.shape[-1]
    p = p8 // S5_GPT
    assert bp % V7X_SUBLANES == 0 and bs % V7X_SUBLANES == 0 and bp % 2 == 0
    assert sp % t == 0 and ss % t == 0 and m == bp * sp + bs * ss and (2 * p8) % ln == 0
    ncp, ncs = sp // t, ss // t
    pitch = ncp + V7X_BF16_ROWS - ncp % V7X_BF16_ROWS
    rows = bp * pitch + bs * ncs
    rows += (-rows) % (2 * V7X_BF16_ROWS)
    def slabbed(a):
        return a.astype(F32).reshape(bs, nl, p8).transpose(1, 0, 2)
    h0 = jnp.concatenate([slabbed(h0_re), slabbed(h0_im)], axis=-1)
    nslab = 2 * p8 // ln
    z, xfp, xfs = pl.pallas_call(
        functools.partial(_s5_kernel, bp=bp, sp=sp, bs=bs, ss=ss, pitch=pitch),
        out_shape=(jax.ShapeDtypeStruct((m, d), BF16),
                   jax.ShapeDtypeStruct((nl, bp, 2 * p8), F32),
                   jax.ShapeDtypeStruct((nl, bs, 2 * p8), F32)),
        grid=(nl,),
        in_specs=[pl.BlockSpec((m, ln), lambda l: (0, l), pipeline_mode=pl.Buffered(1)),
                  pl.BlockSpec((1, t, ln, ln), lambda l: (l, 0, 0, 0)),
                  pl.BlockSpec((1, t * ln, 2 * p8), lambda l: (l, 0, 0), pipeline_mode=pl.Buffered(1)),
                  pl.BlockSpec((1, t * ln, 2 * p8), lambda l: (l, 0, 0), pipeline_mode=pl.Buffered(1)),
                  pl.BlockSpec((1, 2, p8), lambda l: (l, 0, 0)),
                  pl.BlockSpec((1, bs, 2 * p8), lambda l: (l, 0, 0)),
                  pl.BlockSpec((1, ln), lambda l: (0, l))],
        out_specs=(pl.BlockSpec((m, ln), lambda l: (0, l)),
                   pl.BlockSpec((1, bp, 2 * p8), lambda l: (l, 0, 0)),
                   pl.BlockSpec((1, bs, 2 * p8), lambda l: (l, 0, 0))),
        scratch_shapes=[pltpu.VMEM((t * ln, t * ln), BF16),
                        pltpu.VMEM((rows, t * ln), BF16),
                        pltpu.VMEM((nslab, rows, ln), F32),
                        pltpu.VMEM((2 * sp + bs * ss, ln), F32)],
        compiler_params=_cparams(1),
        name="s5_core",
    )(hn, bd, w_st, w_ro, a_pow, h0, d_skip.reshape(1, d).astype(F32))

    def unslab(a, nb):
        re = a[..., :p8].transpose(1, 0, 2).reshape(nb, nl * S5_GPT, p)
        im = a[..., p8:].transpose(1, 0, 2).reshape(nb, nl * S5_GPT, p)
        return re, im

    return (z,) + unslab(xfp, bp) + unslab(xfs, bs)


QBLK = 2 * CHUNK
KWIN = WINDOW + QBLK
BIAS_W = KWIN + WINDOW
BIAS_LEN = BIAS_W + QBLK
ATTN_GROUP = 4


def _attn_bias_vectors(rel_bias):
    n = np.arange(BIAS_LEN)
    delta = np.where(n < BIAS_LEN - (QBLK - 1), n, n - BIAS_LEN)
    idx = np.clip(WINDOW - delta, -REL_CLIP, REL_CLIP) + REL_CLIP
    return rel_bias.astype(F32)[:, idx][:, None, :]


def _bias_tile(e_row, nq):
    return pltpu.roll(jnp.broadcast_to(e_row, (nq, BIAS_LEN)), 0, 1, stride=1, stride_axis=0)


def _attn_prompt_kernel(q_ref, k_ref, v_ref, e_ref, o_ref, nk_ref, nv_ref, bias_ref, *, seq, keep):
    scale = HEAD_DIM ** -0.5
    qc = lax.broadcasted_iota(jnp.int32, (QBLK, BIAS_LEN), 0) // CHUNK
    kc = lax.broadcasted_iota(jnp.int32, (QBLK, BIAS_LEN), 1) // CHUNK
    tile = _bias_tile(e_ref[0], QBLK)
    bias_ref[...] = jnp.where(kc >= qc, jnp.where(kc <= qc + BAND_CHUNKS, tile, NEG_INF), NEG_INF)
    nblk = seq // QBLK
    for g0 in range(0, nblk, ATTN_GROUP):
        blocks = []
        for i in range(g0, min(g0 + ATTN_GROUP, nblk)):
            q0 = i * QBLK
            start = max(q0 - WINDOW, 0)
            blocks.append((q0, start, min(KWIN, seq - start), start - (q0 - WINDOW)))
        ss = []
        for q0, start, kw, shift in blocks:
            q = q_ref[q0:q0 + QBLK, :].astype(BF16)
            k = k_ref[start:start + kw, :].astype(BF16)
            s = lax.dot_general(q, k, (((1,), (1,)), ((), ())), preferred_element_type=F32)
            ss.append(s * scale + bias_ref[:, shift:shift + kw])
        ms = [jnp.max(s, axis=-1, keepdims=True) for s in ss]
        es = [jnp.exp(s - m) for s, m in zip(ss, ms)]
        ls = [jnp.sum(e, axis=-1, keepdims=True) for e in es]
        for (q0, start, kw, shift), e, l in zip(blocks, es, ls):
            v = v_ref[start:start + kw, :].astype(BF16)
            o = jnp.dot(e.astype(BF16), v, preferred_element_type=F32) * (1.0 / l)
            o_ref[q0:q0 + QBLK, :] = o.astype(o_ref.dtype)
    nk_ref[0] = k_ref[seq - keep:seq, :]
    nv_ref[0] = v_ref[seq - keep:seq, :]


def attn_prompt(qkv, bias_vec, *, bp, sp):
    m, d3 = qkv.shape
    d = d3 // 3
    nh = d // HEAD_DIM
    keep = min(WINDOW, sp)
    assert sp % QBLK == 0
    blk = lambda off: pl.BlockSpec((sp, HEAD_DIM), lambda b, h: (b, h + off))
    tail = pl.BlockSpec((1, keep, HEAD_DIM), lambda b, h: (b, 0, h))
    return pl.pallas_call(
        functools.partial(_attn_prompt_kernel, seq=sp, keep=keep),
        out_shape=(jax.ShapeDtypeStruct((m, d), BF16),
                   jax.ShapeDtypeStruct((bp, keep, d), F32),
                   jax.ShapeDtypeStruct((bp, keep, d), F32)),
        grid=(bp, nh),
        in_specs=[blk(0), blk(nh), blk(2 * nh),
                  pl.BlockSpec((1, 1, BIAS_LEN), lambda b, h: (h, 0, 0))],
        out_specs=(pl.BlockSpec((sp, HEAD_DIM), lambda b, h: (b, h)), tail, tail),
        scratch_shapes=[pltpu.VMEM((QBLK, BIAS_LEN), F32)],
        compiler_params=_cparams(2),
        name="attn_prompt",
    )(qkv, qkv, qkv, bias_vec)


def _attn_sample_kernel(qkv_ref, ck_ref, cv_ref, e_ref, prev_ref, o_ref, *, seq, ncache, nh, mask):
    del prev_ref
    scale = HEAD_DIM ** -0.5
    d = nh * HEAD_DIM
    dn = (((1,), (1,)), ((), ()))
    off = WINDOW - ncache
    for h in range(nh):
        c0 = h * HEAD_DIM
        q = qkv_ref[:, c0:c0 + HEAD_DIM].astype(BF16)
        k = qkv_ref[:, d + c0:d + c0 + HEAD_DIM].astype(BF16)
        v = qkv_ref[:, 2 * d + c0:2 * d + c0 + HEAD_DIM].astype(BF16)
        ck = ck_ref[0, pl.ds(h, ncache, stride=nh), :].astype(BF16)
        cv = cv_ref[0, pl.ds(h, ncache, stride=nh), :].astype(BF16)
        bias = _bias_tile(e_ref[h], seq)
        s1 = lax.dot_general(q, ck, dn, preferred_element_type=F32) * scale + bias[:, off:off + ncache]
        s2 = lax.dot_general(q, k, dn, preferred_element_type=F32) * scale + bias[:, WINDOW:WINDOW + seq]
        if mask is not None:
            s1 = jnp.where(jnp.asarray(mask[:, :ncache]), s1, NEG_INF)
            s2 = jnp.where(jnp.asarray(mask[:, ncache:]), s2, NEG_INF)
        mx = jnp.maximum(jnp.max(s1, axis=-1, keepdims=True), jnp.max(s2, axis=-1, keepdims=True))
        e1 = jnp.exp(s1 - mx)
        e2 = jnp.exp(s2 - mx)
        l = jnp.sum(e1, axis=-1, keepdims=True) + jnp.sum(e2, axis=-1, keepdims=True)
        o = (jnp.dot(e1.astype(BF16), cv, preferred_element_type=F32)
             + jnp.dot(e2.astype(BF16), v, preferred_element_type=F32)) * (1.0 / l)
        o_ref[:, c0:c0 + HEAD_DIM] = o.astype(o_ref.dtype)


def attn_sample(qkv, cache_k, cache_v, bias_vec, o_prev, *, row0, bs, ss):
    m, d3 = qkv.shape
    d = d3 // 3
    nh = d // HEAD_DIM
    ncache = cache_k.shape[1]
    assert ss <= QBLK and ss % V7X_SUBLANES == 0 and ncache <= WINDOW and row0 % ss == 0
    q_pos = PAST_LEN + np.arange(ss)
    k_pos = PAST_LEN - ncache + np.arange(ncache + ss)
    qc, kc = q_pos // CHUNK, k_pos // CHUNK
    allowed = ((k_pos[None, :] >= 0) & (kc[None, :] <= qc[:, None])
               & (kc[None, :] >= qc[:, None] - BAND_CHUNKS))
    mask = None if allowed.all() else allowed
    ck = cache_k.reshape(bs, ncache * nh, HEAD_DIM)
    cv = cache_v.reshape(bs, ncache * nh, HEAD_DIM)
    blk0 = row0 // ss
    cblk = pl.BlockSpec((1, ncache * nh, HEAD_DIM), lambda b: (b, 0, 0))
    return pl.pallas_call(
        functools.partial(_attn_sample_kernel, seq=ss, ncache=ncache, nh=nh, mask=mask),
        out_shape=jax.ShapeDtypeStruct((m, d), BF16),
        grid=(bs,),
        in_specs=[pl.BlockSpec((ss, d3), lambda b: (b + blk0, 0)), cblk, cblk,
                  pl.BlockSpec((nh, 1, BIAS_LEN), lambda b: (0, 0, 0)),
                  pl.BlockSpec(memory_space=pl.ANY)],
        out_specs=pl.BlockSpec((ss, d), lambda b: (b + blk0, 0)),
        input_output_aliases={4: 0},
        compiler_params=_cparams(1),
        name="attn_sample",
    )(qkv, ck, cv, bias_vec, o_prev)


HGRN_HEADS_PER_STEP = 2


def _cumsum_rows(g):
    c = g.shape[0]
    row = lax.broadcasted_iota(jnp.int32, g.shape, 0)
    x = g
    for sh in (1, 2, 4):
        x = x + jnp.where((row & (V7X_SUBLANES - 1)) >= sh, pltpu.roll(x, sh, 0), 0.0)
    pieces = []
    carry = None
    for j in range(c // V7X_SUBLANES):
        blk = x[j * V7X_SUBLANES:(j + 1) * V7X_SUBLANES]
        if carry is not None:
            blk = blk + carry
        pieces.append(blk)
        carry = blk[V7X_SUBLANES - 1:V7X_SUBLANES]
    return jnp.concatenate(pieces, axis=0)


def _hgrn_kernel(*refs, seq, csz, has_state, aliased, hp):
    refs = list(refs)
    q_ref, f_ref, v_ref, z_ref, llb_ref, l1m_ref, gain_ref = refs[:7]
    pos = 7
    s0_ref = None
    if has_state:
        s0_ref = refs[pos]
        pos += 1
    if aliased:
        pos += 1
    o_ref, sn_ref, st_ref = refs[pos:pos + 3]
    kdim = HEAD_DIM
    scale = kdim ** -0.5
    dn_t = (((1,), (1,)), ((), ()))
    for hh in range(hp):
        if has_state:
            st_ref[hh] = s0_ref[0, hh].T
        else:
            st_ref[hh] = jnp.zeros((kdim, kdim), F32)
    row = lax.broadcasted_iota(jnp.int32, (csz, kdim), 0)
    tx = (lax.broadcasted_iota(jnp.int32, (csz, csz), 0)
          ^ lax.broadcasted_iota(jnp.int32, (csz, csz), 1))

    def one_head(r0, hh):
        sl = slice(hh * kdim, (hh + 1) * kdim)
        llb = llb_ref[:, sl]
        l1m = l1m_ref[:, sl]
        qz = q_ref[pl.ds(r0, csz), sl]
        fz = f_ref[pl.ds(r0, csz), sl]
        v = v_ref[pl.ds(r0, csz), sl]
        gz = z_ref[pl.ds(r0, csz), sl]
        ls = jnp.minimum(fz, 0.0) - jnp.log(1.0 + jnp.exp(-jnp.abs(fz)))
        cc = l1m + ls
        g = jnp.maximum(llb, cc) + jnp.log(1.0 + jnp.exp(-jnp.abs(llb - cc)))
        kk = jnp.exp(cc - fz)
        q = qz * _sigmoid(qz) * scale
        b = _cumsum_rows(g)
        v_bf = v.astype(BF16)
        sc = lax.dot_general(q.astype(BF16), kk.astype(BF16), dn_t, preferred_element_type=F32)
        scores = jnp.where(tx == 0, sc, 0.0)
        e = b
        w = 1
        while w < min(V7X_SUBLANES, csz):
            bit = (row & w) != 0
            e_prev = pltpu.roll(e, w, 0)
            ex = jnp.exp(jnp.where(bit, b - e_prev, e - b))
            qw = jnp.where(bit, q * ex, 0.0).astype(BF16)
            kw = jnp.where(bit, 0.0, kk * ex).astype(BF16)
            sc = lax.dot_general(qw, kw, dn_t, preferred_element_type=F32)
            scores = scores + jnp.where(tx < 2 * w, sc, 0.0)
            if 2 * w < V7X_SUBLANES:
                e = jnp.where(bit, e, pltpu.roll(e, csz - w, 0))
            w *= 2
        while w < csz:
            n2 = csz // (2 * w)
            b4 = b.reshape(n2, 2, w, kdim)
            bound = b4[:, 0, w - 1:w, :]
            kpart = kk.reshape(n2, 2, w, kdim)[:, 0] * jnp.exp(bound - b4[:, 0])
            qpart = q.reshape(n2, 2, w, kdim)[:, 1] * jnp.exp(b4[:, 1] - bound)
            zero = jnp.zeros((n2, 1, w, kdim), F32)
            kw = jnp.concatenate([kpart[:, None], zero], axis=1).reshape(csz, kdim).astype(BF16)
            qw = jnp.concatenate([zero, qpart[:, None]], axis=1).reshape(csz, kdim).astype(BF16)
            sc = lax.dot_general(qw, kw, dn_t, preferred_element_type=F32)
            if 2 * w < csz:
                scores = scores + jnp.where(tx < 2 * w, sc, 0.0)
            else:
                scores = scores + sc
            w *= 2
        o = jnp.dot(scores.astype(BF16), v_bf, preferred_element_type=F32)
        st = st_ref[hh]
        o = o + lax.dot_general((q * jnp.exp(b)).astype(BF16), st.astype(BF16), dn_t,
                                preferred_element_type=F32)
        b_last = b[csz - 1:csz]
        kd = (kk * jnp.exp(b_last - b)).astype(BF16)
        st_ref[hh] = jnp.exp(b_last) * st + lax.dot_general(
            v_bf, kd, (((0,), (0,)), ((), ())), preferred_element_type=F32)
        o = o * lax.rsqrt(jnp.mean(o * o, axis=-1, keepdims=True) + RMS_EPS)
        o = o * gain_ref[:, sl] * (gz * _sigmoid(gz))
        o_ref[pl.ds(r0, csz), sl] = o.astype(o_ref.dtype)

    def chunk(ci, _):
        r0 = pl.multiple_of(ci * csz, csz)
        for hh in range(hp):
            one_head(r0, hh)
        return 0

    nchunks = seq // csz
    if nchunks == 1:
        chunk(0, 0)
    else:
        lax.fori_loop(0, nchunks, chunk, 0)
    for hh in range(hp):
        sn_ref[0, hh] = st_ref[hh].T


def hgrn_core(proj, log_lb, log_1m_lb, norm_gain, state, o_prev, *, row0, nb, seq):
    m, d4 = proj.shape
    d = d4 // 4
    nh = d // HEAD_DIM
    hp = HGRN_HEADS_PER_STEP if nh % HGRN_HEADS_PER_STEP == 0 else 1
    ng = nh // hp
    w = hp * HEAD_DIM
    csz = _pick(seq, (128, 64, 32, 16, 8))
    assert row0 % seq == 0
    blk0 = row0 // seq
    has_state = state is not None
    aliased = o_prev is not None
    blk = lambda off: pl.BlockSpec((seq, w), lambda b, h: (b + blk0, h + off * ng))
    vec = pl.BlockSpec((1, w), lambda b, h: (0, h))
    sblk = pl.BlockSpec((1, hp, HEAD_DIM, HEAD_DIM), lambda b, h: (b, h, 0, 0))
    in_specs = [blk(0), blk(1), blk(2), blk(3), vec, vec, vec]
    args = [proj, proj, proj, proj, log_lb.reshape(1, d), log_1m_lb.reshape(1, d),
            norm_gain.reshape(1, d).astype(F32)]
    if has_state:
        in_specs.append(sblk)
        args.append(state.astype(F32))
    aliases = {}
    if aliased:
        aliases = {len(args): 0}
        in_specs.append(pl.BlockSpec(memory_space=pl.ANY))
        args.append(o_prev)
    return pl.pallas_call(
        functools.partial(_hgrn_kernel, seq=seq, csz=csz, has_state=has_state, aliased=aliased, hp=hp),
        out_shape=(jax.ShapeDtypeStruct((m, d), BF16),
                   jax.ShapeDtypeStruct((nb, nh, HEAD_DIM, HEAD_DIM), F32)),
        grid=(nb, ng),
        in_specs=in_specs,
        out_specs=(pl.BlockSpec((seq, w), lambda b, h: (b + blk0, h)), sblk),
        scratch_shapes=[pltpu.VMEM((hp, HEAD_DIM, HEAD_DIM), F32)],
        input_output_aliases=aliases,
        compiler_params=_cparams(2),
        name="hgrn_core",
    )(*args)


def kernel(x_prompt, x_sample, state_s5_re, state_s5_im, cache_attn_k, cache_attn_v, state_hgrn,
           norm_mixer, norm_ffn, norm_final, ffn_w_gate_up, ffn_w_down,
           s5_lambda_re, s5_lambda_im, s5_log_step, s5_b_re, s5_b_im, s5_c_re, s5_c_im, s5_d, s5_w_glu,
           attn_w_qkv, attn_rel_bias, attn_w_o,
           hgrn_w_in, hgrn_lower_bounds, hgrn_norm, hgrn_w_o):
    bp, sp, d = x_prompt.shape
    bs, ss, _ = x_sample.shape
    depth = norm_mixer.shape[0]
    mp = bp * sp
    ms = bs * ss
    m = mp + ms
    d_ff = ffn_w_down.shape[1]
    nh = d // HEAD_DIM

    x = jnp.concatenate([x_prompt.reshape(mp, d), x_sample.reshape(ms, d)], axis=0)
    bm = _pick(m, (1536, 768, 512, 256, 128, 64, 32, 16, 8))
    bm_glu = _pick(m, (768, 512, 256, 128, 64, 32, 16, 8))
    bm_down = _pick(m, (512, 256, 128, 64, 32, 16, 8))
    bn = _pick(d, (256, 128))
    bn_proj = _pick(d, (512, 256, 128))
    bn_down = _pick(d, (512, 256, 128))
    bn_ff = _pick(d_ff, (256, 128))

    w_gate_up = ffn_w_gate_up
    w_down = ffn_w_down.astype(BF16)
    w_glu = s5_w_glu
    w_qkv = attn_w_qkv
    w_ao = attn_w_o
    w_hin = hgrn_w_in
    w_ho = hgrn_w_o

    lbs = jnp.cumsum(jax.nn.softmax(hgrn_lower_bounds.astype(F32), axis=0), axis=0)
    lbs = lbs - lbs[0]

    new_re_p, new_im_p, new_re_s, new_im_s = [], [], [], []
    new_k_p, new_v_p, new_k_s, new_v_s = [], [], [], []
    new_h_p, new_h_s = [], []

    for layer in range(depth):
        kind = layer % N_MIXERS
        j = layer // N_MIXERS
        if kind == 0:
            hn = rmsnorm(x, norm_mixer[layer], F32)
            ops = _s5_operators(s5_lambda_re[j], s5_lambda_im[j], s5_log_step[j],
                                s5_b_re[j], s5_b_im[j], s5_c_re[j], s5_c_im[j])
            z, re_p, im_p, re_s, im_s = s5_mixer_core(
                hn, ops, s5_d[j], state_s5_re[j], state_s5_im[j], bp=bp, sp=sp, bs=bs, ss=ss)
            new_re_p.append(re_p)
            new_im_p.append(im_p)
            new_re_s.append(re_s)
            new_im_s.append(im_s)
            x, xg, ssq = matmul(z, w_glu, j, n_out=d, w_col_offsets=(0, d), epi="glu_res",
                                out_dtype=F32, res=x, bm=bm_glu, bn=bn, next_gain=norm_ffn[layer], name="s5_glu")
        elif kind == 1:
            qkv = matmul(xg, w_qkv, j, n_out=3 * d, w_col_offsets=(0,), epi="none",
                         out_dtype=F32, bm=bm, bn=bn_proj, ssq=ssq, name="attn_qkv")
            bias_vec = _attn_bias_vectors(attn_rel_bias[j])
            o, nk_p, nv_p = attn_prompt(qkv, bias_vec, bp=bp, sp=sp)
            o = attn_sample(qkv, cache_attn_k[j], cache_attn_v[j], bias_vec, o, row0=mp, bs=bs, ss=ss)
            qkv_s = qkv[mp:].reshape(bs, ss, 3 * d)
            new_k_p.append(nk_p.reshape(bp, -1, nh, HEAD_DIM))
            new_v_p.append(nv_p.reshape(bp, -1, nh, HEAD_DIM))
            k_all = jnp.concatenate([cache_attn_k[j], qkv_s[:, :, d:2 * d].reshape(bs, ss, nh, HEAD_DIM)], axis=1)
            v_all = jnp.concatenate([cache_attn_v[j], qkv_s[:, :, 2 * d:].reshape(bs, ss, nh, HEAD_DIM)], axis=1)
            new_k_s.append(k_all[:, ss:])
            new_v_s.append(v_all[:, ss:])
            x, xg, ssq = matmul(o, w_ao, j, n_out=d, w_col_offsets=(0,), epi="res",
                                out_dtype=F32, res=x, bm=bm, bn=bn, next_gain=norm_ffn[layer], name="attn_wo")
        else:
            proj = matmul(xg, w_hin, j, n_out=4 * d, w_col_offsets=(0,), epi="none",
                          out_dtype=F32, bm=bm, bn=bn_proj, ssq=ssq, name="hgrn_win")
            lb = lbs[layer]
            log_lb = jnp.log(lb)
            log_1m_lb = jnp.log1p(-lb)
            o, h_p = hgrn_core(proj, log_lb, log_1m_lb, hgrn_norm[j], None, None, row0=0, nb=bp, seq=sp)
            o, h_s = hgrn_core(proj, log_lb, log_1m_lb, hgrn_norm[j], state_hgrn[j], o, row0=mp, nb=bs, seq=ss)
            new_h_p.append(h_p)
            new_h_s.append(h_s)
            x, xg, ssq = matmul(o, w_ho, j, n_out=d, w_col_offsets=(0,), epi="res",
                                out_dtype=F32, res=x, bm=bm, bn=bn, next_gain=norm_ffn[layer], name="hgrn_wo")
        act = matmul(xg, w_gate_up, layer, n_out=d_ff, w_col_offsets=(0, d_ff),
                     epi="swiglu", out_dtype=BF16, bm=bm, bn=bn_ff, ssq=ssq, name="ffn_gate_up")
        fuse_next = layer + 1 < depth and (layer + 1) % N_MIXERS != 0
        if fuse_next:
            x, xg, ssq = matmul(act, w_down, layer, n_out=d, w_col_offsets=(0,), epi="res",
                                out_dtype=F32, res=x, bm=bm_down, bn=bn_down,
                                next_gain=norm_mixer[layer + 1], name="ffn_down")
        else:
            x = matmul(act, w_down, layer, n_out=d, w_col_offsets=(0,), epi="res",
                       out_dtype=F32, res=x, bm=bm_down, bn=bn_down, w_resident=True, name="ffn_down")

    y_p = rmsnorm(x, norm_final, F32, 0, mp).reshape(bp, sp, d)
    y_s = rmsnorm(x, norm_final, F32, mp, ms).reshape(bs, ss, d)
    return (y_p, y_s,
            jnp.stack(new_re_p), jnp.stack(new_im_p),
            jnp.stack(new_k_p), jnp.stack(new_v_p), jnp.stack(new_h_p),
            jnp.stack(new_re_s), jnp.stack(new_im_s),
            jnp.stack(new_k_s), jnp.stack(new_v_s), jnp.stack(new_h_s))
```

```python
import functools
import math

import numpy as np
import jax
import jax.numpy as jnp
from jax import lax
from jax.experimental import pallas as pl
from jax.experimental.pallas import tpu as pltpu

F32 = jnp.float32
BF16 = jnp.bfloat16

PAST_LEN = 2048
CHUNK = 64
BAND_CHUNKS = 8
WINDOW = BAND_CHUNKS * CHUNK
REL_CLIP = 128
N_MIXERS = 3
S5_GROUP = 16
S5_BLOCK = 16
HEAD_DIM = 128
RMS_EPS = 1e-6
NEG_INF = -1e30

V7X_LANES = 128
V7X_SUBLANES = 8
V7X_BF16_ROWS = 16
V7X_MXU = 256
V7X_VMEM_LIMIT = 56 * 1024 * 1024


def _pick(n, candidates):
    for c in candidates:
        if n % c == 0:
            return c
    return n


def _cparams(n_axes):
    return pltpu.CompilerParams(
        dimension_semantics=("parallel",) * n_axes,
        vmem_limit_bytes=V7X_VMEM_LIMIT)


def _sigmoid(x):
    return 1.0 / (1.0 + jnp.exp(-x))


def _gelu_tanh(x):
    c = math.sqrt(2.0 / math.pi)
    return 0.5 * x * (1.0 + jnp.tanh(c * (x + 0.044715 * (x * x * x))))


def _rmsnorm_kernel(x_ref, g_ref, o_ref):
    x = x_ref[...]
    ms = jnp.mean(x * x, axis=-1, keepdims=True)
    o_ref[...] = (x * lax.rsqrt(ms + RMS_EPS) * g_ref[...]).astype(o_ref.dtype)


def rmsnorm(x, gain, out_dtype, row0=0, rows=None):
    m, d = x.shape
    rows = m if rows is None else rows
    bm = _pick(math.gcd(rows, row0) if row0 else rows, (256, 128, 64, 32, 16, 8))
    blk0 = row0 // bm
    return pl.pallas_call(
        _rmsnorm_kernel,
        out_shape=jax.ShapeDtypeStruct((rows, d), out_dtype),
        grid=(rows // bm,),
        in_specs=[pl.BlockSpec((bm, d), lambda i: (i + blk0, 0)),
                  pl.BlockSpec((1, d), lambda i: (0, 0))],
        out_specs=pl.BlockSpec((bm, d), lambda i: (i, 0)),
        compiler_params=_cparams(1),
        name="rmsnorm",
    )(x, gain.reshape(1, d).astype(F32))


MM_SUB_ROWS = 512
def _mm_kernel(*refs, n_w, epi, has_res, pre_norm, post_norm, k_dim, sub):
    refs = list(refs)
    a_ref = refs.pop(0)
    w_refs = [refs.pop(0) for _ in range(n_w)]
    res_ref = refs.pop(0) if has_res else None
    ssq_in_ref = refs.pop(0) if pre_norm else None
    gain_ref = refs.pop(0) if post_norm else None
    o_ref = refs.pop(0)
    bm, bn = o_ref.shape
    ws = [w if w.dtype == BF16 else w[...].astype(BF16) for w in w_refs]
    if post_norm:
        xg_ref, ssq_ref = refs
    for r0 in range(0, bm, sub):
        rows = slice(r0, r0 + sub)
        a = a_ref[rows, :]
        vals = [jnp.dot(a, w[...], preferred_element_type=F32) for w in ws]
        if pre_norm:
            rstd = lax.rsqrt(ssq_in_ref[rows, :] * (1.0 / k_dim) + RMS_EPS)
            rstd = jnp.concatenate([rstd] * (bn // V7X_LANES), axis=1)
            vals = [v * rstd for v in vals]
        if epi == "none":
            o = vals[0]
        elif epi == "swiglu":
            o = vals[0] * _sigmoid(vals[0]) * vals[1]
        elif epi == "res":
            o = res_ref[rows, :] + vals[0]
        elif epi == "glu_res":
            o = res_ref[rows, :] + vals[0] * _sigmoid(vals[1])
        else:
            raise ValueError(epi)
        o_ref[rows, :] = o.astype(o_ref.dtype)
        if post_norm:
            xg_ref[rows, :] = (o * gain_ref[...]).astype(xg_ref.dtype)
            part = jnp.broadcast_to(jnp.sum(o * o, axis=1, keepdims=True), (sub, V7X_LANES))

            @pl.when(pl.program_id(1) == 0)
            def _():
                ssq_ref[rows, :] = part

            @pl.when(pl.program_id(1) != 0)
            def _():
                ssq_ref[rows, :] += part


def matmul(a, w, layer, *, n_out, w_col_offsets, epi, out_dtype, res=None, bm, bn, w_resident=False,
           ssq=None, next_gain=None, name):
    m, k = a.shape
    n_w = len(w_col_offsets)
    post_norm = next_gain is not None
    assert m % bm == 0 and n_out % bn == 0 and bn % V7X_LANES == 0
    assert all(off % bn == 0 for off in w_col_offsets)
    assert not (w_resident and post_norm)
    if w_resident:
        grid = (n_out // bn, m // bm)
        ij = lambda g0, g1: (g1, g0)
    else:
        grid = (m // bm, n_out // bn)
        ij = lambda g0, g1: (g0, g1)

    def a_map(g0, g1):
        return (ij(g0, g1)[0], 0)

    def w_map(off_blocks):
        return lambda g0, g1: (layer, 0, ij(g0, g1)[1] + off_blocks)

    def o_map(g0, g1):
        return ij(g0, g1)

    in_specs = [pl.BlockSpec((bm, k), a_map)]
    args = [a]
    for off in w_col_offsets:
        in_specs.append(pl.BlockSpec((None, k, bn), w_map(off // bn)))
        args.append(w)
    if res is not None:
        in_specs.append(pl.BlockSpec((bm, bn), o_map))
        args.append(res)
    if ssq is not None:
        in_specs.append(pl.BlockSpec((bm, V7X_LANES), a_map))
        args.append(ssq)
    out_shape = [jax.ShapeDtypeStruct((m, n_out), out_dtype)]
    out_specs = [pl.BlockSpec((bm, bn), o_map)]
    if post_norm:
        in_specs.append(pl.BlockSpec((1, bn), lambda g0, g1: (0, ij(g0, g1)[1])))
        args.append(next_gain.reshape(1, n_out).astype(F32))
        out_shape += [jax.ShapeDtypeStruct((m, n_out), BF16), jax.ShapeDtypeStruct((m, V7X_LANES), F32)]
        out_specs += [pl.BlockSpec((bm, bn), o_map), pl.BlockSpec((bm, V7X_LANES), a_map)]
    sem = ("parallel", "arbitrary") if post_norm else ("parallel", "parallel")
    sub = bm
    if epi == "swiglu" and bm % MM_SUB_ROWS == 0:
        sub = MM_SUB_ROWS
    out = pl.pallas_call(
        functools.partial(_mm_kernel, n_w=n_w, epi=epi, has_res=res is not None,
                          pre_norm=ssq is not None, post_norm=post_norm, k_dim=k, sub=sub),
        out_shape=tuple(out_shape),
        grid=grid,
        in_specs=in_specs,
        out_specs=tuple(out_specs),
        compiler_params=pltpu.CompilerParams(dimension_semantics=sem, vmem_limit_bytes=V7X_VMEM_LIMIT),
        name=name,
    )(*args)
    return out if post_norm else out[0]


S5_GPT = V7X_LANES // S5_GROUP


def _s5_operators(lam_re, lam_im, log_step, b_re, b_im, c_re, c_im):
    g, p = lam_re.shape
    j = b_re.shape[-1]
    t = S5_BLOCK
    nl = g // S5_GPT
    hp = lax.Precision.HIGHEST
    step = jnp.exp(log_step.astype(F32))[:, None]
    lam_re = lam_re.astype(F32)
    lam_im = lam_im.astype(F32)
    lr = lam_re * step
    li = lam_im * step
    tau = jnp.arange(t + 1, dtype=F32)[:, None, None]
    mag = jnp.exp(tau * lr[None])
    e_re = mag * jnp.cos(tau * li[None])
    e_im = mag * jnp.sin(tau * li[None])
    lbar_re, lbar_im = e_re[1], e_im[1]
    denom = lam_re * lam_re + lam_im * lam_im
    num_re = lbar_re - 1.0
    coef_re = (num_re * lam_re + lbar_im * lam_im) / denom
    coef_im = (lbar_im * lam_re - num_re * lam_im) / denom
    bt_re = jnp.swapaxes(b_re.astype(F32), 1, 2)
    bt_im = jnp.swapaxes(b_im.astype(F32), 1, 2)
    bb_re = coef_re[:, None, :] * bt_re - coef_im[:, None, :] * bt_im
    bb_im = coef_re[:, None, :] * bt_im + coef_im[:, None, :] * bt_re
    c_re = c_re.astype(F32)
    c_im = c_im.astype(F32)
    eye = jnp.eye(S5_GPT, dtype=F32)

    a_re = e_re[:, :, None, :] * bb_re[None] - e_im[:, :, None, :] * bb_im[None]
    a_im = e_re[:, :, None, :] * bb_im[None] + e_im[:, :, None, :] * bb_re[None]

    al_re = a_re[:t].reshape(t, nl, S5_GPT * j, p)
    al_im = a_im[:t].reshape(t, nl, S5_GPT * j, p)
    cl_re = c_re.reshape(nl, S5_GPT * j, p)
    cl_im = c_im.reshape(nl, S5_GPT * j, p)
    lagfull = (jnp.einsum("tlip,lop->ltio", al_re, cl_re, precision=hp)
               - jnp.einsum("tlip,lop->ltio", al_im, cl_im, precision=hp))
    blockmask = jnp.kron(eye, jnp.ones((j, j), F32))
    bd = (lagfull * blockmask).astype(BF16)

    parity = (jnp.arange(g) % 2)[None, :, None, None]

    def position(src_re, src_im):
        parts = [jnp.where(parity == h, src, 0.0) for src in (src_re, src_im) for h in (0, 1)]
        out = jnp.concatenate(parts, axis=-1).astype(BF16)
        out = out.reshape(t, nl, S5_GPT, j, 4 * p).transpose(1, 0, 2, 3, 4)
        return out.reshape(nl, t * S5_GPT * j, 4 * p)

    w_st = position(a_re[:t][::-1], a_im[:t][::-1])
    ce_re = c_re[None] * e_re[1:, :, None, :] - c_im[None] * e_im[1:, :, None, :]
    ce_im = c_re[None] * e_im[1:, :, None, :] + c_im[None] * e_re[1:, :, None, :]
    w_ro = position(ce_re, -ce_im)

    a_pow = jnp.stack([e_re[t].reshape(nl, S5_GPT * p), e_im[t].reshape(nl, S5_GPT * p)], axis=1)
    return bd, w_st, w_ro, a_pow


def _s5_kernel(x_ref, bd_ref, wstc_ref, wroc_ref, apow_ref, h0_ref, d_ref,
               z_ref, xfp_ref, xfs_ref,
               tbig_ref, wst_ref, wro_ref, ublk_ref, sst_ref, ybuf_ref, *,
               bp, sp, bs, ss, pitch):
    t = S5_BLOCK
    ln = V7X_LANES
    ncp = sp // t
    ncs = ss // t
    mp = bp * sp
    prow = bp * pitch
    rows = ublk_ref.shape[0]
    nslab = sst_ref.shape[0]
    half = nslab // 2

    @pl.when(pl.program_id(0) == 0)
    def _():
        wst_ref[...] = jnp.zeros_like(wst_ref)
        wro_ref[...] = jnp.zeros_like(wro_ref)

    pair_rows = 2 * S5_GROUP
    for src_ref, dst_ref in ((wstc_ref, wst_ref), (wroc_ref, wro_ref)):
        for s in range(t):
            for k in range(half):
                r0 = s * ln + k * pair_rows
                dst_ref[r0:r0 + pair_rows, k * ln:(k + 1) * ln] = src_ref[0, r0:r0 + pair_rows, 0:ln]
                dst_ref[r0:r0 + pair_rows, (half + k) * ln:(half + k + 1) * ln] = (
                    src_ref[0, r0:r0 + pair_rows, ln:2 * ln])

    for s in range(t):
        for tt in range(s, t):
            tbig_ref[s * ln:(s + 1) * ln, tt * ln:(tt + 1) * ln] = bd_ref[0, tt - s]
    for tt in range(0, t, 2):
        tbig_ref[(tt + 1) * ln:(tt + 2) * ln, tt * ln:(tt + 1) * ln] = jnp.zeros((ln, ln), BF16)

    def gather(b, _):
        r0 = pl.multiple_of(b * pitch, V7X_BF16_ROWS)
        for s in range(t):
            ublk_ref[pl.ds(r0, ncp), s * ln:(s + 1) * ln] = (
                x_ref[pl.ds(b * sp + s, ncp, stride=t), :].astype(BF16))
        ublk_ref[pl.ds(r0 + ncp, pitch - ncp), :] = jnp.zeros((pitch - ncp, t * ln), BF16)
        return 0

    lax.fori_loop(0, bp, gather, 0)
    for s in range(t):
        ublk_ref[prow:prow + bs * ncs, s * ln:(s + 1) * ln] = (
            x_ref[pl.ds(mp + s, bs * ncs, stride=t), :].astype(BF16))
    if rows > prow + bs * ncs:
        ublk_ref[prow + bs * ncs:rows, :] = jnp.zeros((rows - prow - bs * ncs, t * ln), BF16)

    rc = rows // 2
    for r0 in (0, rc):
        inj = jnp.dot(ublk_ref[r0:r0 + rc, :], wst_ref[...], preferred_element_type=F32)
        for k in range(nslab):
            sst_ref[k, r0:r0 + rc, :] = inj[:, k * ln:(k + 1) * ln]

    ar = [apow_ref[0, 0:1, k * ln:(k + 1) * ln] for k in range(half)]
    ai = [apow_ref[0, 1:2, k * ln:(k + 1) * ln] for k in range(half)]

    def advance(xr, xi, loc):
        nr, ni = [], []
        for k in range(half):
            s_r = sst_ref[k, loc, :]
            s_i = sst_ref[half + k, loc, :]
            sst_ref[k, loc, :] = xr[k]
            sst_ref[half + k, loc, :] = xi[k]
            nr.append(ar[k] * xr[k] - ai[k] * xi[k] + s_r)
            ni.append(ar[k] * xi[k] + ai[k] * xr[k] + s_i)
        return tuple(nr), tuple(ni)

    for b0 in range(0, bp, V7X_SUBLANES):
        def pstep(c, carry, b0=b0):
            return advance(carry[0], carry[1], pl.ds(b0 * pitch + c, V7X_SUBLANES, stride=pitch))

        zero = tuple(jnp.zeros((V7X_SUBLANES, ln), F32) for _ in range(half))
        xr, xi = lax.fori_loop(0, ncp, pstep, (zero, zero), unroll=2)
        for k in range(half):
            xfp_ref[0, b0:b0 + V7X_SUBLANES, k * ln:(k + 1) * ln] = xr[k]
            xfp_ref[0, b0:b0 + V7X_SUBLANES, (half + k) * ln:(half + k + 1) * ln] = xi[k]
    for b0 in range(0, bs, V7X_SUBLANES):
        xr = tuple(h0_ref[0, b0:b0 + V7X_SUBLANES, k * ln:(k + 1) * ln] for k in range(half))
        xi = tuple(h0_ref[0, b0:b0 + V7X_SUBLANES, (half + k) * ln:(half + k + 1) * ln] for k in range(half))
        for c in range(ncs):
            xr, xi = advance(xr, xi, pl.ds(prow + b0 * ncs + c, V7X_SUBLANES, stride=ncs))
        for k in range(half):
            xfs_ref[0, b0:b0 + V7X_SUBLANES, k * ln:(k + 1) * ln] = xr[k]
            xfs_ref[0, b0:b0 + V7X_SUBLANES, (half + k) * ln:(half + k + 1) * ln] = xi[k]

    d_skip = d_ref[...]
    npair = bp // 2
    for pr in range(npair):
        last = pr == npair - 1
        r0 = pr * 2 * pitch
        nr = (rows - r0) if last else 2 * pitch
        xs = jnp.concatenate([sst_ref[k, r0:r0 + nr, :] for k in range(nslab)], axis=1).astype(BF16)
        for tp in range(t * ln // V7X_MXU):
            c0 = tp * V7X_MXU
            kk = c0 + V7X_MXU
            yc = (jnp.dot(ublk_ref[r0:r0 + nr, 0:kk], tbig_ref[0:kk, c0:c0 + V7X_MXU],
                          preferred_element_type=F32)
                  + lax.dot_general(xs, wro_ref[c0:c0 + V7X_MXU, :], (((1,), (1,)), ((), ())),
                                    preferred_element_type=F32))
            for sub in range(V7X_MXU // ln):
                tok = tp * (V7X_MXU // ln) + sub
                for q in range(2):
                    ybuf_ref[pl.ds(q * sp + tok, ncp, stride=t), :] = (
                        yc[q * pitch:q * pitch + ncp, sub * ln:(sub + 1) * ln])
                if last:
                    ybuf_ref[pl.ds(2 * sp + tok, bs * ncs, stride=t), :] = (
                        yc[2 * pitch:2 * pitch + bs * ncs, sub * ln:(sub + 1) * ln])
        tok0 = pr * 2 * sp
        ntok = 2 * sp + (bs * ss if last else 0)
        z_ref[tok0:tok0 + ntok, :] = _gelu_tanh(
            ybuf_ref[0:ntok, :] + d_skip * x_ref[tok0:tok0 + ntok, :]).astype(z_ref.dtype)


def s5_mixer_core(hn, ops, d_skip, h0_re, h0_im, *, bp, sp, bs, ss):
    bd, w_st, w_ro, a_pow = ops
    m, d = hn.shape
    ln = V7X_LANES
    nl = d // ln
    t = S5_BLOCK
    p8 = a_pow.shape[-1]
    p = p8 // S5_GPT
    assert bp % V7X_SUBLANES == 0 and bs % V7X_SUBLANES == 0 and bp % 2 == 0
    assert sp % t == 0 and ss % t == 0 and m == bp * sp + bs * ss and 2 * p == ln
    ncp, ncs = sp // t, ss // t
    pitch = ncp + V7X_BF16_ROWS - ncp % V7X_BF16_ROWS
    rows = bp * pitch + bs * ncs
    rows += (-rows) % (2 * V7X_BF16_ROWS)
    def slabbed(a):
        return a.astype(F32).reshape(bs, nl, p8).transpose(1, 0, 2)
    h0 = jnp.concatenate([slabbed(h0_re), slabbed(h0_im)], axis=-1)
    nslab = 2 * p8 // ln
    z, xfp, xfs = pl.pallas_call(
        functools.partial(_s5_kernel, bp=bp, sp=sp, bs=bs, ss=ss, pitch=pitch),
        out_shape=(jax.ShapeDtypeStruct((m, d), BF16),
                   jax.ShapeDtypeStruct((nl, bp, 2 * p8), F32),
                   jax.ShapeDtypeStruct((nl, bs, 2 * p8), F32)),
        grid=(nl,),
        in_specs=[pl.BlockSpec((m, ln), lambda l: (0, l), pipeline_mode=pl.Buffered(1)),
                  pl.BlockSpec((1, t, ln, ln), lambda l: (l, 0, 0, 0)),
                  pl.BlockSpec((1, t * ln, 4 * p), lambda l: (l, 0, 0)),
                  pl.BlockSpec((1, t * ln, 4 * p), lambda l: (l, 0, 0)),
                  pl.BlockSpec((1, 2, p8), lambda l: (l, 0, 0)),
                  pl.BlockSpec((1, bs, 2 * p8), lambda l: (l, 0, 0)),
                  pl.BlockSpec((1, ln), lambda l: (0, l))],
        out_specs=(pl.BlockSpec((m, ln), lambda l: (0, l)),
                   pl.BlockSpec((1, bp, 2 * p8), lambda l: (l, 0, 0)),
                   pl.BlockSpec((1, bs, 2 * p8), lambda l: (l, 0, 0))),
        scratch_shapes=[pltpu.VMEM((t * ln, t * ln), BF16),
                        pltpu.VMEM((t * ln, 2 * p8), BF16),
                        pltpu.VMEM((t * ln, 2 * p8), BF16),
                        pltpu.VMEM((rows, t * ln), BF16),
                        pltpu.VMEM((nslab, rows, ln), F32),
                        pltpu.VMEM((2 * sp + bs * ss, ln), F32)],
        compiler_params=pltpu.CompilerParams(dimension_semantics=("arbitrary",),
                                             vmem_limit_bytes=V7X_VMEM_LIMIT),
        name="s5_core",
    )(hn, bd, w_st, w_ro, a_pow, h0, d_skip.reshape(1, d).astype(F32))

    def unslab(a, nb):
        re = a[..., :p8].transpose(1, 0, 2).reshape(nb, nl * S5_GPT, p)
        im = a[..., p8:].transpose(1, 0, 2).reshape(nb, nl * S5_GPT, p)
        return re, im

    return (z,) + unslab(xfp, bp) + unslab(xfs, bs)


QBLK = 2 * CHUNK
KWIN = WINDOW + QBLK
BIAS_W = KWIN + WINDOW
BIAS_LEN = BIAS_W + QBLK
ATTN_GROUP = 4


def _attn_bias_vectors(rel_bias):
    n = np.arange(BIAS_LEN)
    delta = np.where(n < BIAS_LEN - (QBLK - 1), n, n - BIAS_LEN)
    idx = np.clip(WINDOW - delta, -REL_CLIP, REL_CLIP) + REL_CLIP
    return rel_bias.astype(F32)[:, idx][:, None, :]


def _bias_tile(e_row, nq):
    return pltpu.roll(jnp.broadcast_to(e_row, (nq, BIAS_LEN)), 0, 1, stride=1, stride_axis=0)


def _attn_prompt_kernel(q_ref, k_ref, v_ref, e_ref, o_ref, nk_ref, nv_ref, bias_ref, *, seq, keep):
    scale = HEAD_DIM ** -0.5
    qc = lax.broadcasted_iota(jnp.int32, (QBLK, BIAS_LEN), 0) // CHUNK
    kc = lax.broadcasted_iota(jnp.int32, (QBLK, BIAS_LEN), 1) // CHUNK
    tile = _bias_tile(e_ref[0], QBLK)
    bias_ref[...] = jnp.where(kc >= qc, jnp.where(kc <= qc + BAND_CHUNKS, tile, NEG_INF), NEG_INF)
    nblk = seq // QBLK
    for g0 in range(0, nblk, ATTN_GROUP):
        blocks = []
        for i in range(g0, min(g0 + ATTN_GROUP, nblk)):
            q0 = i * QBLK
            start = max(q0 - WINDOW, 0)
            blocks.append((q0, start, min(KWIN, seq - start), start - (q0 - WINDOW)))
        ss = []
        for q0, start, kw, shift in blocks:
            q = q_ref[q0:q0 + QBLK, :].astype(BF16)
            k = k_ref[start:start + kw, :].astype(BF16)
            s = lax.dot_general(q, k, (((1,), (1,)), ((), ())), preferred_element_type=F32)
            ss.append(s * scale + bias_ref[:, shift:shift + kw])
        ms = [jnp.max(s, axis=-1, keepdims=True) for s in ss]
        es = [jnp.exp(s - m) for s, m in zip(ss, ms)]
        ls = [jnp.sum(e, axis=-1, keepdims=True) for e in es]
        for (q0, start, kw, shift), e, l in zip(blocks, es, ls):
            v = v_ref[start:start + kw, :].astype(BF16)
            o = jnp.dot(e.astype(BF16), v, preferred_element_type=F32) * (1.0 / l)
            o_ref[q0:q0 + QBLK, :] = o.astype(o_ref.dtype)
    nk_ref[0] = k_ref[seq - keep:seq, :]
    nv_ref[0] = v_ref[seq - keep:seq, :]


def attn_prompt(qkv, bias_vec, *, bp, sp):
    m, d3 = qkv.shape
    d = d3 // 3
    nh = d // HEAD_DIM
    keep = min(WINDOW, sp)
    assert sp % QBLK == 0
    blk = lambda off: pl.BlockSpec((sp, HEAD_DIM), lambda b, h: (b, h + off))
    tail = pl.BlockSpec((1, keep, HEAD_DIM), lambda b, h: (b, 0, h))
    return pl.pallas_call(
        functools.partial(_attn_prompt_kernel, seq=sp, keep=keep),
        out_shape=(jax.ShapeDtypeStruct((m, d), BF16),
                   jax.ShapeDtypeStruct((bp, keep, d), F32),
                   jax.ShapeDtypeStruct((bp, keep, d), F32)),
        grid=(bp, nh),
        in_specs=[blk(0), blk(nh), blk(2 * nh),
                  pl.BlockSpec((1, 1, BIAS_LEN), lambda b, h: (h, 0, 0))],
        out_specs=(pl.BlockSpec((sp, HEAD_DIM), lambda b, h: (b, h)), tail, tail),
        scratch_shapes=[pltpu.VMEM((QBLK, BIAS_LEN), F32)],
        compiler_params=_cparams(2),
        name="attn_prompt",
    )(qkv, qkv, qkv, bias_vec)


def _attn_sample_kernel(qkv_ref, ck_ref, cv_ref, e_ref, prev_ref, o_ref, *, seq, ncache, nh, mask):
    del prev_ref
    scale = HEAD_DIM ** -0.5
    d = nh * HEAD_DIM
    dn = (((1,), (1,)), ((), ()))
    off = WINDOW - ncache
    for h in range(nh):
        c0 = h * HEAD_DIM
        q = qkv_ref[:, c0:c0 + HEAD_DIM].astype(BF16)
        k = qkv_ref[:, d + c0:d + c0 + HEAD_DIM].astype(BF16)
        v = qkv_ref[:, 2 * d + c0:2 * d + c0 + HEAD_DIM].astype(BF16)
        ck = ck_ref[0, pl.ds(h, ncache, stride=nh), :].astype(BF16)
        cv = cv_ref[0, pl.ds(h, ncache, stride=nh), :].astype(BF16)
        bias = _bias_tile(e_ref[h], seq)
        s1 = lax.dot_general(q, ck, dn, preferred_element_type=F32) * scale + bias[:, off:off + ncache]
        s2 = lax.dot_general(q, k, dn, preferred_element_type=F32) * scale + bias[:, WINDOW:WINDOW + seq]
        if mask is not None:
            s1 = jnp.where(jnp.asarray(mask[:, :ncache]), s1, NEG_INF)
            s2 = jnp.where(jnp.asarray(mask[:, ncache:]), s2, NEG_INF)
        mx = jnp.maximum(jnp.max(s1, axis=-1, keepdims=True), jnp.max(s2, axis=-1, keepdims=True))
        e1 = jnp.exp(s1 - mx)
        e2 = jnp.exp(s2 - mx)
        l = jnp.sum(e1, axis=-1, keepdims=True) + jnp.sum(e2, axis=-1, keepdims=True)
        o = (jnp.dot(e1.astype(BF16), cv, preferred_element_type=F32)
             + jnp.dot(e2.astype(BF16), v, preferred_element_type=F32)) * (1.0 / l)
        o_ref[:, c0:c0 + HEAD_DIM] = o.astype(o_ref.dtype)


def attn_sample(qkv, cache_k, cache_v, bias_vec, o_prev, *, row0, bs, ss):
    m, d3 = qkv.shape
    d = d3 // 3
    nh = d // HEAD_DIM
    ncache = cache_k.shape[1]
    assert ss <= QBLK and ss % V7X_SUBLANES == 0 and ncache <= WINDOW and row0 % ss == 0
    q_pos = PAST_LEN + np.arange(ss)
    k_pos = PAST_LEN - ncache + np.arange(ncache + ss)
    qc, kc = q_pos // CHUNK, k_pos // CHUNK
    allowed = ((k_pos[None, :] >= 0) & (kc[None, :] <= qc[:, None])
               & (kc[None, :] >= qc[:, None] - BAND_CHUNKS))
    mask = None if allowed.all() else allowed
    ck = cache_k.reshape(bs, ncache * nh, HEAD_DIM)
    cv = cache_v.reshape(bs, ncache * nh, HEAD_DIM)
    blk0 = row0 // ss
    cblk = pl.BlockSpec((1, ncache * nh, HEAD_DIM), lambda b: (b, 0, 0))
    return pl.pallas_call(
        functools.partial(_attn_sample_kernel, seq=ss, ncache=ncache, nh=nh, mask=mask),
        out_shape=jax.ShapeDtypeStruct((m, d), BF16),
        grid=(bs,),
        in_specs=[pl.BlockSpec((ss, d3), lambda b: (b + blk0, 0)), cblk, cblk,
                  pl.BlockSpec((nh, 1, BIAS_LEN), lambda b: (0, 0, 0)),
                  pl.BlockSpec(memory_space=pl.ANY)],
        out_specs=pl.BlockSpec((ss, d), lambda b: (b + blk0, 0)),
        input_output_aliases={4: 0},
        compiler_params=_cparams(1),
        name="attn_sample",
    )(qkv, ck, cv, bias_vec, o_prev)


HGRN_HEADS_PER_STEP = 4


def _cumsum_rows(g):
    c = g.shape[0]
    row = lax.broadcasted_iota(jnp.int32, g.shape, 0)
    x = g
    for sh in (1, 2, 4):
        x = x + jnp.where((row & (V7X_SUBLANES - 1)) >= sh, pltpu.roll(x, sh, 0), 0.0)
    pieces = []
    carry = None
    for j in range(c // V7X_SUBLANES):
        blk = x[j * V7X_SUBLANES:(j + 1) * V7X_SUBLANES]
        if carry is not None:
            blk = blk + carry
        pieces.append(blk)
        carry = blk[V7X_SUBLANES - 1:V7X_SUBLANES]
    return jnp.concatenate(pieces, axis=0)


def _hgrn_kernel(*refs, seq, csz, has_state, aliased, hp):
    refs = list(refs)
    q_ref, f_ref, v_ref, z_ref, llb_ref, l1m_ref, gain_ref = refs[:7]
    pos = 7
    s0_ref = None
    if has_state:
        s0_ref = refs[pos]
        pos += 1
    if aliased:
        pos += 1
    o_ref, sn_ref, st_ref = refs[pos:pos + 3]
    kdim = HEAD_DIM
    scale = kdim ** -0.5
    dn_t = (((1,), (1,)), ((), ()))
    for hh in range(hp):
        if has_state:
            st_ref[hh] = s0_ref[0, hh].T
        else:
            st_ref[hh] = jnp.zeros((kdim, kdim), F32)
    row = lax.broadcasted_iota(jnp.int32, (csz, kdim), 0)
    tx = (lax.broadcasted_iota(jnp.int32, (csz, csz), 0)
          ^ lax.broadcasted_iota(jnp.int32, (csz, csz), 1))

    def one_head(r0, hh):
        sl = slice(hh * kdim, (hh + 1) * kdim)
        llb = llb_ref[:, sl]
        l1m = l1m_ref[:, sl]
        qz = q_ref[pl.ds(r0, csz), sl]
        fz = f_ref[pl.ds(r0, csz), sl]
        v = v_ref[pl.ds(r0, csz), sl]
        gz = z_ref[pl.ds(r0, csz), sl]
        ls = jnp.minimum(fz, 0.0) - jnp.log(1.0 + jnp.exp(-jnp.abs(fz)))
        cc = l1m + ls
        g = jnp.maximum(llb, cc) + jnp.log(1.0 + jnp.exp(-jnp.abs(llb - cc)))
        kk = jnp.exp(cc - fz)
        q = qz * _sigmoid(qz) * scale
        b = _cumsum_rows(g)
        v_bf = v.astype(BF16)
        sc = lax.dot_general(q.astype(BF16), kk.astype(BF16), dn_t, preferred_element_type=F32)
        scores = jnp.where(tx == 0, sc, 0.0)
        e = b
        w = 1
        while w < min(V7X_SUBLANES, csz):
            bit = (row & w) != 0
            e_prev = pltpu.roll(e, w, 0)
            ex = jnp.exp(jnp.where(bit, b - e_prev, e - b))
            qw = jnp.where(bit, q * ex, 0.0).astype(BF16)
            kw = jnp.where(bit, 0.0, kk * ex).astype(BF16)
            sc = lax.dot_general(qw, kw, dn_t, preferred_element_type=F32)
            scores = scores + jnp.where(tx < 2 * w, sc, 0.0)
            if 2 * w < V7X_SUBLANES:
                e = jnp.where(bit, e, pltpu.roll(e, csz - w, 0))
            w *= 2
        while w < csz:
            n2 = csz // (2 * w)
            b4 = b.reshape(n2, 2, w, kdim)
            bound = b4[:, 0, w - 1:w, :]
            kpart = kk.reshape(n2, 2, w, kdim)[:, 0] * jnp.exp(bound - b4[:, 0])
            qpart = q.reshape(n2, 2, w, kdim)[:, 1] * jnp.exp(b4[:, 1] - bound)
            zero = jnp.zeros((n2, 1, w, kdim), F32)
            kw = jnp.concatenate([kpart[:, None], zero], axis=1).reshape(csz, kdim).astype(BF16)
            qw = jnp.concatenate([zero, qpart[:, None]], axis=1).reshape(csz, kdim).astype(BF16)
            sc = lax.dot_general(qw, kw, dn_t, preferred_element_type=F32)
            if 2 * w < csz:
                scores = scores + jnp.where(tx < 2 * w, sc, 0.0)
            else:
                scores = scores + sc
            w *= 2
        o = jnp.dot(scores.astype(BF16), v_bf, preferred_element_type=F32)
        st = st_ref[hh]
        o = o + lax.dot_general((q * jnp.exp(b)).astype(BF16), st.astype(BF16), dn_t,
                                preferred_element_type=F32)
        b_last = b[csz - 1:csz]
        kd = (kk * jnp.exp(b_last - b)).astype(BF16)
        st_ref[hh] = jnp.exp(b_last) * st + lax.dot_general(
            v_bf, kd, (((0,), (0,)), ((), ())), preferred_element_type=F32)
        o = o * lax.rsqrt(jnp.mean(o * o, axis=-1, keepdims=True) + RMS_EPS)
        o = o * gain_ref[:, sl] * (gz * _sigmoid(gz))
        o_ref[pl.ds(r0, csz), sl] = o.astype(o_ref.dtype)

    def chunk(ci, _):
        r0 = pl.multiple_of(ci * csz, csz)
        for hh in range(hp):
            one_head(r0, hh)
        return 0

    nchunks = seq // csz
    if nchunks == 1:
        chunk(0, 0)
    else:
        lax.fori_loop(0, nchunks, chunk, 0)
    for hh in range(hp):
        sn_ref[0, hh] = st_ref[hh].T


def hgrn_core(proj, log_lb, log_1m_lb, norm_gain, state, o_prev, *, row0, nb, seq):
    m, d4 = proj.shape
    d = d4 // 4
    nh = d // HEAD_DIM
    hp = HGRN_HEADS_PER_STEP if nh % HGRN_HEADS_PER_STEP == 0 else 1
    ng = nh // hp
    w = hp * HEAD_DIM
    csz = _pick(seq, (128, 64, 32, 16, 8))
    assert row0 % seq == 0
    blk0 = row0 // seq
    has_state = state is not None
    aliased = o_prev is not None
    blk = lambda off: pl.BlockSpec((seq, w), lambda b, h: (b + blk0, h + off * ng))
    vec = pl.BlockSpec((1, w), lambda b, h: (0, h))
    sblk = pl.BlockSpec((1, hp, HEAD_DIM, HEAD_DIM), lambda b, h: (b, h, 0, 0))
    in_specs = [blk(0), blk(1), blk(2), blk(3), vec, vec, vec]
    args = [proj, proj, proj, proj, log_lb.reshape(1, d), log_1m_lb.reshape(1, d),
            norm_gain.reshape(1, d).astype(F32)]
    if has_state:
        in_specs.append(sblk)
        args.append(state.astype(F32))
    aliases = {}
    if aliased:
        aliases = {len(args): 0}
        in_specs.append(pl.BlockSpec(memory_space=pl.ANY))
        args.append(o_prev)
    return pl.pallas_call(
        functools.partial(_hgrn_kernel, seq=seq, csz=csz, has_state=has_state, aliased=aliased, hp=hp),
        out_shape=(jax.ShapeDtypeStruct((m, d), BF16),
                   jax.ShapeDtypeStruct((nb, nh, HEAD_DIM, HEAD_DIM), F32)),
        grid=(nb, ng),
        in_specs=in_specs,
        out_specs=(pl.BlockSpec((seq, w), lambda b, h: (b + blk0, h)), sblk),
        scratch_shapes=[pltpu.VMEM((hp, HEAD_DIM, HEAD_DIM), F32)],
        input_output_aliases=aliases,
        compiler_params=_cparams(2),
        name="hgrn_core",
    )(*args)


def kernel(x_prompt, x_sample, state_s5_re, state_s5_im, cache_attn_k, cache_attn_v, state_hgrn,
           norm_mixer, norm_ffn, norm_final, ffn_w_gate_up, ffn_w_down,
           s5_lambda_re, s5_lambda_im, s5_log_step, s5_b_re, s5_b_im, s5_c_re, s5_c_im, s5_d, s5_w_glu,
           attn_w_qkv, attn_rel_bias, attn_w_o,
           hgrn_w_in, hgrn_lower_bounds, hgrn_norm, hgrn_w_o):
    bp, sp, d = x_prompt.shape
    bs, ss, _ = x_sample.shape
    depth = norm_mixer.shape[0]
    mp = bp * sp
    ms = bs * ss
    m = mp + ms
    d_ff = ffn_w_down.shape[1]
    nh = d // HEAD_DIM

    x = jnp.concatenate([x_prompt.reshape(mp, d), x_sample.reshape(ms, d)], axis=0)
    bm = _pick(m, (1536, 768, 512, 256, 128, 64, 32, 16, 8))
    bm_glu = _pick(m, (768, 512, 256, 128, 64, 32, 16, 8))
    bm_down = _pick(m, (512, 256, 128, 64, 32, 16, 8))
    bn = _pick(d, (256, 128))
    bn_proj = _pick(d, (512, 256, 128))
    bn_down = _pick(d, (512, 256, 128))
    bn_ff = _pick(d_ff, (256, 128))

    w_gate_up = ffn_w_gate_up
    w_down = ffn_w_down.astype(BF16)
    w_glu = s5_w_glu
    w_qkv = attn_w_qkv
    w_ao = attn_w_o
    w_hin = hgrn_w_in
    w_ho = hgrn_w_o

    lbs = jnp.cumsum(jax.nn.softmax(hgrn_lower_bounds.astype(F32), axis=0), axis=0)
    lbs = lbs - lbs[0]

    new_re_p, new_im_p, new_re_s, new_im_s = [], [], [], []
    new_k_p, new_v_p, new_k_s, new_v_s = [], [], [], []
    new_h_p, new_h_s = [], []

    for layer in range(depth):
        kind = layer % N_MIXERS
        j = layer // N_MIXERS
        if kind == 0:
            hn = rmsnorm(x, norm_mixer[layer], F32)
            ops = _s5_operators(s5_lambda_re[j], s5_lambda_im[j], s5_log_step[j],
                                s5_b_re[j], s5_b_im[j], s5_c_re[j], s5_c_im[j])
            z, re_p, im_p, re_s, im_s = s5_mixer_core(
                hn, ops, s5_d[j], state_s5_re[j], state_s5_im[j], bp=bp, sp=sp, bs=bs, ss=ss)
            new_re_p.append(re_p)
            new_im_p.append(im_p)
            new_re_s.append(re_s)
            new_im_s.append(im_s)
            x, xg, ssq = matmul(z, w_glu, j, n_out=d, w_col_offsets=(0, d), epi="glu_res",
                                out_dtype=F32, res=x, bm=bm_glu, bn=bn, next_gain=norm_ffn[layer], name="s5_glu")
        elif kind == 1:
            qkv = matmul(xg, w_qkv, j, n_out=3 * d, w_col_offsets=(0,), epi="none",
                         out_dtype=F32, bm=bm, bn=bn_proj, ssq=ssq, name="attn_qkv")
            bias_vec = _attn_bias_vectors(attn_rel_bias[j])
            o, nk_p, nv_p = attn_prompt(qkv, bias_vec, bp=bp, sp=sp)
            o = attn_sample(qkv, cache_attn_k[j], cache_attn_v[j], bias_vec, o, row0=mp, bs=bs, ss=ss)
            qkv_s = qkv[mp:].reshape(bs, ss, 3 * d)
            new_k_p.append(nk_p.reshape(bp, -1, nh, HEAD_DIM))
            new_v_p.append(nv_p.reshape(bp, -1, nh, HEAD_DIM))
            k_all = jnp.concatenate([cache_attn_k[j], qkv_s[:, :, d:2 * d].reshape(bs, ss, nh, HEAD_DIM)], axis=1)
            v_all = jnp.concatenate([cache_attn_v[j], qkv_s[:, :, 2 * d:].reshape(bs, ss, nh, HEAD_DIM)], axis=1)
            new_k_s.append(k_all[:, ss:])
            new_v_s.append(v_all[:, ss:])
            x, xg, ssq = matmul(o, w_ao, j, n_out=d, w_col_offsets=(0,), epi="res",
                                out_dtype=F32, res=x, bm=bm, bn=bn, next_gain=norm_ffn[layer], name="attn_wo")
        else:
            proj = matmul(xg, w_hin, j, n_out=4 * d, w_col_offsets=(0,), epi="none",
                          out_dtype=F32, bm=bm, bn=bn_proj, ssq=ssq, name="hgrn_win")
            lb = lbs[layer]
            log_lb = jnp.log(lb)
            log_1m_lb = jnp.log1p(-lb)
            o, h_p = hgrn_core(proj, log_lb, log_1m_lb, hgrn_norm[j], None, None, row0=0, nb=bp, seq=sp)
            o, h_s = hgrn_core(proj, log_lb, log_1m_lb, hgrn_norm[j], state_hgrn[j], o, row0=mp, nb=bs, seq=ss)
            new_h_p.append(h_p)
            new_h_s.append(h_s)
            x, xg, ssq = matmul(o, w_ho, j, n_out=d, w_col_offsets=(0,), epi="res",
                                out_dtype=F32, res=x, bm=bm, bn=bn, next_gain=norm_ffn[layer], name="hgrn_wo")
        act = matmul(xg, w_gate_up, layer, n_out=d_ff, w_col_offsets=(0, d_ff),
                     epi="swiglu", out_dtype=BF16, bm=bm, bn=bn_ff, ssq=ssq, name="ffn_gate_up")
        fuse_next = layer + 1 < depth and (layer + 1) % N_MIXERS != 0
        if fuse_next:
            x, xg, ssq = matmul(act, w_down, layer, n_out=d, w_col_offsets=(0,), epi="res",
                                out_dtype=F32, res=x, bm=bm_down, bn=bn_down,
                                next_gain=norm_mixer[layer + 1], name="ffn_down")
        else:
            x = matmul(act, w_down, layer, n_out=d, w_col_offsets=(0,), epi="res",
                       out_dtype=F32, res=x, bm=bm_down, bn=bn_down, w_resident=True, name="ffn_down")

    y_p = rmsnorm(x, norm_final, F32, 0, mp).reshape(bp, sp, d)
    y_s = rmsnorm(x, norm_final, F32, mp, ms).reshape(bs, ss, d)
    return (y_p, y_s,
            jnp.stack(new_re_p), jnp.stack(new_im_p),
            jnp.stack(new_k_p), jnp.stack(new_v_p), jnp.stack(new_h_p),
            jnp.stack(new_re_s), jnp.stack(new_im_s),
            jnp.stack(new_k_s), jnp.stack(new_v_s), jnp.stack(new_h_s))
```

```python
import functools
import math

import numpy as np
import jax
import jax.numpy as jnp
from jax import lax
from jax.experimental import pallas as pl
from jax.experimental.pallas import tpu as pltpu

F32 = jnp.float32
BF16 = jnp.bfloat16

PAST_LEN = 2048
CHUNK = 64
BAND_CHUNKS = 8
WINDOW = BAND_CHUNKS * CHUNK
REL_CLIP = 128
N_MIXERS = 3
S5_GROUP = 16
S5_BLOCK = 16
HEAD_DIM = 128
RMS_EPS = 1e-6
NEG_INF = -1e30

V7X_LANES = 128
V7X_SUBLANES = 8
V7X_BF16_ROWS = 16
V7X_MXU = 256
V7X_VMEM_LIMIT = 56 * 1024 * 1024


def _pick(n, candidates):
    for c in candidates:
        if n % c == 0:
            return c
    return n


def _cparams(n_axes):
    return pltpu.CompilerParams(
        dimension_semantics=("parallel",) * n_axes,
        vmem_limit_bytes=V7X_VMEM_LIMIT)


def _sigmoid(x):
    return 1.0 / (1.0 + jnp.exp(-x))


def _gelu_tanh(x):
    c = math.sqrt(2.0 / math.pi)
    return 0.5 * x * (1.0 + jnp.tanh(c * (x + 0.044715 * (x * x * x))))


def _rmsnorm_kernel(x_ref, g_ref, o_ref):
    x = x_ref[...]
    ms = jnp.mean(x * x, axis=-1, keepdims=True)
    o_ref[...] = (x * lax.rsqrt(ms + RMS_EPS) * g_ref[...]).astype(o_ref.dtype)


def rmsnorm(x, gain, out_dtype, row0=0, rows=None):
    m, d = x.shape
    rows = m if rows is None else rows
    bm = _pick(math.gcd(rows, row0) if row0 else rows, (256, 128, 64, 32, 16, 8))
    blk0 = row0 // bm
    return pl.pallas_call(
        _rmsnorm_kernel,
        out_shape=jax.ShapeDtypeStruct((rows, d), out_dtype),
        grid=(rows // bm,),
        in_specs=[pl.BlockSpec((bm, d), lambda i: (i + blk0, 0)),
                  pl.BlockSpec((1, d), lambda i: (0, 0))],
        out_specs=pl.BlockSpec((bm, d), lambda i: (i, 0)),
        compiler_params=_cparams(1),
        name="rmsnorm",
    )(x, gain.reshape(1, d).astype(F32))


MM_SUB_ROWS = 512
def _mm_kernel(*refs, n_w, epi, has_res, pre_norm, post_norm, k_dim, sub):
    refs = list(refs)
    a_ref = refs.pop(0)
    w_refs = [refs.pop(0) for _ in range(n_w)]
    res_ref = refs.pop(0) if has_res else None
    ssq_in_ref = refs.pop(0) if pre_norm else None
    gain_ref = refs.pop(0) if post_norm else None
    o_ref = refs.pop(0)
    bm, bn = o_ref.shape
    ws = [w if w.dtype == BF16 else w[...].astype(BF16) for w in w_refs]
    if post_norm:
        xg_ref, ssq_ref = refs
    for r0 in range(0, bm, sub):
        rows = slice(r0, r0 + sub)
        a = a_ref[rows, :]
        vals = [jnp.dot(a, w[...], preferred_element_type=F32) for w in ws]
        if pre_norm:
            rstd = lax.rsqrt(ssq_in_ref[rows, :] * (1.0 / k_dim) + RMS_EPS)
            rstd = jnp.concatenate([rstd] * (bn // V7X_LANES), axis=1)
            vals = [v * rstd for v in vals]
        if epi == "none":
            o = vals[0]
        elif epi == "swiglu":
            o = vals[0] * _sigmoid(vals[0]) * vals[1]
        elif epi == "res":
            o = res_ref[rows, :] + vals[0]
        elif epi == "glu_res":
            o = res_ref[rows, :] + vals[0] * _sigmoid(vals[1])
        else:
            raise ValueError(epi)
        o_ref[rows, :] = o.astype(o_ref.dtype)
        if post_norm:
            xg_ref[rows, :] = (o * gain_ref[...]).astype(xg_ref.dtype)
            part = jnp.broadcast_to(jnp.sum(o * o, axis=1, keepdims=True), (sub, V7X_LANES))

            @pl.when(pl.program_id(1) == 0)
            def _():
                ssq_ref[rows, :] = part

            @pl.when(pl.program_id(1) != 0)
            def _():
                ssq_ref[rows, :] += part


def matmul(a, w, layer, *, n_out, w_col_offsets, epi, out_dtype, res=None, bm, bn, w_resident=False,
           ssq=None, next_gain=None, name):
    m, k = a.shape
    n_w = len(w_col_offsets)
    post_norm = next_gain is not None
    assert m % bm == 0 and n_out % bn == 0 and bn % V7X_LANES == 0
    assert all(off % bn == 0 for off in w_col_offsets)
    assert not (w_resident and post_norm)
    if w_resident:
        grid = (n_out // bn, m // bm)
        ij = lambda g0, g1: (g1, g0)
    else:
        grid = (m // bm, n_out // bn)
        ij = lambda g0, g1: (g0, g1)

    def a_map(g0, g1):
        return (ij(g0, g1)[0], 0)

    def w_map(off_blocks):
        return lambda g0, g1: (layer, 0, ij(g0, g1)[1] + off_blocks)

    def o_map(g0, g1):
        return ij(g0, g1)

    in_specs = [pl.BlockSpec((bm, k), a_map)]
    args = [a]
    for off in w_col_offsets:
        in_specs.append(pl.BlockSpec((None, k, bn), w_map(off // bn)))
        args.append(w)
    if res is not None:
        in_specs.append(pl.BlockSpec((bm, bn), o_map))
        args.append(res)
    if ssq is not None:
        in_specs.append(pl.BlockSpec((bm, V7X_LANES), a_map))
        args.append(ssq)
    out_shape = [jax.ShapeDtypeStruct((m, n_out), out_dtype)]
    out_specs = [pl.BlockSpec((bm, bn), o_map)]
    if post_norm:
        in_specs.append(pl.BlockSpec((1, bn), lambda g0, g1: (0, ij(g0, g1)[1])))
        args.append(next_gain.reshape(1, n_out).astype(F32))
        out_shape += [jax.ShapeDtypeStruct((m, n_out), BF16), jax.ShapeDtypeStruct((m, V7X_LANES), F32)]
        out_specs += [pl.BlockSpec((bm, bn), o_map), pl.BlockSpec((bm, V7X_LANES), a_map)]
    sem = ("parallel", "arbitrary") if post_norm else ("parallel", "parallel")
    sub = bm
    if epi == "swiglu" and bm % MM_SUB_ROWS == 0:
        sub = MM_SUB_ROWS
    out = pl.pallas_call(
        functools.partial(_mm_kernel, n_w=n_w, epi=epi, has_res=res is not None,
                          pre_norm=ssq is not None, post_norm=post_norm, k_dim=k, sub=sub),
        out_shape=tuple(out_shape),
        grid=grid,
        in_specs=in_specs,
        out_specs=tuple(out_specs),
        compiler_params=pltpu.CompilerParams(dimension_semantics=sem, vmem_limit_bytes=V7X_VMEM_LIMIT),
        name=name,
    )(*args)
    return out if post_norm else out[0]


S5_GPT = V7X_LANES // S5_GROUP


def _s5_operators(lam_re, lam_im, log_step, b_re, b_im, c_re, c_im):
    g, p = lam_re.shape
    j = b_re.shape[-1]
    t = S5_BLOCK
    nl = g // S5_GPT
    hp = lax.Precision.HIGHEST
    step = jnp.exp(log_step.astype(F32))[:, None]
    lam_re = lam_re.astype(F32)
    lam_im = lam_im.astype(F32)
    lr = lam_re * step
    li = lam_im * step
    tau = jnp.arange(t + 1, dtype=F32)[:, None, None]
    mag = jnp.exp(tau * lr[None])
    e_re = mag * jnp.cos(tau * li[None])
    e_im = mag * jnp.sin(tau * li[None])
    lbar_re, lbar_im = e_re[1], e_im[1]
    denom = lam_re * lam_re + lam_im * lam_im
    num_re = lbar_re - 1.0
    coef_re = (num_re * lam_re + lbar_im * lam_im) / denom
    coef_im = (lbar_im * lam_re - num_re * lam_im) / denom
    bt_re = jnp.swapaxes(b_re.astype(F32), 1, 2)
    bt_im = jnp.swapaxes(b_im.astype(F32), 1, 2)
    bb_re = coef_re[:, None, :] * bt_re - coef_im[:, None, :] * bt_im
    bb_im = coef_re[:, None, :] * bt_im + coef_im[:, None, :] * bt_re
    c_re = c_re.astype(F32)
    c_im = c_im.astype(F32)
    eye = jnp.eye(S5_GPT, dtype=F32)

    a_re = e_re[:, :, None, :] * bb_re[None] - e_im[:, :, None, :] * bb_im[None]
    a_im = e_re[:, :, None, :] * bb_im[None] + e_im[:, :, None, :] * bb_re[None]

    al_re = a_re[:t].reshape(t, nl, S5_GPT * j, p)
    al_im = a_im[:t].reshape(t, nl, S5_GPT * j, p)
    cl_re = c_re.reshape(nl, S5_GPT * j, p)
    cl_im = c_im.reshape(nl, S5_GPT * j, p)
    lagfull = (jnp.einsum("tlip,lop->ltio", al_re, cl_re, precision=hp)
               - jnp.einsum("tlip,lop->ltio", al_im, cl_im, precision=hp))
    blockmask = jnp.kron(eye, jnp.ones((j, j), F32))
    bd = (lagfull * blockmask).astype(BF16)

    parity = (jnp.arange(g) % 2)[None, :, None, None]

    def position(src_re, src_im):
        parts = [jnp.where(parity == h, src, 0.0) for src in (src_re, src_im) for h in (0, 1)]
        out = jnp.concatenate(parts, axis=-1).astype(BF16)
        out = out.reshape(t, nl, S5_GPT, j, 4 * p).transpose(1, 0, 2, 3, 4)
        return out.reshape(nl, t * S5_GPT * j, 4 * p)

    w_st = position(a_re[:t][::-1], a_im[:t][::-1])
    ce_re = c_re[None] * e_re[1:, :, None, :] - c_im[None] * e_im[1:, :, None, :]
    ce_im = c_re[None] * e_im[1:, :, None, :] + c_im[None] * e_re[1:, :, None, :]
    w_ro = position(ce_re, -ce_im)

    a_pow = jnp.stack([e_re[t].reshape(nl, S5_GPT * p), e_im[t].reshape(nl, S5_GPT * p)], axis=1)
    return bd, w_st, w_ro, a_pow


def _s5_kernel(x_ref, bd_ref, wstc_ref, wroc_ref, apow_ref, h0_ref, d_ref,
               z_ref, xfp_ref, xfs_ref,
               tbig_ref, wst_ref, wro_ref, ublk_ref, sst_ref, ybuf_ref, *,
               bp, sp, bs, ss, pitch):
    t = S5_BLOCK
    ln = V7X_LANES
    ncp = sp // t
    ncs = ss // t
    mp = bp * sp
    prow = bp * pitch
    rows = ublk_ref.shape[0]
    nslab = sst_ref.shape[0]
    half = nslab // 2

    @pl.when(pl.program_id(0) == 0)
    def _():
        wst_ref[...] = jnp.zeros_like(wst_ref)
        wro_ref[...] = jnp.zeros_like(wro_ref)

    pair_rows = 2 * S5_GROUP
    for src_ref, dst_ref in ((wstc_ref, wst_ref), (wroc_ref, wro_ref)):
        for s in range(t):
            for k in range(half):
                r0 = s * ln + k * pair_rows
                dst_ref[r0:r0 + pair_rows, k * ln:(k + 1) * ln] = src_ref[0, r0:r0 + pair_rows, 0:ln]
                dst_ref[r0:r0 + pair_rows, (half + k) * ln:(half + k + 1) * ln] = (
                    src_ref[0, r0:r0 + pair_rows, ln:2 * ln])

    for s in range(t):
        for tt in range(s, t):
            tbig_ref[s * ln:(s + 1) * ln, tt * ln:(tt + 1) * ln] = bd_ref[0, tt - s]
    for tt in range(0, t, 2):
        tbig_ref[(tt + 1) * ln:(tt + 2) * ln, tt * ln:(tt + 1) * ln] = jnp.zeros((ln, ln), BF16)

    def gather(b, _):
        r0 = pl.multiple_of(b * pitch, V7X_BF16_ROWS)
        for s in range(t):
            ublk_ref[pl.ds(r0, ncp), s * ln:(s + 1) * ln] = (
                x_ref[pl.ds(b * sp + s, ncp, stride=t), :].astype(BF16))
        ublk_ref[pl.ds(r0 + ncp, pitch - ncp), :] = jnp.zeros((pitch - ncp, t * ln), BF16)
        return 0

    lax.fori_loop(0, bp, gather, 0)
    for s in range(t):
        ublk_ref[prow:prow + bs * ncs, s * ln:(s + 1) * ln] = (
            x_ref[pl.ds(mp + s, bs * ncs, stride=t), :].astype(BF16))
    if rows > prow + bs * ncs:
        ublk_ref[prow + bs * ncs:rows, :] = jnp.zeros((rows - prow - bs * ncs, t * ln), BF16)

    rc = rows // 2
    for r0 in (0, rc):
        inj = jnp.dot(ublk_ref[r0:r0 + rc, :], wst_ref[...], preferred_element_type=F32)
        for k in range(nslab):
            sst_ref[k, r0:r0 + rc, :] = inj[:, k * ln:(k + 1) * ln]

    ar = [apow_ref[0, 0:1, k * ln:(k + 1) * ln] for k in range(half)]
    ai = [apow_ref[0, 1:2, k * ln:(k + 1) * ln] for k in range(half)]

    def advance(xr, xi, loc):
        nr, ni = [], []
        for k in range(half):
            s_r = sst_ref[k, loc, :]
            s_i = sst_ref[half + k, loc, :]
            sst_ref[k, loc, :] = xr[k]
            sst_ref[half + k, loc, :] = xi[k]
            nr.append(ar[k] * xr[k] - ai[k] * xi[k] + s_r)
            ni.append(ar[k] * xi[k] + ai[k] * xr[k] + s_i)
        return tuple(nr), tuple(ni)

    for b0 in range(0, bp, V7X_SUBLANES):
        def pstep(c, carry, b0=b0):
            return advance(carry[0], carry[1], pl.ds(b0 * pitch + c, V7X_SUBLANES, stride=pitch))

        zero = tuple(jnp.zeros((V7X_SUBLANES, ln), F32) for _ in range(half))
        xr, xi = lax.fori_loop(0, ncp, pstep, (zero, zero), unroll=2)
        for k in range(half):
            xfp_ref[0, b0:b0 + V7X_SUBLANES, k * ln:(k + 1) * ln] = xr[k]
            xfp_ref[0, b0:b0 + V7X_SUBLANES, (half + k) * ln:(half + k + 1) * ln] = xi[k]
    for b0 in range(0, bs, V7X_SUBLANES):
        xr = tuple(h0_ref[0, b0:b0 + V7X_SUBLANES, k * ln:(k + 1) * ln] for k in range(half))
        xi = tuple(h0_ref[0, b0:b0 + V7X_SUBLANES, (half + k) * ln:(half + k + 1) * ln] for k in range(half))
        for c in range(ncs):
            xr, xi = advance(xr, xi, pl.ds(prow + b0 * ncs + c, V7X_SUBLANES, stride=ncs))
        for k in range(half):
            xfs_ref[0, b0:b0 + V7X_SUBLANES, k * ln:(k + 1) * ln] = xr[k]
            xfs_ref[0, b0:b0 + V7X_SUBLANES, (half + k) * ln:(half + k + 1) * ln] = xi[k]

    d_skip = d_ref[...]
    npair = bp // 2
    for pr in range(npair):
        last = pr == npair - 1
        r0 = pr * 2 * pitch
        nr = (rows - r0) if last else 2 * pitch
        xs = jnp.concatenate([sst_ref[k, r0:r0 + nr, :] for k in range(nslab)], axis=1).astype(BF16)
        for tp in range(t * ln // V7X_MXU):
            c0 = tp * V7X_MXU
            kk = c0 + V7X_MXU
            yc = (jnp.dot(ublk_ref[r0:r0 + nr, 0:kk], tbig_ref[0:kk, c0:c0 + V7X_MXU],
                          preferred_element_type=F32)
                  + lax.dot_general(xs, wro_ref[c0:c0 + V7X_MXU, :], (((1,), (1,)), ((), ())),
                                    preferred_element_type=F32))
            for sub in range(V7X_MXU // ln):
                tok = tp * (V7X_MXU // ln) + sub
                for q in range(2):
                    ybuf_ref[pl.ds(q * sp + tok, ncp, stride=t), :] = (
                        yc[q * pitch:q * pitch + ncp, sub * ln:(sub + 1) * ln])
                if last:
                    ybuf_ref[pl.ds(2 * sp + tok, bs * ncs, stride=t), :] = (
                        yc[2 * pitch:2 * pitch + bs * ncs, sub * ln:(sub + 1) * ln])
        tok0 = pr * 2 * sp
        ntok = 2 * sp + (bs * ss if last else 0)
        z_ref[tok0:tok0 + ntok, :] = _gelu_tanh(
            ybuf_ref[0:ntok, :] + d_skip * x_ref[tok0:tok0 + ntok, :]).astype(z_ref.dtype)


def s5_mixer_core(hn, ops, d_skip, h0_re, h0_im, *, bp, sp, bs, ss):
    bd, w_st, w_ro, a_pow = ops
    m, d = hn.shape
    ln = V7X_LANES
    nl = d // ln
    t = S5_BLOCK
    p8 = a_pow.shape[-1]
    p = p8 // S5_GPT
    assert bp % V7X_SUBLANES == 0 and bs % V7X_SUBLANES == 0 and bp % 2 == 0
    assert sp % t == 0 and ss % t == 0 and m == bp * sp + bs * ss and 2 * p == ln
    ncp, ncs = sp // t, ss // t
    pitch = ncp + V7X_BF16_ROWS - ncp % V7X_BF16_ROWS
    rows = bp * pitch + bs * ncs
    rows += (-rows) % (2 * V7X_BF16_ROWS)
    def slabbed(a):
        return a.astype(F32).reshape(bs, nl, p8).transpose(1, 0, 2)
    h0 = jnp.concatenate([slabbed(h0_re), slabbed(h0_im)], axis=-1)
    nslab = 2 * p8 // ln
    z, xfp, xfs = pl.pallas_call(
        functools.partial(_s5_kernel, bp=bp, sp=sp, bs=bs, ss=ss, pitch=pitch),
        out_shape=(jax.ShapeDtypeStruct((m, d), BF16),
                   jax.ShapeDtypeStruct((nl, bp, 2 * p8), F32),
                   jax.ShapeDtypeStruct((nl, bs, 2 * p8), F32)),
        grid=(nl,),
        in_specs=[pl.BlockSpec((m, ln), lambda l: (0, l), pipeline_mode=pl.Buffered(1)),
                  pl.BlockSpec((1, t, ln, ln), lambda l: (l, 0, 0, 0)),
                  pl.BlockSpec((1, t * ln, 4 * p), lambda l: (l, 0, 0)),
                  pl.BlockSpec((1, t * ln, 4 * p), lambda l: (l, 0, 0)),
                  pl.BlockSpec((1, 2, p8), lambda l: (l, 0, 0)),
                  pl.BlockSpec((1, bs, 2 * p8), lambda l: (l, 0, 0)),
                  pl.BlockSpec((1, ln), lambda l: (0, l))],
        out_specs=(pl.BlockSpec((m, ln), lambda l: (0, l)),
                   pl.BlockSpec((1, bp, 2 * p8), lambda l: (l, 0, 0)),
                   pl.BlockSpec((1, bs, 2 * p8), lambda l: (l, 0, 0))),
        scratch_shapes=[pltpu.VMEM((t * ln, t * ln), BF16),
                        pltpu.VMEM((t * ln, 2 * p8), BF16),
                        pltpu.VMEM((t * ln, 2 * p8), BF16),
                        pltpu.VMEM((rows, t * ln), BF16),
                        pltpu.VMEM((nslab, rows, ln), F32),
                        pltpu.VMEM((2 * sp + bs * ss, ln), F32)],
        compiler_params=pltpu.CompilerParams(dimension_semantics=("arbitrary",),
                                             vmem_limit_bytes=V7X_VMEM_LIMIT),
        name="s5_core",
    )(hn, bd, w_st, w_ro, a_pow, h0, d_skip.reshape(1, d).astype(F32))

    def unslab(a, nb):
        re = a[..., :p8].transpose(1, 0, 2).reshape(nb, nl * S5_GPT, p)
        im = a[..., p8:].transpose(1, 0, 2).reshape(nb, nl * S5_GPT, p)
        return re, im

    return (z,) + unslab(xfp, bp) + unslab(xfs, bs)


QBLK = 2 * CHUNK
KWIN = WINDOW + QBLK
BIAS_W = KWIN + WINDOW
BIAS_LEN = BIAS_W + QBLK
ATTN_GROUP = 4


def _attn_bias_vectors(rel_bias):
    n = np.arange(BIAS_LEN)
    delta = np.where(n < BIAS_LEN - (QBLK - 1), n, n - BIAS_LEN)
    idx = np.clip(WINDOW - delta, -REL_CLIP, REL_CLIP) + REL_CLIP
    return rel_bias.astype(F32)[:, idx][:, None, :]


def _bias_tile(e_row, nq):
    return pltpu.roll(jnp.broadcast_to(e_row, (nq, BIAS_LEN)), 0, 1, stride=1, stride_axis=0)


def _attn_prompt_kernel(q_ref, k_ref, v_ref, e_ref, o_ref, nk_ref, nv_ref, bias_ref, *, seq, keep):
    scale = HEAD_DIM ** -0.5
    qc = lax.broadcasted_iota(jnp.int32, (QBLK, BIAS_LEN), 0) // CHUNK
    kc = lax.broadcasted_iota(jnp.int32, (QBLK, BIAS_LEN), 1) // CHUNK
    tile = _bias_tile(e_ref[0], QBLK)
    bias_ref[...] = jnp.where(kc >= qc, jnp.where(kc <= qc + BAND_CHUNKS, tile, NEG_INF), NEG_INF)
    nblk = seq // QBLK
    for g0 in range(0, nblk, ATTN_GROUP):
        blocks = []
        for i in range(g0, min(g0 + ATTN_GROUP, nblk)):
            q0 = i * QBLK
            start = max(q0 - WINDOW, 0)
            blocks.append((q0, start, min(KWIN, seq - start), start - (q0 - WINDOW)))
        ss = []
        for q0, start, kw, shift in blocks:
            q = q_ref[q0:q0 + QBLK, :].astype(BF16)
            k = k_ref[start:start + kw, :].astype(BF16)
            s = lax.dot_general(q, k, (((1,), (1,)), ((), ())), preferred_element_type=F32)
            ss.append(s * scale + bias_ref[:, shift:shift + kw])
        ms = [jnp.max(s, axis=-1, keepdims=True) for s in ss]
        es = [jnp.exp(s - m) for s, m in zip(ss, ms)]
        ls = [jnp.sum(e, axis=-1, keepdims=True) for e in es]
        for (q0, start, kw, shift), e, l in zip(blocks, es, ls):
            v = v_ref[start:start + kw, :].astype(BF16)
            o = jnp.dot(e.astype(BF16), v, preferred_element_type=F32) * (1.0 / l)
            o_ref[q0:q0 + QBLK, :] = o.astype(o_ref.dtype)
    nk_ref[0] = k_ref[seq - keep:seq, :]
    nv_ref[0] = v_ref[seq - keep:seq, :]


def attn_prompt(qkv, bias_vec, *, bp, sp):
    m, d3 = qkv.shape
    d = d3 // 3
    nh = d // HEAD_DIM
    keep = min(WINDOW, sp)
    assert sp % QBLK == 0
    blk = lambda off: pl.BlockSpec((sp, HEAD_DIM), lambda b, h: (b, h + off))
    tail = pl.BlockSpec((1, keep, HEAD_DIM), lambda b, h: (b, 0, h))
    return pl.pallas_call(
        functools.partial(_attn_prompt_kernel, seq=sp, keep=keep),
        out_shape=(jax.ShapeDtypeStruct((m, d), BF16),
                   jax.ShapeDtypeStruct((bp, keep, d), F32),
                   jax.ShapeDtypeStruct((bp, keep, d), F32)),
        grid=(bp, nh),
        in_specs=[blk(0), blk(nh), blk(2 * nh),
                  pl.BlockSpec((1, 1, BIAS_LEN), lambda b, h: (h, 0, 0))],
        out_specs=(pl.BlockSpec((sp, HEAD_DIM), lambda b, h: (b, h)), tail, tail),
        scratch_shapes=[pltpu.VMEM((QBLK, BIAS_LEN), F32)],
        compiler_params=_cparams(2),
        name="attn_prompt",
    )(qkv, qkv, qkv, bias_vec)


def _attn_sample_kernel(qkv_ref, ck_ref, cv_ref, e_ref, prev_ref, o_ref, *, seq, ncache, nh, mask):
    del prev_ref
    scale = HEAD_DIM ** -0.5
    d = nh * HEAD_DIM
    dn = (((1,), (1,)), ((), ()))
    off = WINDOW - ncache
    for h0 in range(0, nh, ATTN_GROUP):
        heads = range(h0, min(h0 + ATTN_GROUP, nh))
        s1s, s2s = [], []
        for h in heads:
            c0 = h * HEAD_DIM
            q = qkv_ref[:, c0:c0 + HEAD_DIM].astype(BF16)
            k = qkv_ref[:, d + c0:d + c0 + HEAD_DIM].astype(BF16)
            ck = ck_ref[0, pl.ds(h, ncache, stride=nh), :].astype(BF16)
            bias = _bias_tile(e_ref[h], seq)
            s1 = lax.dot_general(q, ck, dn, preferred_element_type=F32) * scale + bias[:, off:off + ncache]
            s2 = lax.dot_general(q, k, dn, preferred_element_type=F32) * scale + bias[:, WINDOW:WINDOW + seq]
            if mask is not None:
                s1 = jnp.where(jnp.asarray(mask[:, :ncache]), s1, NEG_INF)
                s2 = jnp.where(jnp.asarray(mask[:, ncache:]), s2, NEG_INF)
            s1s.append(s1)
            s2s.append(s2)
        mxs = [jnp.maximum(jnp.max(s1, axis=-1, keepdims=True), jnp.max(s2, axis=-1, keepdims=True))
               for s1, s2 in zip(s1s, s2s)]
        e1s = [jnp.exp(s1 - mx) for s1, mx in zip(s1s, mxs)]
        e2s = [jnp.exp(s2 - mx) for s2, mx in zip(s2s, mxs)]
        ls = [jnp.sum(e1, axis=-1, keepdims=True) + jnp.sum(e2, axis=-1, keepdims=True)
              for e1, e2 in zip(e1s, e2s)]
        for h, e1, e2, l in zip(heads, e1s, e2s, ls):
            c0 = h * HEAD_DIM
            v = qkv_ref[:, 2 * d + c0:2 * d + c0 + HEAD_DIM].astype(BF16)
            cv = cv_ref[0, pl.ds(h, ncache, stride=nh), :].astype(BF16)
            o = (jnp.dot(e1.astype(BF16), cv, preferred_element_type=F32)
                 + jnp.dot(e2.astype(BF16), v, preferred_element_type=F32)) * (1.0 / l)
            o_ref[:, c0:c0 + HEAD_DIM] = o.astype(o_ref.dtype)


def attn_sample(qkv, cache_k, cache_v, bias_vec, o_prev, *, row0, bs, ss):
    m, d3 = qkv.shape
    d = d3 // 3
    nh = d // HEAD_DIM
    ncache = cache_k.shape[1]
    assert ss <= QBLK and ss % V7X_SUBLANES == 0 and ncache <= WINDOW and row0 % ss == 0
    q_pos = PAST_LEN + np.arange(ss)
    k_pos = PAST_LEN - ncache + np.arange(ncache + ss)
    qc, kc = q_pos // CHUNK, k_pos // CHUNK
    allowed = ((k_pos[None, :] >= 0) & (kc[None, :] <= qc[:, None])
               & (kc[None, :] >= qc[:, None] - BAND_CHUNKS))
    mask = None if allowed.all() else allowed
    ck = cache_k.reshape(bs, ncache * nh, HEAD_DIM)
    cv = cache_v.reshape(bs, ncache * nh, HEAD_DIM)
    blk0 = row0 // ss
    cblk = pl.BlockSpec((1, ncache * nh, HEAD_DIM), lambda b: (b, 0, 0))
    return pl.pallas_call(
        functools.partial(_attn_sample_kernel, seq=ss, ncache=ncache, nh=nh, mask=mask),
        out_shape=jax.ShapeDtypeStruct((m, d), BF16),
        grid=(bs,),
        in_specs=[pl.BlockSpec((ss, d3), lambda b: (b + blk0, 0)), cblk, cblk,
                  pl.BlockSpec((nh, 1, BIAS_LEN), lambda b: (0, 0, 0)),
                  pl.BlockSpec(memory_space=pl.ANY)],
        out_specs=pl.BlockSpec((ss, d), lambda b: (b + blk0, 0)),
        input_output_aliases={4: 0},
        compiler_params=_cparams(1),
        name="attn_sample",
    )(qkv, ck, cv, bias_vec, o_prev)


HGRN_HEADS_PER_STEP = 4


def _hgrn_kernel(*refs, seq, csz, has_state, aliased, hp):
    refs = list(refs)
    q_ref, f_ref, v_ref, z_ref, llb_ref, l1m_ref, gain_ref = refs[:7]
    pos = 7
    s0_ref = None
    if has_state:
        s0_ref = refs[pos]
        pos += 1
    if aliased:
        pos += 1
    o_ref, sn_ref, st_ref = refs[pos:pos + 3]
    kdim = HEAD_DIM
    scale = kdim ** -0.5
    dn_t = (((1,), (1,)), ((), ()))
    for hh in range(hp):
        if has_state:
            st_ref[hh] = s0_ref[0, hh].T
        else:
            st_ref[hh] = jnp.zeros((kdim, kdim), F32)
    row = lax.broadcasted_iota(jnp.int32, (csz, kdim), 0)
    tx = (lax.broadcasted_iota(jnp.int32, (csz, csz), 0)
          ^ lax.broadcasted_iota(jnp.int32, (csz, csz), 1))

    tri = jnp.where(lax.broadcasted_iota(jnp.int32, (csz, csz), 0)
                    >= lax.broadcasted_iota(jnp.int32, (csz, csz), 1), 1.0, 0.0).astype(BF16)

    def gates(r0, hh):
        sl = slice(hh * kdim, (hh + 1) * kdim)
        llb = llb_ref[:, sl]
        l1m = l1m_ref[:, sl]
        qz = q_ref[pl.ds(r0, csz), sl]
        fz = f_ref[pl.ds(r0, csz), sl]
        ls = jnp.minimum(fz, 0.0) - jnp.log(1.0 + jnp.exp(jnp.minimum(fz, -fz)))
        cc = l1m + ls
        dd = llb - cc
        g = jnp.maximum(llb, cc) + jnp.log(1.0 + jnp.exp(jnp.minimum(dd, -dd)))
        kk = jnp.exp(cc - fz)
        q = qz * _sigmoid(qz) * scale
        return g, kk, q

    def prefix_sums(gs):
        parts = []
        for g in gs:
            hi = g.astype(BF16)
            r1 = g - hi.astype(F32)
            mid = r1.astype(BF16)
            lo = (r1 - mid.astype(F32)).astype(BF16)
            parts += [hi, mid, lo]
        sums = jnp.dot(tri, jnp.concatenate(parts, axis=1), preferred_element_type=F32)
        out = []
        for i in range(len(gs)):
            c0 = 3 * i * kdim
            out.append(sums[:, c0:c0 + kdim] + sums[:, c0 + kdim:c0 + 2 * kdim]
                       + sums[:, c0 + 2 * kdim:c0 + 3 * kdim])
        return out

    def one_head(r0, hh, b, kk, q):
        sl = slice(hh * kdim, (hh + 1) * kdim)
        v = v_ref[pl.ds(r0, csz), sl]
        gz = z_ref[pl.ds(r0, csz), sl]
        v_bf = v.astype(BF16)
        sc = lax.dot_general(q.astype(BF16), kk.astype(BF16), dn_t, preferred_element_type=F32)
        scores = jnp.where(tx == 0, sc, 0.0)
        e = b
        w = 1
        while w < min(V7X_SUBLANES, csz):
            bit = (row & w) != 0
            e_prev = pltpu.roll(e, w, 0)
            ex = jnp.exp(jnp.where(bit, b - e_prev, e - b))
            qw = jnp.where(bit, q * ex, 0.0).astype(BF16)
            kw = jnp.where(bit, 0.0, kk * ex).astype(BF16)
            sc = lax.dot_general(qw, kw, dn_t, preferred_element_type=F32)
            scores = scores + jnp.where(tx < 2 * w, sc, 0.0)
            if 2 * w < V7X_SUBLANES:
                e = jnp.where(bit, e, pltpu.roll(e, csz - w, 0))
            w *= 2
        while w < csz:
            n2 = csz // (2 * w)
            b4 = b.reshape(n2, 2, w, kdim)
            bound = b4[:, 0, w - 1:w, :]
            kpart = kk.reshape(n2, 2, w, kdim)[:, 0] * jnp.exp(bound - b4[:, 0])
            qpart = q.reshape(n2, 2, w, kdim)[:, 1] * jnp.exp(b4[:, 1] - bound)
            zero = jnp.zeros((n2, 1, w, kdim), F32)
            kw = jnp.concatenate([kpart[:, None], zero], axis=1).reshape(csz, kdim).astype(BF16)
            qw = jnp.concatenate([zero, qpart[:, None]], axis=1).reshape(csz, kdim).astype(BF16)
            sc = lax.dot_general(qw, kw, dn_t, preferred_element_type=F32)
            if 2 * w < csz:
                scores = scores + jnp.where(tx < 2 * w, sc, 0.0)
            else:
                scores = scores + sc
            w *= 2
        o = jnp.dot(scores.astype(BF16), v_bf, preferred_element_type=F32)
        st = st_ref[hh]
        o = o + lax.dot_general((q * jnp.exp(b)).astype(BF16), st.astype(BF16), dn_t,
                                preferred_element_type=F32)
        b_last = b[csz - 1:csz]
        kd = (kk * jnp.exp(b_last - b)).astype(BF16)
        st_ref[hh] = jnp.exp(b_last) * st + lax.dot_general(
            v_bf, kd, (((0,), (0,)), ((), ())), preferred_element_type=F32)
        o = o * lax.rsqrt(jnp.mean(o * o, axis=-1, keepdims=True) + RMS_EPS)
        o = o * gain_ref[:, sl] * (gz * _sigmoid(gz))
        o_ref[pl.ds(r0, csz), sl] = o.astype(o_ref.dtype)

    def chunk(ci, _):
        r0 = pl.multiple_of(ci * csz, csz)
        gkq = [gates(r0, hh) for hh in range(hp)]
        bs = prefix_sums([g for g, _, _ in gkq])
        for hh in range(hp):
            one_head(r0, hh, bs[hh], gkq[hh][1], gkq[hh][2])
        return 0

    nchunks = seq // csz
    if nchunks == 1:
        chunk(0, 0)
    else:
        lax.fori_loop(0, nchunks, chunk, 0)
    for hh in range(hp):
        sn_ref[0, hh] = st_ref[hh].T


def hgrn_core(proj, log_lb, log_1m_lb, norm_gain, state, o_prev, *, row0, nb, seq):
    m, d4 = proj.shape
    d = d4 // 4
    nh = d // HEAD_DIM
    hp = HGRN_HEADS_PER_STEP if nh % HGRN_HEADS_PER_STEP == 0 else 1
    ng = nh // hp
    w = hp * HEAD_DIM
    csz = _pick(seq, (128, 64, 32, 16, 8))
    assert row0 % seq == 0
    blk0 = row0 // seq
    has_state = state is not None
    aliased = o_prev is not None
    blk = lambda off: pl.BlockSpec((seq, w), lambda b, h: (b + blk0, h + off * ng))
    vec = pl.BlockSpec((1, w), lambda b, h: (0, h))
    sblk = pl.BlockSpec((1, hp, HEAD_DIM, HEAD_DIM), lambda b, h: (b, h, 0, 0))
    in_specs = [blk(0), blk(1), blk(2), blk(3), vec, vec, vec]
    args = [proj, proj, proj, proj, log_lb.reshape(1, d), log_1m_lb.reshape(1, d),
            norm_gain.reshape(1, d).astype(F32)]
    if has_state:
        in_specs.append(sblk)
        args.append(state.astype(F32))
    aliases = {}
    if aliased:
        aliases = {len(args): 0}
        in_specs.append(pl.BlockSpec(memory_space=pl.ANY))
        args.append(o_prev)
    return pl.pallas_call(
        functools.partial(_hgrn_kernel, seq=seq, csz=csz, has_state=has_state, aliased=aliased, hp=hp),
        out_shape=(jax.ShapeDtypeStruct((m, d), BF16),
                   jax.ShapeDtypeStruct((nb, nh, HEAD_DIM, HEAD_DIM), F32)),
        grid=(nb, ng),
        in_specs=in_specs,
        out_specs=(pl.BlockSpec((seq, w), lambda b, h: (b + blk0, h)), sblk),
        scratch_shapes=[pltpu.VMEM((hp, HEAD_DIM, HEAD_DIM), F32)],
        input_output_aliases=aliases,
        compiler_params=_cparams(2),
        name="hgrn_core",
    )(*args)


def kernel(x_prompt, x_sample, state_s5_re, state_s5_im, cache_attn_k, cache_attn_v, state_hgrn,
           norm_mixer, norm_ffn, norm_final, ffn_w_gate_up, ffn_w_down,
           s5_lambda_re, s5_lambda_im, s5_log_step, s5_b_re, s5_b_im, s5_c_re, s5_c_im, s5_d, s5_w_glu,
           attn_w_qkv, attn_rel_bias, attn_w_o,
           hgrn_w_in, hgrn_lower_bounds, hgrn_norm, hgrn_w_o):
    bp, sp, d = x_prompt.shape
    bs, ss, _ = x_sample.shape
    depth = norm_mixer.shape[0]
    mp = bp * sp
    ms = bs * ss
    m = mp + ms
    d_ff = ffn_w_down.shape[1]
    nh = d // HEAD_DIM

    x = jnp.concatenate([x_prompt.reshape(mp, d), x_sample.reshape(ms, d)], axis=0)
    bm = _pick(m, (1536, 768, 512, 256, 128, 64, 32, 16, 8))
    bm_down = _pick(m, (512, 256, 128, 64, 32, 16, 8))
    bn = _pick(d, (256, 128))
    bn_proj = _pick(d, (512, 256, 128))
    bn_down = _pick(d, (512, 256, 128))
    bn_ff = _pick(d_ff, (256, 128))

    w_gate_up = ffn_w_gate_up
    w_qkv = attn_w_qkv
    w_hin = hgrn_w_in
    w_down = ffn_w_down.astype(BF16)
    w_glu = s5_w_glu.astype(BF16)
    w_ao = attn_w_o.astype(BF16)
    w_ho = hgrn_w_o.astype(BF16)

    lbs = jnp.cumsum(jax.nn.softmax(hgrn_lower_bounds.astype(F32), axis=0), axis=0)
    lbs = lbs - lbs[0]

    new_re_p, new_im_p, new_re_s, new_im_s = [], [], [], []
    new_k_p, new_v_p, new_k_s, new_v_s = [], [], [], []
    new_h_p, new_h_s = [], []

    for layer in range(depth):
        kind = layer % N_MIXERS
        j = layer // N_MIXERS
        if kind == 0:
            hn = rmsnorm(x, norm_mixer[layer], F32)
            ops = _s5_operators(s5_lambda_re[j], s5_lambda_im[j], s5_log_step[j],
                                s5_b_re[j], s5_b_im[j], s5_c_re[j], s5_c_im[j])
            z, re_p, im_p, re_s, im_s = s5_mixer_core(
                hn, ops, s5_d[j], state_s5_re[j], state_s5_im[j], bp=bp, sp=sp, bs=bs, ss=ss)
            new_re_p.append(re_p)
            new_im_p.append(im_p)
            new_re_s.append(re_s)
            new_im_s.append(im_s)
            x, xg, ssq = matmul(z, w_glu, j, n_out=d, w_col_offsets=(0, d), epi="glu_res",
                                out_dtype=F32, res=x, bm=bm, bn=bn, next_gain=norm_ffn[layer], name="s5_glu")
        elif kind == 1:
            qkv = matmul(xg, w_qkv, j, n_out=3 * d, w_col_offsets=(0,), epi="none",
                         out_dtype=F32, bm=bm, bn=bn_proj, ssq=ssq, name="attn_qkv")
            bias_vec = _attn_bias_vectors(attn_rel_bias[j])
            o, nk_p, nv_p = attn_prompt(qkv, bias_vec, bp=bp, sp=sp)
            o = attn_sample(qkv, cache_attn_k[j], cache_attn_v[j], bias_vec, o, row0=mp, bs=bs, ss=ss)
            qkv_s = qkv[mp:].reshape(bs, ss, 3 * d)
            new_k_p.append(nk_p.reshape(bp, -1, nh, HEAD_DIM))
            new_v_p.append(nv_p.reshape(bp, -1, nh, HEAD_DIM))
            k_all = jnp.concatenate([cache_attn_k[j], qkv_s[:, :, d:2 * d].reshape(bs, ss, nh, HEAD_DIM)], axis=1)
            v_all = jnp.concatenate([cache_attn_v[j], qkv_s[:, :, 2 * d:].reshape(bs, ss, nh, HEAD_DIM)], axis=1)
            new_k_s.append(k_all[:, ss:])
            new_v_s.append(v_all[:, ss:])
            x, xg, ssq = matmul(o, w_ao, j, n_out=d, w_col_offsets=(0,), epi="res",
                                out_dtype=F32, res=x, bm=bm, bn=bn, next_gain=norm_ffn[layer], name="attn_wo")
        else:
            proj = matmul(xg, w_hin, j, n_out=4 * d, w_col_offsets=(0,), epi="none",
                          out_dtype=F32, bm=bm, bn=bn_proj, ssq=ssq, name="hgrn_win")
            lb = lbs[layer]
            log_lb = jnp.log(lb)
            log_1m_lb = jnp.log1p(-lb)
            o, h_p = hgrn_core(proj, log_lb, log_1m_lb, hgrn_norm[j], None, None, row0=0, nb=bp, seq=sp)
            o, h_s = hgrn_core(proj, log_lb, log_1m_lb, hgrn_norm[j], state_hgrn[j], o, row0=mp, nb=bs, seq=ss)
            new_h_p.append(h_p)
            new_h_s.append(h_s)
            x, xg, ssq = matmul(o, w_ho, j, n_out=d, w_col_offsets=(0,), epi="res",
                                out_dtype=F32, res=x, bm=bm, bn=bn, next_gain=norm_ffn[layer], name="hgrn_wo")
        act = matmul(xg, w_gate_up, layer, n_out=d_ff, w_col_offsets=(0, d_ff),
                     epi="swiglu", out_dtype=BF16, bm=bm, bn=bn_ff, ssq=ssq, name="ffn_gate_up")
        fuse_next = layer + 1 < depth and (layer + 1) % N_MIXERS != 0
        if fuse_next:
            x, xg, ssq = matmul(act, w_down, layer, n_out=d, w_col_offsets=(0,), epi="res",
                                out_dtype=F32, res=x, bm=bm_down, bn=bn_down,
                                next_gain=norm_mixer[layer + 1], name="ffn_down")
        else:
            x = matmul(act, w_down, layer, n_out=d, w_col_offsets=(0,), epi="res",
                       out_dtype=F32, res=x, bm=bm_down, bn=bn_down, w_resident=True, name="ffn_down")

    y_p = rmsnorm(x, norm_final, F32, 0, mp).reshape(bp, sp, d)
    y_s = rmsnorm(x, norm_final, F32, mp, ms).reshape(bs, ss, d)
    return (y_p, y_s,
            jnp.stack(new_re_p), jnp.stack(new_im_p),
            jnp.stack(new_k_p), jnp.stack(new_v_p), jnp.stack(new_h_p),
            jnp.stack(new_re_s), jnp.stack(new_im_s),
            jnp.stack(new_k_s), jnp.stack(new_v_s), jnp.stack(new_h_s))
```

```python
import functools
import math

import numpy as np
import jax
import jax.numpy as jnp
from jax import lax
from jax.experimental import pallas as pl
from jax.experimental.pallas import tpu as pltpu

F32 = jnp.float32
BF16 = jnp.bfloat16

PAST_LEN = 2048
CHUNK = 64
BAND_CHUNKS = 8
WINDOW = BAND_CHUNKS * CHUNK
REL_CLIP = 128
N_MIXERS = 3
S5_GROUP = 16
S5_BLOCK = 16
HEAD_DIM = 128
RMS_EPS = 1e-6
NEG_INF = -1e30

V7X_LANES = 128
V7X_SUBLANES = 8
V7X_BF16_ROWS = 16
V7X_MXU = 256
V7X_VMEM_LIMIT = 56 * 1024 * 1024


def _pick(n, candidates):
    for c in candidates:
        if n % c == 0:
            return c
    return n


def _cparams(n_axes):
    return pltpu.CompilerParams(
        dimension_semantics=("parallel",) * n_axes,
        vmem_limit_bytes=V7X_VMEM_LIMIT)


def _sigmoid(x):
    return 1.0 / (1.0 + jnp.exp(-x))


def _gelu_tanh(x):
    c = math.sqrt(2.0 / math.pi)
    return 0.5 * x * (1.0 + jnp.tanh(c * (x + 0.044715 * (x * x * x))))


def _rmsnorm_kernel(x_ref, g_ref, o_ref):
    x = x_ref[...]
    ms = jnp.mean(x * x, axis=-1, keepdims=True)
    o_ref[...] = (x * lax.rsqrt(ms + RMS_EPS) * g_ref[...]).astype(o_ref.dtype)


def rmsnorm(x, gain, out_dtype, row0=0, rows=None):
    m, d = x.shape
    rows = m if rows is None else rows
    bm = _pick(math.gcd(rows, row0) if row0 else rows, (256, 128, 64, 32, 16, 8))
    blk0 = row0 // bm
    return pl.pallas_call(
        _rmsnorm_kernel,
        out_shape=jax.ShapeDtypeStruct((rows, d), out_dtype),
        grid=(rows // bm,),
        in_specs=[pl.BlockSpec((bm, d), lambda i: (i + blk0, 0)),
                  pl.BlockSpec((1, d), lambda i: (0, 0))],
        out_specs=pl.BlockSpec((bm, d), lambda i: (i, 0)),
        compiler_params=_cparams(1),
        name="rmsnorm",
    )(x, gain.reshape(1, d).astype(F32))


MM_SUB_ROWS = 512
def _mm_kernel(*refs, n_w, epi, has_res, pre_norm, post_norm, k_dim, sub):
    refs = list(refs)
    a_ref = refs.pop(0)
    w_refs = [refs.pop(0) for _ in range(n_w)]
    res_ref = refs.pop(0) if has_res else None
    ssq_in_ref = refs.pop(0) if pre_norm else None
    gain_ref = refs.pop(0) if post_norm else None
    o_ref = refs.pop(0)
    bm, bn = o_ref.shape
    ws = [w if w.dtype == BF16 else w[...].astype(BF16) for w in w_refs]
    if post_norm:
        xg_ref, ssq_ref = refs

        @pl.when(pl.program_id(1) == 0)
        def _():
            ssq_ref[...] = jnp.zeros_like(ssq_ref)

    for r0 in range(0, bm, sub):
        rows = slice(r0, r0 + sub)
        a = a_ref[rows, :]
        vals = [jnp.dot(a, w[...], preferred_element_type=F32) for w in ws]
        if pre_norm:
            rstd = lax.rsqrt(ssq_in_ref[rows, :] * (1.0 / k_dim) + RMS_EPS)
            rstd = jnp.concatenate([rstd] * (bn // V7X_LANES), axis=1)
            vals = [v * rstd for v in vals]
        if epi == "none":
            o = vals[0]
        elif epi == "swiglu":
            o = vals[0] * _sigmoid(vals[0]) * vals[1]
        elif epi == "res":
            o = res_ref[rows, :] + vals[0]
        elif epi == "glu_res":
            o = res_ref[rows, :] + vals[0] * _sigmoid(vals[1])
        else:
            raise ValueError(epi)
        o_ref[rows, :] = o.astype(o_ref.dtype)
        if post_norm:
            xg_ref[rows, :] = (o * gain_ref[...]).astype(xg_ref.dtype)
            ssq_ref[rows, :] += jnp.broadcast_to(jnp.sum(o * o, axis=1, keepdims=True), (sub, V7X_LANES))


def matmul(a, w, layer, *, n_out, w_col_offsets, epi, out_dtype, res=None, bm, bn, w_resident=False,
           ssq=None, next_gain=None, name):
    m, k = a.shape
    n_w = len(w_col_offsets)
    post_norm = next_gain is not None
    assert m % bm == 0 and n_out % bn == 0 and bn % V7X_LANES == 0
    assert all(off % bn == 0 for off in w_col_offsets)
    assert not (w_resident and post_norm)
    if w_resident:
        grid = (n_out // bn, m // bm)
        ij = lambda g0, g1: (g1, g0)
    else:
        grid = (m // bm, n_out // bn)
        ij = lambda g0, g1: (g0, g1)

    def a_map(g0, g1):
        return (ij(g0, g1)[0], 0)

    def w_map(off_blocks):
        return lambda g0, g1: (layer, 0, ij(g0, g1)[1] + off_blocks)

    def o_map(g0, g1):
        return ij(g0, g1)

    in_specs = [pl.BlockSpec((bm, k), a_map)]
    args = [a]
    for off in w_col_offsets:
        in_specs.append(pl.BlockSpec((None, k, bn), w_map(off // bn)))
        args.append(w)
    if res is not None:
        in_specs.append(pl.BlockSpec((bm, bn), o_map))
        args.append(res)
    if ssq is not None:
        in_specs.append(pl.BlockSpec((bm, V7X_LANES), a_map))
        args.append(ssq)
    out_shape = [jax.ShapeDtypeStruct((m, n_out), out_dtype)]
    out_specs = [pl.BlockSpec((bm, bn), o_map)]
    if post_norm:
        in_specs.append(pl.BlockSpec((1, bn), lambda g0, g1: (0, ij(g0, g1)[1])))
        args.append(next_gain.reshape(1, n_out).astype(F32))
        out_shape += [jax.ShapeDtypeStruct((m, n_out), BF16), jax.ShapeDtypeStruct((m, V7X_LANES), F32)]
        out_specs += [pl.BlockSpec((bm, bn), o_map), pl.BlockSpec((bm, V7X_LANES), a_map)]
    sem = ("parallel", "arbitrary") if post_norm else ("parallel", "parallel")
    sub = bm
    if (epi == "swiglu" or post_norm) and bm % MM_SUB_ROWS == 0:
        sub = MM_SUB_ROWS
    out = pl.pallas_call(
        functools.partial(_mm_kernel, n_w=n_w, epi=epi, has_res=res is not None,
                          pre_norm=ssq is not None, post_norm=post_norm, k_dim=k, sub=sub),
        out_shape=tuple(out_shape),
        grid=grid,
        in_specs=in_specs,
        out_specs=tuple(out_specs),
        compiler_params=pltpu.CompilerParams(dimension_semantics=sem, vmem_limit_bytes=V7X_VMEM_LIMIT),
        name=name,
    )(*args)
    return out if post_norm else out[0]


S5_GPT = V7X_LANES // S5_GROUP


def _s5_operators(lam_re, lam_im, log_step, b_re, b_im, c_re, c_im):
    g, p = lam_re.shape
    j = b_re.shape[-1]
    t = S5_BLOCK
    nl = g // S5_GPT
    hp = lax.Precision.HIGHEST
    step = jnp.exp(log_step.astype(F32))[:, None]
    lam_re = lam_re.astype(F32)
    lam_im = lam_im.astype(F32)
    lr = lam_re * step
    li = lam_im * step
    tau = jnp.arange(t + 1, dtype=F32)[:, None, None]
    mag = jnp.exp(tau * lr[None])
    e_re = mag * jnp.cos(tau * li[None])
    e_im = mag * jnp.sin(tau * li[None])
    lbar_re, lbar_im = e_re[1], e_im[1]
    denom = lam_re * lam_re + lam_im * lam_im
    num_re = lbar_re - 1.0
    coef_re = (num_re * lam_re + lbar_im * lam_im) / denom
    coef_im = (lbar_im * lam_re - num_re * lam_im) / denom
    bt_re = jnp.swapaxes(b_re.astype(F32), 1, 2)
    bt_im = jnp.swapaxes(b_im.astype(F32), 1, 2)
    bb_re = coef_re[:, None, :] * bt_re - coef_im[:, None, :] * bt_im
    bb_im = coef_re[:, None, :] * bt_im + coef_im[:, None, :] * bt_re
    c_re = c_re.astype(F32)
    c_im = c_im.astype(F32)
    eye = jnp.eye(S5_GPT, dtype=F32)

    a_re = e_re[:, :, None, :] * bb_re[None] - e_im[:, :, None, :] * bb_im[None]
    a_im = e_re[:, :, None, :] * bb_im[None] + e_im[:, :, None, :] * bb_re[None]

    al_re = a_re[:t].reshape(t, nl, S5_GPT * j, p)
    al_im = a_im[:t].reshape(t, nl, S5_GPT * j, p)
    cl_re = c_re.reshape(nl, S5_GPT * j, p)
    cl_im = c_im.reshape(nl, S5_GPT * j, p)
    lagfull = (jnp.einsum("tlip,lop->ltio", al_re, cl_re, precision=hp)
               - jnp.einsum("tlip,lop->ltio", al_im, cl_im, precision=hp))
    blockmask = jnp.kron(eye, jnp.ones((j, j), F32))
    bd = (lagfull * blockmask).astype(BF16)

    parity = (jnp.arange(g) % 2)[None, :, None, None]

    def position(src_re, src_im):
        parts = [jnp.where(parity == h, src, 0.0) for src in (src_re, src_im) for h in (0, 1)]
        out = jnp.concatenate(parts, axis=-1).astype(BF16)
        out = out.reshape(t, nl, S5_GPT, j, 4 * p).transpose(1, 0, 2, 3, 4)
        return out.reshape(nl, t * S5_GPT * j, 4 * p)

    w_st = position(a_re[:t][::-1], a_im[:t][::-1])
    ce_re = c_re[None] * e_re[1:, :, None, :] - c_im[None] * e_im[1:, :, None, :]
    ce_im = c_re[None] * e_im[1:, :, None, :] + c_im[None] * e_re[1:, :, None, :]
    w_ro = position(ce_re, -ce_im)

    a_pow = jnp.stack([e_re[t].reshape(nl, S5_GPT * p), e_im[t].reshape(nl, S5_GPT * p)], axis=1)
    return bd, w_st, w_ro, a_pow


def _s5_kernel(x_ref, bd_ref, wstc_ref, wroc_ref, apow_ref, h0_ref, d_ref,
               z_ref, xfp_ref, xfs_ref,
               tbig_ref, wst_ref, wro_ref, ublk_ref, sst_ref, ybuf_ref, *,
               bp, sp, bs, ss, pitch):
    t = S5_BLOCK
    ln = V7X_LANES
    ncp = sp // t
    ncs = ss // t
    mp = bp * sp
    prow = bp * pitch
    rows = ublk_ref.shape[0]
    nslab = sst_ref.shape[0]
    half = nslab // 2

    @pl.when(pl.program_id(0) == 0)
    def _():
        wst_ref[...] = jnp.zeros_like(wst_ref)
        wro_ref[...] = jnp.zeros_like(wro_ref)

    pair_rows = 2 * S5_GROUP
    for src_ref, dst_ref in ((wstc_ref, wst_ref), (wroc_ref, wro_ref)):
        for s in range(t):
            for k in range(half):
                r0 = s * ln + k * pair_rows
                dst_ref[r0:r0 + pair_rows, k * ln:(k + 1) * ln] = src_ref[0, r0:r0 + pair_rows, 0:ln]
                dst_ref[r0:r0 + pair_rows, (half + k) * ln:(half + k + 1) * ln] = (
                    src_ref[0, r0:r0 + pair_rows, ln:2 * ln])

    for s in range(t):
        for tt in range(s, t):
            tbig_ref[s * ln:(s + 1) * ln, tt * ln:(tt + 1) * ln] = bd_ref[0, tt - s]
    for tt in range(0, t, 2):
        tbig_ref[(tt + 1) * ln:(tt + 2) * ln, tt * ln:(tt + 1) * ln] = jnp.zeros((ln, ln), BF16)

    def gather(b, _):
        r0 = pl.multiple_of(b * pitch, V7X_BF16_ROWS)
        for s in range(t):
            ublk_ref[pl.ds(r0, ncp), s * ln:(s + 1) * ln] = (
                x_ref[pl.ds(b * sp + s, ncp, stride=t), :].astype(BF16))
        ublk_ref[pl.ds(r0 + ncp, pitch - ncp), :] = jnp.zeros((pitch - ncp, t * ln), BF16)
        return 0

    lax.fori_loop(0, bp, gather, 0)
    for s in range(t):
        ublk_ref[prow:prow + bs * ncs, s * ln:(s + 1) * ln] = (
            x_ref[pl.ds(mp + s, bs * ncs, stride=t), :].astype(BF16))
    if rows > prow + bs * ncs:
        ublk_ref[prow + bs * ncs:rows, :] = jnp.zeros((rows - prow - bs * ncs, t * ln), BF16)

    rc = rows // 2
    for r0 in (0, rc):
        inj = jnp.dot(ublk_ref[r0:r0 + rc, :], wst_ref[...], preferred_element_type=F32)
        for k in range(nslab):
            sst_ref[k, r0:r0 + rc, :] = inj[:, k * ln:(k + 1) * ln]

    ar = [apow_ref[0, 0:1, k * ln:(k + 1) * ln] for k in range(half)]
    ai = [apow_ref[0, 1:2, k * ln:(k + 1) * ln] for k in range(half)]

    def advance(xr, xi, loc):
        nr, ni = [], []
        for k in range(half):
            s_r = sst_ref[k, loc, :]
            s_i = sst_ref[half + k, loc, :]
            sst_ref[k, loc, :] = xr[k]
            sst_ref[half + k, loc, :] = xi[k]
            nr.append(ar[k] * xr[k] - ai[k] * xi[k] + s_r)
            ni.append(ar[k] * xi[k] + ai[k] * xr[k] + s_i)
        return tuple(nr), tuple(ni)

    for b0 in range(0, bp, V7X_SUBLANES):
        def pstep(c, carry, b0=b0):
            return advance(carry[0], carry[1], pl.ds(b0 * pitch + c, V7X_SUBLANES, stride=pitch))

        zero = tuple(jnp.zeros((V7X_SUBLANES, ln), F32) for _ in range(half))
        xr, xi = lax.fori_loop(0, ncp, pstep, (zero, zero), unroll=2)
        for k in range(half):
            xfp_ref[0, b0:b0 + V7X_SUBLANES, k * ln:(k + 1) * ln] = xr[k]
            xfp_ref[0, b0:b0 + V7X_SUBLANES, (half + k) * ln:(half + k + 1) * ln] = xi[k]
    for b0 in range(0, bs, V7X_SUBLANES):
        xr = tuple(h0_ref[0, b0:b0 + V7X_SUBLANES, k * ln:(k + 1) * ln] for k in range(half))
        xi = tuple(h0_ref[0, b0:b0 + V7X_SUBLANES, (half + k) * ln:(half + k + 1) * ln] for k in range(half))
        for c in range(ncs):
            xr, xi = advance(xr, xi, pl.ds(prow + b0 * ncs + c, V7X_SUBLANES, stride=ncs))
        for k in range(half):
            xfs_ref[0, b0:b0 + V7X_SUBLANES, k * ln:(k + 1) * ln] = xr[k]
            xfs_ref[0, b0:b0 + V7X_SUBLANES, (half + k) * ln:(half + k + 1) * ln] = xi[k]

    d_skip = d_ref[...]
    npair = bp // 2
    for pr in range(npair):
        last = pr == npair - 1
        r0 = pr * 2 * pitch
        nr = (rows - r0) if last else 2 * pitch
        xs = jnp.concatenate([sst_ref[k, r0:r0 + nr, :] for k in range(nslab)], axis=1).astype(BF16)
        for tp in range(t * ln // V7X_MXU):
            c0 = tp * V7X_MXU
            kk = c0 + V7X_MXU
            yc = (jnp.dot(ublk_ref[r0:r0 + nr, 0:kk], tbig_ref[0:kk, c0:c0 + V7X_MXU],
                          preferred_element_type=F32)
                  + lax.dot_general(xs, wro_ref[c0:c0 + V7X_MXU, :], (((1,), (1,)), ((), ())),
                                    preferred_element_type=F32))
            for sub in range(V7X_MXU // ln):
                tok = tp * (V7X_MXU // ln) + sub
                for q in range(2):
                    ybuf_ref[pl.ds(q * sp + tok, ncp, stride=t), :] = (
                        yc[q * pitch:q * pitch + ncp, sub * ln:(sub + 1) * ln])
                if last:
                    ybuf_ref[pl.ds(2 * sp + tok, bs * ncs, stride=t), :] = (
                        yc[2 * pitch:2 * pitch + bs * ncs, sub * ln:(sub + 1) * ln])
        tok0 = pr * 2 * sp
        ntok = 2 * sp + (bs * ss if last else 0)
        z_ref[tok0:tok0 + ntok, :] = _gelu_tanh(
            ybuf_ref[0:ntok, :] + d_skip * x_ref[tok0:tok0 + ntok, :]).astype(z_ref.dtype)


def s5_mixer_core(hn, ops, d_skip, h0_re, h0_im, *, bp, sp, bs, ss):
    bd, w_st, w_ro, a_pow = ops
    m, d = hn.shape
    ln = V7X_LANES
    nl = d // ln
    t = S5_BLOCK
    p8 = a_pow.shape[-1]
    p = p8 // S5_GPT
    assert bp % V7X_SUBLANES == 0 and bs % V7X_SUBLANES == 0 and bp % 2 == 0
    assert sp % t == 0 and ss % t == 0 and m == bp * sp + bs * ss and 2 * p == ln
    ncp, ncs = sp // t, ss // t
    pitch = ncp + V7X_BF16_ROWS - ncp % V7X_BF16_ROWS
    rows = bp * pitch + bs * ncs
    rows += (-rows) % (2 * V7X_BF16_ROWS)
    def slabbed(a):
        return a.astype(F32).reshape(bs, nl, p8).transpose(1, 0, 2)
    h0 = jnp.concatenate([slabbed(h0_re), slabbed(h0_im)], axis=-1)
    nslab = 2 * p8 // ln
    z, xfp, xfs = pl.pallas_call(
        functools.partial(_s5_kernel, bp=bp, sp=sp, bs=bs, ss=ss, pitch=pitch),
        out_shape=(jax.ShapeDtypeStruct((m, d), BF16),
                   jax.ShapeDtypeStruct((nl, bp, 2 * p8), F32),
                   jax.ShapeDtypeStruct((nl, bs, 2 * p8), F32)),
        grid=(nl,),
        in_specs=[pl.BlockSpec((m, ln), lambda l: (0, l), pipeline_mode=pl.Buffered(1)),
                  pl.BlockSpec((1, t, ln, ln), lambda l: (l, 0, 0, 0)),
                  pl.BlockSpec((1, t * ln, 4 * p), lambda l: (l, 0, 0)),
                  pl.BlockSpec((1, t * ln, 4 * p), lambda l: (l, 0, 0)),
                  pl.BlockSpec((1, 2, p8), lambda l: (l, 0, 0)),
                  pl.BlockSpec((1, bs, 2 * p8), lambda l: (l, 0, 0)),
                  pl.BlockSpec((1, ln), lambda l: (0, l))],
        out_specs=(pl.BlockSpec((m, ln), lambda l: (0, l)),
                   pl.BlockSpec((1, bp, 2 * p8), lambda l: (l, 0, 0)),
                   pl.BlockSpec((1, bs, 2 * p8), lambda l: (l, 0, 0))),
        scratch_shapes=[pltpu.VMEM((t * ln, t * ln), BF16),
                        pltpu.VMEM((t * ln, 2 * p8), BF16),
                        pltpu.VMEM((t * ln, 2 * p8), BF16),
                        pltpu.VMEM((rows, t * ln), BF16),
                        pltpu.VMEM((nslab, rows, ln), F32),
                        pltpu.VMEM((2 * sp + bs * ss, ln), F32)],
        compiler_params=pltpu.CompilerParams(dimension_semantics=("arbitrary",),
                                             vmem_limit_bytes=V7X_VMEM_LIMIT),
        name="s5_core",
    )(hn, bd, w_st, w_ro, a_pow, h0, d_skip.reshape(1, d).astype(F32))

    def unslab(a, nb):
        re = a[..., :p8].transpose(1, 0, 2).reshape(nb, nl * S5_GPT, p)
        im = a[..., p8:].transpose(1, 0, 2).reshape(nb, nl * S5_GPT, p)
        return re, im

    return (z,) + unslab(xfp, bp) + unslab(xfs, bs)


QBLK = 2 * CHUNK
KWIN = WINDOW + QBLK
BIAS_W = KWIN + WINDOW
BIAS_LEN = BIAS_W + QBLK
ATTN_GROUP = 4


def _attn_bias_vectors(rel_bias):
    n = np.arange(BIAS_LEN)
    delta = np.where(n < BIAS_LEN - (QBLK - 1), n, n - BIAS_LEN)
    idx = np.clip(WINDOW - delta, -REL_CLIP, REL_CLIP) + REL_CLIP
    return rel_bias.astype(F32)[:, idx][:, None, :]


def _bias_tile(e_row, nq):
    return pltpu.roll(jnp.broadcast_to(e_row, (nq, BIAS_LEN)), 0, 1, stride=1, stride_axis=0)


def _attn_prompt_kernel(q_ref, k_ref, v_ref, e_ref, o_ref, nk_ref, nv_ref, bias_ref, kbf_ref, vbf_ref,
                        *, seq, keep):
    log2e = math.log2(math.e)
    scale = HEAD_DIM ** -0.5 * log2e
    qc = lax.broadcasted_iota(jnp.int32, (QBLK, BIAS_LEN), 0) // CHUNK
    kc = lax.broadcasted_iota(jnp.int32, (QBLK, BIAS_LEN), 1) // CHUNK
    tile = _bias_tile(e_ref[0], QBLK) * log2e
    bias_ref[...] = jnp.where(kc >= qc, jnp.where(kc <= qc + BAND_CHUNKS, tile, NEG_INF), NEG_INF)
    kbf_ref[...] = k_ref[...].astype(BF16)
    vbf_ref[...] = v_ref[...].astype(BF16)
    nblk = seq // QBLK
    for g0 in range(0, nblk, ATTN_GROUP):
        blocks = []
        for i in range(g0, min(g0 + ATTN_GROUP, nblk)):
            q0 = i * QBLK
            start = max(q0 - WINDOW, 0)
            blocks.append((q0, start, min(KWIN, seq - start), start - (q0 - WINDOW)))
        ss = []
        for q0, start, kw, shift in blocks:
            q = q_ref[q0:q0 + QBLK, :].astype(BF16)
            k = kbf_ref[start:start + kw, :]
            s = lax.dot_general(q, k, (((1,), (1,)), ((), ())), preferred_element_type=F32)
            ss.append(s * scale + bias_ref[:, shift:shift + kw])
        ms = [jnp.max(s, axis=-1, keepdims=True) for s in ss]
        es = [jnp.exp2(s - m) for s, m in zip(ss, ms)]
        ls = [jnp.sum(e, axis=-1, keepdims=True) for e in es]
        for (q0, start, kw, shift), e, l in zip(blocks, es, ls):
            v = vbf_ref[start:start + kw, :]
            o = jnp.dot(e.astype(BF16), v, preferred_element_type=F32) * (1.0 / l)
            o_ref[q0:q0 + QBLK, :] = o.astype(o_ref.dtype)
    nk_ref[0] = k_ref[seq - keep:seq, :]
    nv_ref[0] = v_ref[seq - keep:seq, :]


def attn_prompt(qkv, bias_vec, *, bp, sp):
    m, d3 = qkv.shape
    d = d3 // 3
    nh = d // HEAD_DIM
    keep = min(WINDOW, sp)
    assert sp % QBLK == 0
    blk = lambda off: pl.BlockSpec((sp, HEAD_DIM), lambda b, h: (b, h + off))
    tail = pl.BlockSpec((1, keep, HEAD_DIM), lambda b, h: (b, 0, h))
    return pl.pallas_call(
        functools.partial(_attn_prompt_kernel, seq=sp, keep=keep),
        out_shape=(jax.ShapeDtypeStruct((m, d), BF16),
                   jax.ShapeDtypeStruct((bp, keep, d), F32),
                   jax.ShapeDtypeStruct((bp, keep, d), F32)),
        grid=(bp, nh),
        in_specs=[blk(0), blk(nh), blk(2 * nh),
                  pl.BlockSpec((1, 1, BIAS_LEN), lambda b, h: (h, 0, 0))],
        out_specs=(pl.BlockSpec((sp, HEAD_DIM), lambda b, h: (b, h)), tail, tail),
        scratch_shapes=[pltpu.VMEM((QBLK, BIAS_LEN), F32),
                        pltpu.VMEM((sp, HEAD_DIM), BF16),
                        pltpu.VMEM((sp, HEAD_DIM), BF16)],
        compiler_params=_cparams(2),
        name="attn_prompt",
    )(qkv, qkv, qkv, bias_vec)


def _attn_sample_kernel(qkv_ref, ck_ref, cv_ref, e_ref, prev_ref, o_ref, *, seq, ncache, nh, mask):
    del prev_ref
    scale = HEAD_DIM ** -0.5
    d = nh * HEAD_DIM
    dn = (((1,), (1,)), ((), ()))
    off = WINDOW - ncache
    for h0 in range(0, nh, ATTN_GROUP):
        heads = range(h0, min(h0 + ATTN_GROUP, nh))
        s1s, s2s = [], []
        for h in heads:
            c0 = h * HEAD_DIM
            q = qkv_ref[:, c0:c0 + HEAD_DIM].astype(BF16)
            k = qkv_ref[:, d + c0:d + c0 + HEAD_DIM].astype(BF16)
            ck = ck_ref[0, pl.ds(h, ncache, stride=nh), :].astype(BF16)
            bias = _bias_tile(e_ref[h], seq)
            s1 = lax.dot_general(q, ck, dn, preferred_element_type=F32) * scale + bias[:, off:off + ncache]
            s2 = lax.dot_general(q, k, dn, preferred_element_type=F32) * scale + bias[:, WINDOW:WINDOW + seq]
            if mask is not None:
                s1 = jnp.where(jnp.asarray(mask[:, :ncache]), s1, NEG_INF)
                s2 = jnp.where(jnp.asarray(mask[:, ncache:]), s2, NEG_INF)
            s1s.append(s1)
            s2s.append(s2)
        mxs = [jnp.maximum(jnp.max(s1, axis=-1, keepdims=True), jnp.max(s2, axis=-1, keepdims=True))
               for s1, s2 in zip(s1s, s2s)]
        e1s = [jnp.exp(s1 - mx) for s1, mx in zip(s1s, mxs)]
        e2s = [jnp.exp(s2 - mx) for s2, mx in zip(s2s, mxs)]
        ls = [jnp.sum(e1, axis=-1, keepdims=True) + jnp.sum(e2, axis=-1, keepdims=True)
              for e1, e2 in zip(e1s, e2s)]
        for h, e1, e2, l in zip(heads, e1s, e2s, ls):
            c0 = h * HEAD_DIM
            v = qkv_ref[:, 2 * d + c0:2 * d + c0 + HEAD_DIM].astype(BF16)
            cv = cv_ref[0, pl.ds(h, ncache, stride=nh), :].astype(BF16)
            o = (jnp.dot(e1.astype(BF16), cv, preferred_element_type=F32)
                 + jnp.dot(e2.astype(BF16), v, preferred_element_type=F32)) * (1.0 / l)
            o_ref[:, c0:c0 + HEAD_DIM] = o.astype(o_ref.dtype)


def attn_sample(qkv, cache_k, cache_v, bias_vec, o_prev, *, row0, bs, ss):
    m, d3 = qkv.shape
    d = d3 // 3
    nh = d // HEAD_DIM
    ncache = cache_k.shape[1]
    assert ss <= QBLK and ss % V7X_SUBLANES == 0 and ncache <= WINDOW and row0 % ss == 0
    q_pos = PAST_LEN + np.arange(ss)
    k_pos = PAST_LEN - ncache + np.arange(ncache + ss)
    qc, kc = q_pos // CHUNK, k_pos // CHUNK
    allowed = ((k_pos[None, :] >= 0) & (kc[None, :] <= qc[:, None])
               & (kc[None, :] >= qc[:, None] - BAND_CHUNKS))
    mask = None if allowed.all() else allowed
    ck = cache_k.reshape(bs, ncache * nh, HEAD_DIM)
    cv = cache_v.reshape(bs, ncache * nh, HEAD_DIM)
    blk0 = row0 // ss
    cblk = pl.BlockSpec((1, ncache * nh, HEAD_DIM), lambda b: (b, 0, 0))
    return pl.pallas_call(
        functools.partial(_attn_sample_kernel, seq=ss, ncache=ncache, nh=nh, mask=mask),
        out_shape=jax.ShapeDtypeStruct((m, d), BF16),
        grid=(bs,),
        in_specs=[pl.BlockSpec((ss, d3), lambda b: (b + blk0, 0)), cblk, cblk,
                  pl.BlockSpec((nh, 1, BIAS_LEN), lambda b: (0, 0, 0)),
                  pl.BlockSpec(memory_space=pl.ANY)],
        out_specs=pl.BlockSpec((ss, d), lambda b: (b + blk0, 0)),
        input_output_aliases={4: 0},
        compiler_params=_cparams(1),
        name="attn_sample",
    )(qkv, ck, cv, bias_vec, o_prev)


HGRN_HEADS_PER_STEP = 4


def _hgrn_kernel(*refs, seq, csz, has_state, aliased, hp):
    refs = list(refs)
    q_ref, f_ref, v_ref, z_ref, llb_ref, l1m_ref, gain_ref = refs[:7]
    pos = 7
    s0_ref = None
    if has_state:
        s0_ref = refs[pos]
        pos += 1
    if aliased:
        pos += 1
    o_ref, sn_ref, st_ref = refs[pos:pos + 3]
    kdim = HEAD_DIM
    scale = kdim ** -0.5
    dn_t = (((1,), (1,)), ((), ()))
    for hh in range(hp):
        if has_state:
            st_ref[hh] = s0_ref[0, hh].T
        else:
            st_ref[hh] = jnp.zeros((kdim, kdim), F32)
    row = lax.broadcasted_iota(jnp.int32, (csz, kdim), 0)
    tx = (lax.broadcasted_iota(jnp.int32, (csz, csz), 0)
          ^ lax.broadcasted_iota(jnp.int32, (csz, csz), 1))

    tri = jnp.where(lax.broadcasted_iota(jnp.int32, (csz, csz), 0)
                    >= lax.broadcasted_iota(jnp.int32, (csz, csz), 1), 1.0, 0.0).astype(BF16)

    def gates(r0, hh):
        sl = slice(hh * kdim, (hh + 1) * kdim)
        llb = llb_ref[:, sl]
        l1m = l1m_ref[:, sl]
        qz = q_ref[pl.ds(r0, csz), sl]
        fz = f_ref[pl.ds(r0, csz), sl]
        ls = jnp.minimum(fz, 0.0) - jnp.log(1.0 + jnp.exp(jnp.minimum(fz, -fz)))
        cc = l1m + ls
        dd = llb - cc
        g = jnp.maximum(llb, cc) + jnp.log(1.0 + jnp.exp(jnp.minimum(dd, -dd)))
        kk = jnp.exp(cc - fz)
        q = qz * _sigmoid(qz) * scale
        return g, kk, q

    def prefix_sums(gs):
        parts = []
        for g in gs:
            hi = g.astype(BF16)
            r1 = g - hi.astype(F32)
            mid = r1.astype(BF16)
            lo = (r1 - mid.astype(F32)).astype(BF16)
            parts += [hi, mid, lo]
        sums = jnp.dot(tri, jnp.concatenate(parts, axis=1), preferred_element_type=F32)
        out = []
        for i in range(len(gs)):
            c0 = 3 * i * kdim
            out.append(sums[:, c0:c0 + kdim] + sums[:, c0 + kdim:c0 + 2 * kdim]
                       + sums[:, c0 + 2 * kdim:c0 + 3 * kdim])
        return out

    def one_head(r0, hh, b, kk, q):
        sl = slice(hh * kdim, (hh + 1) * kdim)
        v = v_ref[pl.ds(r0, csz), sl]
        gz = z_ref[pl.ds(r0, csz), sl]
        v_bf = v.astype(BF16)
        sc = lax.dot_general(q.astype(BF16), kk.astype(BF16), dn_t, preferred_element_type=F32)
        scores = jnp.where(tx == 0, sc, 0.0)
        e = b
        w = 1
        while w < min(V7X_SUBLANES, csz):
            bit = (row & w) != 0
            e_prev = pltpu.roll(e, w, 0)
            ex = jnp.exp(jnp.where(bit, b - e_prev, e - b))
            qw = jnp.where(bit, q * ex, 0.0).astype(BF16)
            kw = jnp.where(bit, 0.0, kk * ex).astype(BF16)
            sc = lax.dot_general(qw, kw, dn_t, preferred_element_type=F32)
            scores = scores + jnp.where(tx < 2 * w, sc, 0.0)
            if 2 * w < V7X_SUBLANES:
                e = jnp.where(bit, e, pltpu.roll(e, csz - w, 0))
            w *= 2
        while w < csz:
            n2 = csz // (2 * w)
            b4 = b.reshape(n2, 2, w, kdim)
            bound = b4[:, 0, w - 1:w, :]
            kpart = kk.reshape(n2, 2, w, kdim)[:, 0] * jnp.exp(bound - b4[:, 0])
            qpart = q.reshape(n2, 2, w, kdim)[:, 1] * jnp.exp(b4[:, 1] - bound)
            zero = jnp.zeros((n2, 1, w, kdim), F32)
            kw = jnp.concatenate([kpart[:, None], zero], axis=1).reshape(csz, kdim).astype(BF16)
            qw = jnp.concatenate([zero, qpart[:, None]], axis=1).reshape(csz, kdim).astype(BF16)
            sc = lax.dot_general(qw, kw, dn_t, preferred_element_type=F32)
            if 2 * w < csz:
                scores = scores + jnp.where(tx < 2 * w, sc, 0.0)
            else:
                scores = scores + sc
            w *= 2
        o = jnp.dot(scores.astype(BF16), v_bf, preferred_element_type=F32)
        st = st_ref[hh]
        o = o + lax.dot_general((q * jnp.exp(b)).astype(BF16), st.astype(BF16), dn_t,
                                preferred_element_type=F32)
        b_last = b[csz - 1:csz]
        kd = (kk * jnp.exp(b_last - b)).astype(BF16)
        st_ref[hh] = jnp.exp(b_last) * st + lax.dot_general(
            v_bf, kd, (((0,), (0,)), ((), ())), preferred_element_type=F32)
        o = o * lax.rsqrt(jnp.mean(o * o, axis=-1, keepdims=True) + RMS_EPS)
        o = o * gain_ref[:, sl] * (gz * _sigmoid(gz))
        o_ref[pl.ds(r0, csz), sl] = o.astype(o_ref.dtype)

    def chunk(ci, _):
        r0 = pl.multiple_of(ci * csz, csz)
        gkq = [gates(r0, hh) for hh in range(hp)]
        bs = prefix_sums([g for g, _, _ in gkq])
        for hh in range(hp):
            one_head(r0, hh, bs[hh], gkq[hh][1], gkq[hh][2])
        return 0

    nchunks = seq // csz
    if nchunks == 1:
        chunk(0, 0)
    else:
        lax.fori_loop(0, nchunks, chunk, 0)
    for hh in range(hp):
        sn_ref[0, hh] = st_ref[hh].T


def hgrn_core(proj, log_lb, log_1m_lb, norm_gain, state, o_prev, *, row0, nb, seq):
    m, d4 = proj.shape
    d = d4 // 4
    nh = d // HEAD_DIM
    hp = HGRN_HEADS_PER_STEP if nh % HGRN_HEADS_PER_STEP == 0 else 1
    ng = nh // hp
    w = hp * HEAD_DIM
    csz = _pick(seq, (128, 64, 32, 16, 8))
    assert row0 % seq == 0
    blk0 = row0 // seq
    has_state = state is not None
    aliased = o_prev is not None
    blk = lambda off: pl.BlockSpec((seq, w), lambda b, h: (b + blk0, h + off * ng))
    vec = pl.BlockSpec((1, w), lambda b, h: (0, h))
    sblk = pl.BlockSpec((1, hp, HEAD_DIM, HEAD_DIM), lambda b, h: (b, h, 0, 0))
    in_specs = [blk(0), blk(1), blk(2), blk(3), vec, vec, vec]
    args = [proj, proj, proj, proj, log_lb.reshape(1, d), log_1m_lb.reshape(1, d),
            norm_gain.reshape(1, d).astype(F32)]
    if has_state:
        in_specs.append(sblk)
        args.append(state.astype(F32))
    aliases = {}
    if aliased:
        aliases = {len(args): 0}
        in_specs.append(pl.BlockSpec(memory_space=pl.ANY))
        args.append(o_prev)
    return pl.pallas_call(
        functools.partial(_hgrn_kernel, seq=seq, csz=csz, has_state=has_state, aliased=aliased, hp=hp),
        out_shape=(jax.ShapeDtypeStruct((m, d), BF16),
                   jax.ShapeDtypeStruct((nb, nh, HEAD_DIM, HEAD_DIM), F32)),
        grid=(nb, ng),
        in_specs=in_specs,
        out_specs=(pl.BlockSpec((seq, w), lambda b, h: (b + blk0, h)), sblk),
        scratch_shapes=[pltpu.VMEM((hp, HEAD_DIM, HEAD_DIM), F32)],
        input_output_aliases=aliases,
        compiler_params=_cparams(2),
        name="hgrn_core",
    )(*args)


def kernel(x_prompt, x_sample, state_s5_re, state_s5_im, cache_attn_k, cache_attn_v, state_hgrn,
           norm_mixer, norm_ffn, norm_final, ffn_w_gate_up, ffn_w_down,
           s5_lambda_re, s5_lambda_im, s5_log_step, s5_b_re, s5_b_im, s5_c_re, s5_c_im, s5_d, s5_w_glu,
           attn_w_qkv, attn_rel_bias, attn_w_o,
           hgrn_w_in, hgrn_lower_bounds, hgrn_norm, hgrn_w_o):
    bp, sp, d = x_prompt.shape
    bs, ss, _ = x_sample.shape
    depth = norm_mixer.shape[0]
    mp = bp * sp
    ms = bs * ss
    m = mp + ms
    d_ff = ffn_w_down.shape[1]
    nh = d // HEAD_DIM

    x = jnp.concatenate([x_prompt.reshape(mp, d), x_sample.reshape(ms, d)], axis=0)
    bm = _pick(m, (1536, 768, 512, 256, 128, 64, 32, 16, 8))
    bm_down = _pick(m, (512, 256, 128, 64, 32, 16, 8))
    bn = _pick(d, (256, 128))
    bn_proj = _pick(d, (512, 256, 128))
    bn_down = _pick(d, (512, 256, 128))
    bn_ff = _pick(d_ff, (256, 128))

    w_gate_up = ffn_w_gate_up
    w_qkv = attn_w_qkv
    w_hin = hgrn_w_in
    w_down = ffn_w_down.astype(BF16)
    w_glu = s5_w_glu.astype(BF16)
    w_ao = attn_w_o.astype(BF16)
    w_ho = hgrn_w_o.astype(BF16)

    lbs = jnp.cumsum(jax.nn.softmax(hgrn_lower_bounds.astype(F32), axis=0), axis=0)
    lbs = lbs - lbs[0]

    new_re_p, new_im_p, new_re_s, new_im_s = [], [], [], []
    new_k_p, new_v_p, new_k_s, new_v_s = [], [], [], []
    new_h_p, new_h_s = [], []

    for layer in range(depth):
        kind = layer % N_MIXERS
        j = layer // N_MIXERS
        if kind == 0:
            hn = rmsnorm(x, norm_mixer[layer], F32)
            ops = _s5_operators(s5_lambda_re[j], s5_lambda_im[j], s5_log_step[j],
                                s5_b_re[j], s5_b_im[j], s5_c_re[j], s5_c_im[j])
            z, re_p, im_p, re_s, im_s = s5_mixer_core(
                hn, ops, s5_d[j], state_s5_re[j], state_s5_im[j], bp=bp, sp=sp, bs=bs, ss=ss)
            new_re_p.append(re_p)
            new_im_p.append(im_p)
            new_re_s.append(re_s)
            new_im_s.append(im_s)
            x, xg, ssq = matmul(z, w_glu, j, n_out=d, w_col_offsets=(0, d), epi="glu_res",
                                out_dtype=F32, res=x, bm=bm, bn=bn, next_gain=norm_ffn[layer], name="s5_glu")
        elif kind == 1:
            qkv = matmul(xg, w_qkv, j, n_out=3 * d, w_col_offsets=(0,), epi="none",
                         out_dtype=F32, bm=bm, bn=bn_proj, ssq=ssq, name="attn_qkv")
            bias_vec = _attn_bias_vectors(attn_rel_bias[j])
            o, nk_p, nv_p = attn_prompt(qkv, bias_vec, bp=bp, sp=sp)
            o = attn_sample(qkv, cache_attn_k[j], cache_attn_v[j], bias_vec, o, row0=mp, bs=bs, ss=ss)
            qkv_s = qkv[mp:].reshape(bs, ss, 3 * d)
            new_k_p.append(nk_p.reshape(bp, -1, nh, HEAD_DIM))
            new_v_p.append(nv_p.reshape(bp, -1, nh, HEAD_DIM))
            k_all = jnp.concatenate([cache_attn_k[j], qkv_s[:, :, d:2 * d].reshape(bs, ss, nh, HEAD_DIM)], axis=1)
            v_all = jnp.concatenate([cache_attn_v[j], qkv_s[:, :, 2 * d:].reshape(bs, ss, nh, HEAD_DIM)], axis=1)
            new_k_s.append(k_all[:, ss:])
            new_v_s.append(v_all[:, ss:])
            x, xg, ssq = matmul(o, w_ao, j, n_out=d, w_col_offsets=(0,), epi="res",
                                out_dtype=F32, res=x, bm=bm, bn=bn, next_gain=norm_ffn[layer], name="attn_wo")
        else:
            proj = matmul(xg, w_hin, j, n_out=4 * d, w_col_offsets=(0,), epi="none",
                          out_dtype=F32, bm=bm, bn=bn_proj, ssq=ssq, name="hgrn_win")
            lb = lbs[layer]
            log_lb = jnp.log(lb)
            log_1m_lb = jnp.log1p(-lb)
            o, h_p = hgrn_core(proj, log_lb, log_1m_lb, hgrn_norm[j], None, None, row0=0, nb=bp, seq=sp)
            o, h_s = hgrn_core(proj, log_lb, log_1m_lb, hgrn_norm[j], state_hgrn[j], o, row0=mp, nb=bs, seq=ss)
            new_h_p.append(h_p)
            new_h_s.append(h_s)
            x, xg, ssq = matmul(o, w_ho, j, n_out=d, w_col_offsets=(0,), epi="res",
                                out_dtype=F32, res=x, bm=bm, bn=bn, next_gain=norm_ffn[layer], name="hgrn_wo")
        act = matmul(xg, w_gate_up, layer, n_out=d_ff, w_col_offsets=(0, d_ff),
                     epi="swiglu", out_dtype=BF16, bm=bm, bn=bn_ff, ssq=ssq, name="ffn_gate_up")
        fuse_next = layer + 1 < depth and (layer + 1) % N_MIXERS != 0
        if fuse_next:
            x, xg, ssq = matmul(act, w_down, layer, n_out=d, w_col_offsets=(0,), epi="res",
                                out_dtype=F32, res=x, bm=bm_down, bn=bn_down,
                                next_gain=norm_mixer[layer + 1], name="ffn_down")
        else:
            x = matmul(act, w_down, layer, n_out=d, w_col_offsets=(0,), epi="res",
                       out_dtype=F32, res=x, bm=bm_down, bn=bn_down, w_resident=True, name="ffn_down")

    y_p = rmsnorm(x, norm_final, F32, 0, mp).reshape(bp, sp, d)
    y_s = rmsnorm(x, norm_final, F32, mp, ms).reshape(bs, ss, d)
    return (y_p, y_s,
            jnp.stack(new_re_p), jnp.stack(new_im_p),
            jnp.stack(new_k_p), jnp.stack(new_v_p), jnp.stack(new_h_p),
            jnp.stack(new_re_s), jnp.stack(new_im_s),
            jnp.stack(new_k_s), jnp.stack(new_v_s), jnp.stack(new_h_s))
```

```python
import functools
import math

import numpy as np
import jax
import jax.numpy as jnp
from jax import lax
from jax.experimental import pallas as pl
from jax.experimental.pallas import tpu as pltpu

F32 = jnp.float32
BF16 = jnp.bfloat16

PAST_LEN = 2048
CHUNK = 64
BAND_CHUNKS = 8
WINDOW = BAND_CHUNKS * CHUNK
REL_CLIP = 128
N_MIXERS = 3
S5_GROUP = 16
S5_BLOCK = 16
HEAD_DIM = 128
RMS_EPS = 1e-6
NEG_INF = -1e30

V7X_LANES = 128
V7X_SUBLANES = 8
V7X_BF16_ROWS = 16
V7X_MXU = 256
V7X_VMEM_LIMIT = 56 * 1024 * 1024


def _pick(n, candidates):
    for c in candidates:
        if n % c == 0:
            return c
    return n


def _cparams(n_axes):
    return pltpu.CompilerParams(
        dimension_semantics=("parallel",) * n_axes,
        vmem_limit_bytes=V7X_VMEM_LIMIT)


def _sigmoid(x):
    return 1.0 / (1.0 + jnp.exp(-x))


def _gelu_tanh(x):
    c = math.sqrt(2.0 / math.pi)
    return 0.5 * x * (1.0 + jnp.tanh(c * (x + 0.044715 * (x * x * x))))


def _rmsnorm_kernel(x_ref, g_ref, o_ref):
    x = x_ref[...]
    ms = jnp.mean(x * x, axis=-1, keepdims=True)
    o_ref[...] = (x * lax.rsqrt(ms + RMS_EPS) * g_ref[...]).astype(o_ref.dtype)


def rmsnorm(x, gain, out_dtype, row0=0, rows=None):
    m, d = x.shape
    rows = m if rows is None else rows
    bm = _pick(math.gcd(rows, row0) if row0 else rows, (256, 128, 64, 32, 16, 8))
    blk0 = row0 // bm
    return pl.pallas_call(
        _rmsnorm_kernel,
        out_shape=jax.ShapeDtypeStruct((rows, d), out_dtype),
        grid=(rows // bm,),
        in_specs=[pl.BlockSpec((bm, d), lambda i: (i + blk0, 0)),
                  pl.BlockSpec((1, d), lambda i: (0, 0))],
        out_specs=pl.BlockSpec((bm, d), lambda i: (i, 0)),
        compiler_params=_cparams(1),
        name="rmsnorm",
    )(x, gain.reshape(1, d).astype(F32))


def _rmsnorm_copy_kernel(*refs):
    x_ref, g_ref = refs[:2]
    o_ref, c_ref = refs[-2:]
    x = x_ref[...]
    ms = jnp.mean(x * x, axis=-1, keepdims=True)
    o_ref[...] = x * lax.rsqrt(ms + RMS_EPS) * g_ref[...]
    c_ref[...] = x


def rmsnorm_concat(parts, gain):
    d = parts[0].shape[1]
    m = sum(p.shape[0] for p in parts)
    g2 = gain.reshape(1, d).astype(F32)
    outs = None
    row0 = 0
    for p in parts:
        rows = p.shape[0]
        bm = _pick(math.gcd(rows, row0) if row0 else rows, (256, 128, 64, 32, 16, 8))
        blk0 = row0 // bm
        row = pl.BlockSpec((bm, d), lambda i: (i, 0))
        orow = pl.BlockSpec((bm, d), functools.partial(lambda blk0, i: (i + blk0, 0), blk0))
        in_specs = [row, pl.BlockSpec((1, d), lambda i: (0, 0))]
        args = [p, g2]
        aliases = {}
        if outs is not None:
            in_specs += [pl.BlockSpec(memory_space=pl.ANY)] * 2
            args += list(outs)
            aliases = {2: 0, 3: 1}
        outs = pl.pallas_call(
            _rmsnorm_copy_kernel,
            out_shape=(jax.ShapeDtypeStruct((m, d), F32), jax.ShapeDtypeStruct((m, d), F32)),
            grid=(rows // bm,),
            in_specs=in_specs,
            out_specs=(orow, orow),
            input_output_aliases=aliases,
            compiler_params=_cparams(1),
            name="rmsnorm_concat",
        )(*args)
        row0 += rows
    return outs


MM_SUB_ROWS = 512
def _mm_kernel(*refs, n_w, epi, has_res, pre_norm, post_norm, k_dim, sub):
    refs = list(refs)
    a_ref = refs.pop(0)
    w_refs = [refs.pop(0) for _ in range(n_w)]
    res_ref = refs.pop(0) if has_res else None
    ssq_in_ref = refs.pop(0) if pre_norm else None
    gain_ref = refs.pop(0) if post_norm else None
    o_ref = refs.pop(0)
    bm, bn = o_ref.shape
    ws = [w if w.dtype == BF16 else w[...].astype(BF16) for w in w_refs]
    if post_norm:
        xg_ref, ssq_ref = refs

        @pl.when(pl.program_id(1) == 0)
        def _():
            ssq_ref[...] = jnp.zeros_like(ssq_ref)

    for r0 in range(0, bm, sub):
        rows = slice(r0, r0 + sub)
        a = a_ref[rows, :]
        vals = [jnp.dot(a, w[...], preferred_element_type=F32) for w in ws]
        if pre_norm:
            rstd = lax.rsqrt(ssq_in_ref[rows, :] * (1.0 / k_dim) + RMS_EPS)
            rstd = jnp.concatenate([rstd] * (bn // V7X_LANES), axis=1)
            vals = [v * rstd for v in vals]
        if epi == "none":
            o = vals[0]
        elif epi == "swiglu":
            o = vals[0] * _sigmoid(vals[0]) * vals[1]
        elif epi == "res":
            o = res_ref[rows, :] + vals[0]
        elif epi == "glu_res":
            o = res_ref[rows, :] + vals[0] * _sigmoid(vals[1])
        else:
            raise ValueError(epi)
        o_ref[rows, :] = o.astype(o_ref.dtype)
        if post_norm:
            xg_ref[rows, :] = (o * gain_ref[...]).astype(xg_ref.dtype)
            ssq_ref[rows, :] += jnp.broadcast_to(jnp.sum(o * o, axis=1, keepdims=True), (sub, V7X_LANES))


def matmul(a, w, layer, *, n_out, w_col_offsets, epi, out_dtype, res=None, bm, bn, w_resident=False,
           ssq=None, next_gain=None, name):
    m, k = a.shape
    n_w = len(w_col_offsets)
    post_norm = next_gain is not None
    assert m % bm == 0 and n_out % bn == 0 and bn % V7X_LANES == 0
    assert all(off % bn == 0 for off in w_col_offsets)
    assert not (w_resident and post_norm)
    if w_resident:
        grid = (n_out // bn, m // bm)
        ij = lambda g0, g1: (g1, g0)
    else:
        grid = (m // bm, n_out // bn)
        ij = lambda g0, g1: (g0, g1)

    def a_map(g0, g1):
        return (ij(g0, g1)[0], 0)

    def w_map(off_blocks):
        return lambda g0, g1: (layer, 0, ij(g0, g1)[1] + off_blocks)

    def o_map(g0, g1):
        return ij(g0, g1)

    in_specs = [pl.BlockSpec((bm, k), a_map)]
    args = [a]
    for off in w_col_offsets:
        in_specs.append(pl.BlockSpec((None, k, bn), w_map(off // bn)))
        args.append(w)
    if res is not None:
        in_specs.append(pl.BlockSpec((bm, bn), o_map))
        args.append(res)
    if ssq is not None:
        in_specs.append(pl.BlockSpec((bm, V7X_LANES), a_map))
        args.append(ssq)
    out_shape = [jax.ShapeDtypeStruct((m, n_out), out_dtype)]
    out_specs = [pl.BlockSpec((bm, bn), o_map)]
    if post_norm:
        in_specs.append(pl.BlockSpec((1, bn), lambda g0, g1: (0, ij(g0, g1)[1])))
        args.append(next_gain.reshape(1, n_out).astype(F32))
        out_shape += [jax.ShapeDtypeStruct((m, n_out), BF16), jax.ShapeDtypeStruct((m, V7X_LANES), F32)]
        out_specs += [pl.BlockSpec((bm, bn), o_map), pl.BlockSpec((bm, V7X_LANES), a_map)]
    sem = ("parallel", "arbitrary") if post_norm else ("parallel", "parallel")
    sub = bm
    if (epi == "swiglu" or post_norm) and bm % MM_SUB_ROWS == 0:
        sub = MM_SUB_ROWS
    out = pl.pallas_call(
        functools.partial(_mm_kernel, n_w=n_w, epi=epi, has_res=res is not None,
                          pre_norm=ssq is not None, post_norm=post_norm, k_dim=k, sub=sub),
        out_shape=tuple(out_shape),
        grid=grid,
        in_specs=in_specs,
        out_specs=tuple(out_specs),
        compiler_params=pltpu.CompilerParams(dimension_semantics=sem, vmem_limit_bytes=V7X_VMEM_LIMIT),
        name=name,
    )(*args)
    return out if post_norm else out[0]


S5_GPT = V7X_LANES // S5_GROUP


def _s5_operators(lam_re, lam_im, log_step, b_re, b_im, c_re, c_im):
    g, p = lam_re.shape
    j = b_re.shape[-1]
    t = S5_BLOCK
    nl = g // S5_GPT
    hp = lax.Precision.HIGHEST
    step = jnp.exp(log_step.astype(F32))[:, None]
    lam_re = lam_re.astype(F32)
    lam_im = lam_im.astype(F32)
    lr = lam_re * step
    li = lam_im * step
    tau = jnp.arange(t + 1, dtype=F32)[:, None, None]
    mag = jnp.exp(tau * lr[None])
    e_re = mag * jnp.cos(tau * li[None])
    e_im = mag * jnp.sin(tau * li[None])
    lbar_re, lbar_im = e_re[1], e_im[1]
    denom = lam_re * lam_re + lam_im * lam_im
    num_re = lbar_re - 1.0
    coef_re = (num_re * lam_re + lbar_im * lam_im) / denom
    coef_im = (lbar_im * lam_re - num_re * lam_im) / denom
    bt_re = jnp.swapaxes(b_re.astype(F32), 1, 2)
    bt_im = jnp.swapaxes(b_im.astype(F32), 1, 2)
    bb_re = coef_re[:, None, :] * bt_re - coef_im[:, None, :] * bt_im
    bb_im = coef_re[:, None, :] * bt_im + coef_im[:, None, :] * bt_re
    c_re = c_re.astype(F32)
    c_im = c_im.astype(F32)
    eye = jnp.eye(S5_GPT, dtype=F32)

    a_re = e_re[:, :, None, :] * bb_re[None] - e_im[:, :, None, :] * bb_im[None]
    a_im = e_re[:, :, None, :] * bb_im[None] + e_im[:, :, None, :] * bb_re[None]

    al_re = a_re[:t].reshape(t, nl, S5_GPT * j, p)
    al_im = a_im[:t].reshape(t, nl, S5_GPT * j, p)
    cl_re = c_re.reshape(nl, S5_GPT * j, p)
    cl_im = c_im.reshape(nl, S5_GPT * j, p)
    lagfull = (jnp.einsum("tlip,lop->ltio", al_re, cl_re, precision=hp)
               - jnp.einsum("tlip,lop->ltio", al_im, cl_im, precision=hp))
    blockmask = jnp.kron(eye, jnp.ones((j, j), F32))
    bd = (lagfull * blockmask).astype(BF16)

    parity = (jnp.arange(g) % 2)[None, :, None, None]

    def position(src_re, src_im):
        parts = [jnp.where(parity == h, src, 0.0) for src in (src_re, src_im) for h in (0, 1)]
        out = jnp.concatenate(parts, axis=-1).astype(BF16)
        out = out.reshape(t, nl, S5_GPT, j, 4 * p).transpose(1, 0, 2, 3, 4)
        return out.reshape(nl, t * S5_GPT * j, 4 * p)

    w_st = position(a_re[:t][::-1], a_im[:t][::-1])
    ce_re = c_re[None] * e_re[1:, :, None, :] - c_im[None] * e_im[1:, :, None, :]
    ce_im = c_re[None] * e_im[1:, :, None, :] + c_im[None] * e_re[1:, :, None, :]
    w_ro = position(ce_re, -ce_im)

    a_pow = jnp.stack([e_re[t].reshape(nl, S5_GPT * p), e_im[t].reshape(nl, S5_GPT * p)], axis=1)
    return bd, w_st, w_ro, a_pow


def _s5_kernel(x_ref, bd_ref, wstc_ref, wroc_ref, apow_ref, h0_ref, d_ref,
               z_ref, xfp_ref, xfs_ref,
               tbig_ref, wst_ref, wro_ref, ublk_ref, sst_ref, ybuf_ref, *,
               bp, sp, bs, ss, pitch):
    t = S5_BLOCK
    ln = V7X_LANES
    ncp = sp // t
    ncs = ss // t
    mp = bp * sp
    prow = bp * pitch
    rows = ublk_ref.shape[0]
    nslab = sst_ref.shape[0]
    half = nslab // 2

    @pl.when(pl.program_id(0) == 0)
    def _():
        wst_ref[...] = jnp.zeros_like(wst_ref)
        wro_ref[...] = jnp.zeros_like(wro_ref)

    pair_rows = 2 * S5_GROUP
    for src_ref, dst_ref in ((wstc_ref, wst_ref), (wroc_ref, wro_ref)):
        for s in range(t):
            for k in range(half):
                r0 = s * ln + k * pair_rows
                dst_ref[r0:r0 + pair_rows, k * ln:(k + 1) * ln] = src_ref[0, r0:r0 + pair_rows, 0:ln]
                dst_ref[r0:r0 + pair_rows, (half + k) * ln:(half + k + 1) * ln] = (
                    src_ref[0, r0:r0 + pair_rows, ln:2 * ln])

    for s in range(t):
        for tt in range(s, t):
            tbig_ref[s * ln:(s + 1) * ln, tt * ln:(tt + 1) * ln] = bd_ref[0, tt - s]
    for tt in range(0, t, 2):
        tbig_ref[(tt + 1) * ln:(tt + 2) * ln, tt * ln:(tt + 1) * ln] = jnp.zeros((ln, ln), BF16)

    def gather(b, _):
        r0 = pl.multiple_of(b * pitch, V7X_BF16_ROWS)
        for s in range(t):
            ublk_ref[pl.ds(r0, ncp), s * ln:(s + 1) * ln] = (
                x_ref[pl.ds(b * sp + s, ncp, stride=t), :].astype(BF16))
        ublk_ref[pl.ds(r0 + ncp, pitch - ncp), :] = jnp.zeros((pitch - ncp, t * ln), BF16)
        return 0

    lax.fori_loop(0, bp, gather, 0)
    for s in range(t):
        ublk_ref[prow:prow + bs * ncs, s * ln:(s + 1) * ln] = (
            x_ref[pl.ds(mp + s, bs * ncs, stride=t), :].astype(BF16))
    if rows > prow + bs * ncs:
        ublk_ref[prow + bs * ncs:rows, :] = jnp.zeros((rows - prow - bs * ncs, t * ln), BF16)

    rc = rows // 2
    for r0 in (0, rc):
        inj = jnp.dot(ublk_ref[r0:r0 + rc, :], wst_ref[...], preferred_element_type=F32)
        for k in range(nslab):
            sst_ref[k, r0:r0 + rc, :] = inj[:, k * ln:(k + 1) * ln]

    ar = [apow_ref[0, 0:1, k * ln:(k + 1) * ln] for k in range(half)]
    ai = [apow_ref[0, 1:2, k * ln:(k + 1) * ln] for k in range(half)]

    def advance(xr, xi, loc):
        nr, ni = [], []
        for k in range(half):
            s_r = sst_ref[k, loc, :]
            s_i = sst_ref[half + k, loc, :]
            sst_ref[k, loc, :] = xr[k]
            sst_ref[half + k, loc, :] = xi[k]
            nr.append(ar[k] * xr[k] - ai[k] * xi[k] + s_r)
            ni.append(ar[k] * xi[k] + ai[k] * xr[k] + s_i)
        return tuple(nr), tuple(ni)

    for b0 in range(0, bp, V7X_SUBLANES):
        def pstep(c, carry, b0=b0):
            return advance(carry[0], carry[1], pl.ds(b0 * pitch + c, V7X_SUBLANES, stride=pitch))

        zero = tuple(jnp.zeros((V7X_SUBLANES, ln), F32) for _ in range(half))
        xr, xi = lax.fori_loop(0, ncp, pstep, (zero, zero), unroll=2)
        for k in range(half):
            xfp_ref[0, b0:b0 + V7X_SUBLANES, k * ln:(k + 1) * ln] = xr[k]
            xfp_ref[0, b0:b0 + V7X_SUBLANES, (half + k) * ln:(half + k + 1) * ln] = xi[k]
    for b0 in range(0, bs, V7X_SUBLANES):
        xr = tuple(h0_ref[0, b0:b0 + V7X_SUBLANES, k * ln:(k + 1) * ln] for k in range(half))
        xi = tuple(h0_ref[0, b0:b0 + V7X_SUBLANES, (half + k) * ln:(half + k + 1) * ln] for k in range(half))
        for c in range(ncs):
            xr, xi = advance(xr, xi, pl.ds(prow + b0 * ncs + c, V7X_SUBLANES, stride=ncs))
        for k in range(half):
            xfs_ref[0, b0:b0 + V7X_SUBLANES, k * ln:(k + 1) * ln] = xr[k]
            xfs_ref[0, b0:b0 + V7X_SUBLANES, (half + k) * ln:(half + k + 1) * ln] = xi[k]

    d_skip = d_ref[...]
    npair = bp // 2
    for pr in range(npair):
        last = pr == npair - 1
        r0 = pr * 2 * pitch
        nr = (rows - r0) if last else 2 * pitch
        xs = jnp.concatenate([sst_ref[k, r0:r0 + nr, :] for k in range(nslab)], axis=1).astype(BF16)
        for tp in range(t * ln // V7X_MXU):
            c0 = tp * V7X_MXU
            kk = c0 + V7X_MXU
            yc = (jnp.dot(ublk_ref[r0:r0 + nr, 0:kk], tbig_ref[0:kk, c0:c0 + V7X_MXU],
                          preferred_element_type=F32)
                  + lax.dot_general(xs, wro_ref[c0:c0 + V7X_MXU, :], (((1,), (1,)), ((), ())),
                                    preferred_element_type=F32))
            for sub in range(V7X_MXU // ln):
                tok = tp * (V7X_MXU // ln) + sub
                for q in range(2):
                    ybuf_ref[pl.ds(q * sp + tok, ncp, stride=t), :] = (
                        yc[q * pitch:q * pitch + ncp, sub * ln:(sub + 1) * ln])
                if last:
                    ybuf_ref[pl.ds(2 * sp + tok, bs * ncs, stride=t), :] = (
                        yc[2 * pitch:2 * pitch + bs * ncs, sub * ln:(sub + 1) * ln])
        tok0 = pr * 2 * sp
        ntok = 2 * sp + (bs * ss if last else 0)
        z_ref[tok0:tok0 + ntok, :] = _gelu_tanh(
            ybuf_ref[0:ntok, :] + d_skip * x_ref[tok0:tok0 + ntok, :]).astype(z_ref.dtype)


def s5_mixer_core(hn, ops, d_skip, h0_re, h0_im, *, bp, sp, bs, ss):
    bd, w_st, w_ro, a_pow = ops
    m, d = hn.shape
    ln = V7X_LANES
    nl = d // ln
    t = S5_BLOCK
    p8 = a_pow.shape[-1]
    p = p8 // S5_GPT
    assert bp % V7X_SUBLANES == 0 and bs % V7X_SUBLANES == 0 and bp % 2 == 0
    assert sp % t == 0 and ss % t == 0 and m == bp * sp + bs * ss and 2 * p == ln
    ncp, ncs = sp // t, ss // t
    pitch = ncp + V7X_BF16_ROWS - ncp % V7X_BF16_ROWS
    rows = bp * pitch + bs * ncs
    rows += (-rows) % (2 * V7X_BF16_ROWS)
    def slabbed(a):
        return a.astype(F32).reshape(bs, nl, p8).transpose(1, 0, 2)
    h0 = jnp.concatenate([slabbed(h0_re), slabbed(h0_im)], axis=-1)
    nslab = 2 * p8 // ln
    z, xfp, xfs = pl.pallas_call(
        functools.partial(_s5_kernel, bp=bp, sp=sp, bs=bs, ss=ss, pitch=pitch),
        out_shape=(jax.ShapeDtypeStruct((m, d), BF16),
                   jax.ShapeDtypeStruct((nl, bp, 2 * p8), F32),
                   jax.ShapeDtypeStruct((nl, bs, 2 * p8), F32)),
        grid=(nl,),
        in_specs=[pl.BlockSpec((m, ln), lambda l: (0, l), pipeline_mode=pl.Buffered(1)),
                  pl.BlockSpec((1, t, ln, ln), lambda l: (l, 0, 0, 0)),
                  pl.BlockSpec((1, t * ln, 4 * p), lambda l: (l, 0, 0)),
                  pl.BlockSpec((1, t * ln, 4 * p), lambda l: (l, 0, 0)),
                  pl.BlockSpec((1, 2, p8), lambda l: (l, 0, 0)),
                  pl.BlockSpec((1, bs, 2 * p8), lambda l: (l, 0, 0)),
                  pl.BlockSpec((1, ln), lambda l: (0, l))],
        out_specs=(pl.BlockSpec((m, ln), lambda l: (0, l)),
                   pl.BlockSpec((1, bp, 2 * p8), lambda l: (l, 0, 0)),
                   pl.BlockSpec((1, bs, 2 * p8), lambda l: (l, 0, 0))),
        scratch_shapes=[pltpu.VMEM((t * ln, t * ln), BF16),
                        pltpu.VMEM((t * ln, 2 * p8), BF16),
                        pltpu.VMEM((t * ln, 2 * p8), BF16),
                        pltpu.VMEM((rows, t * ln), BF16),
                        pltpu.VMEM((nslab, rows, ln), F32),
                        pltpu.VMEM((2 * sp + bs * ss, ln), F32)],
        compiler_params=pltpu.CompilerParams(dimension_semantics=("arbitrary",),
                                             vmem_limit_bytes=V7X_VMEM_LIMIT),
        name="s5_core",
    )(hn, bd, w_st, w_ro, a_pow, h0, d_skip.reshape(1, d).astype(F32))

    def unslab(a, nb):
        re = a[..., :p8].transpose(1, 0, 2).reshape(nb, nl * S5_GPT, p)
        im = a[..., p8:].transpose(1, 0, 2).reshape(nb, nl * S5_GPT, p)
        return re, im

    return (z,) + unslab(xfp, bp) + unslab(xfs, bs)


QBLK = 2 * CHUNK
KWIN = WINDOW + QBLK
BIAS_W = KWIN + WINDOW
BIAS_LEN = BIAS_W + QBLK
ATTN_GROUP = 2
ATTN_HEAD_GROUP = 4


def _attn_bias_vectors(rel_bias):
    n = np.arange(BIAS_LEN)
    delta = np.where(n < BIAS_LEN - (QBLK - 1), n, n - BIAS_LEN)
    idx = np.clip(WINDOW - delta, -REL_CLIP, REL_CLIP) + REL_CLIP
    return rel_bias.astype(F32)[:, idx][:, None, :]


def _bias_tile(e_row, nq):
    return pltpu.roll(jnp.broadcast_to(e_row, (nq, BIAS_LEN)), 0, 1, stride=1, stride_axis=0)


def _attn_prompt_kernel(q_ref, k_ref, v_ref, e_ref, o_ref, nk_ref, nv_ref, bias_ref, kbf_ref, vbf_ref,
                        *, seq, keep):
    log2e = math.log2(math.e)
    scale = HEAD_DIM ** -0.5 * log2e
    qc = lax.broadcasted_iota(jnp.int32, (QBLK, BIAS_LEN), 0) // CHUNK
    kc = lax.broadcasted_iota(jnp.int32, (QBLK, BIAS_LEN), 1) // CHUNK
    tile = _bias_tile(e_ref[0], QBLK) * log2e
    bias_ref[...] = jnp.where(kc >= qc, jnp.where(kc <= qc + BAND_CHUNKS, tile, NEG_INF), NEG_INF)
    kbf_ref[...] = k_ref[...].astype(BF16)
    vbf_ref[...] = v_ref[...].astype(BF16)
    nblk = seq // QBLK

    def group_blocks(g0):
        blocks = []
        for i in range(g0, min(g0 + ATTN_GROUP, nblk)):
            q0 = i * QBLK
            start = max(q0 - WINDOW, 0)
            blocks.append((q0, start, min(KWIN, seq - start), start - (q0 - WINDOW)))
        return blocks

    def group_scores(blocks):
        ss = []
        for q0, start, kw, shift in blocks:
            q = q_ref[q0:q0 + QBLK, :].astype(BF16)
            k = kbf_ref[start:start + kw, :]
            s = lax.dot_general(q, k, (((1,), (1,)), ((), ())), preferred_element_type=F32)
            ss.append(s * scale + bias_ref[:, shift:shift + kw])
        return ss

    starts = list(range(0, nblk, ATTN_GROUP))
    ss_next = group_scores(group_blocks(starts[0]))
    for gi, g0 in enumerate(starts):
        blocks = group_blocks(g0)
        ss = ss_next
        if gi + 1 < len(starts):
            ss_next = group_scores(group_blocks(starts[gi + 1]))
        ms = [jnp.max(s, axis=-1, keepdims=True) for s in ss]
        es = [jnp.exp2(s - m) for s, m in zip(ss, ms)]
        ls = [jnp.sum(e, axis=-1, keepdims=True) for e in es]
        for (q0, start, kw, shift), e, l in zip(blocks, es, ls):
            v = vbf_ref[start:start + kw, :]
            o = jnp.dot(e.astype(BF16), v, preferred_element_type=F32) * (1.0 / l)
            o_ref[q0:q0 + QBLK, :] = o.astype(o_ref.dtype)
    nk_ref[0] = k_ref[seq - keep:seq, :]
    nv_ref[0] = v_ref[seq - keep:seq, :]


def attn_prompt(qkv, bias_vec, *, bp, sp):
    m, d3 = qkv.shape
    d = d3 // 3
    nh = d // HEAD_DIM
    keep = min(WINDOW, sp)
    assert sp % QBLK == 0
    blk = lambda off: pl.BlockSpec((sp, HEAD_DIM), lambda b, h: (b, h + off))
    tail = pl.BlockSpec((1, keep, HEAD_DIM), lambda b, h: (b, 0, h))
    return pl.pallas_call(
        functools.partial(_attn_prompt_kernel, seq=sp, keep=keep),
        out_shape=(jax.ShapeDtypeStruct((m, d), BF16),
                   jax.ShapeDtypeStruct((bp, keep, d), F32),
                   jax.ShapeDtypeStruct((bp, keep, d), F32)),
        grid=(bp, nh),
        in_specs=[blk(0), blk(nh), blk(2 * nh),
                  pl.BlockSpec((1, 1, BIAS_LEN), lambda b, h: (h, 0, 0))],
        out_specs=(pl.BlockSpec((sp, HEAD_DIM), lambda b, h: (b, h)), tail, tail),
        scratch_shapes=[pltpu.VMEM((QBLK, BIAS_LEN), F32),
                        pltpu.VMEM((sp, HEAD_DIM), BF16),
                        pltpu.VMEM((sp, HEAD_DIM), BF16)],
        compiler_params=_cparams(2),
        name="attn_prompt",
    )(qkv, qkv, qkv, bias_vec)


def _attn_sample_kernel(qkv_ref, ck_ref, cv_ref, e_ref, prev_ref, o_ref, *, seq, ncache, nh, mask):
    del prev_ref
    scale = HEAD_DIM ** -0.5
    d = nh * HEAD_DIM
    dn = (((1,), (1,)), ((), ()))
    off = WINDOW - ncache
    for h0 in range(0, nh, ATTN_HEAD_GROUP):
        heads = range(h0, min(h0 + ATTN_HEAD_GROUP, nh))
        s1s, s2s = [], []
        for h in heads:
            c0 = h * HEAD_DIM
            q = qkv_ref[:, c0:c0 + HEAD_DIM].astype(BF16)
            k = qkv_ref[:, d + c0:d + c0 + HEAD_DIM].astype(BF16)
            ck = ck_ref[0, pl.ds(h, ncache, stride=nh), :].astype(BF16)
            bias = _bias_tile(e_ref[h], seq)
            s1 = lax.dot_general(q, ck, dn, preferred_element_type=F32) * scale + bias[:, off:off + ncache]
            s2 = lax.dot_general(q, k, dn, preferred_element_type=F32) * scale + bias[:, WINDOW:WINDOW + seq]
            if mask is not None:
                s1 = jnp.where(jnp.asarray(mask[:, :ncache]), s1, NEG_INF)
                s2 = jnp.where(jnp.asarray(mask[:, ncache:]), s2, NEG_INF)
            s1s.append(s1)
            s2s.append(s2)
        mxs = [jnp.maximum(jnp.max(s1, axis=-1, keepdims=True), jnp.max(s2, axis=-1, keepdims=True))
               for s1, s2 in zip(s1s, s2s)]
        e1s = [jnp.exp(s1 - mx) for s1, mx in zip(s1s, mxs)]
        e2s = [jnp.exp(s2 - mx) for s2, mx in zip(s2s, mxs)]
        ls = [jnp.sum(e1, axis=-1, keepdims=True) + jnp.sum(e2, axis=-1, keepdims=True)
              for e1, e2 in zip(e1s, e2s)]
        for h, e1, e2, l in zip(heads, e1s, e2s, ls):
            c0 = h * HEAD_DIM
            v = qkv_ref[:, 2 * d + c0:2 * d + c0 + HEAD_DIM].astype(BF16)
            cv = cv_ref[0, pl.ds(h, ncache, stride=nh), :].astype(BF16)
            o = (jnp.dot(e1.astype(BF16), cv, preferred_element_type=F32)
                 + jnp.dot(e2.astype(BF16), v, preferred_element_type=F32)) * (1.0 / l)
            o_ref[:, c0:c0 + HEAD_DIM] = o.astype(o_ref.dtype)


def attn_sample(qkv, cache_k, cache_v, bias_vec, o_prev, *, row0, bs, ss):
    m, d3 = qkv.shape
    d = d3 // 3
    nh = d // HEAD_DIM
    ncache = cache_k.shape[1]
    assert ss <= QBLK and ss % V7X_SUBLANES == 0 and ncache <= WINDOW and row0 % ss == 0
    q_pos = PAST_LEN + np.arange(ss)
    k_pos = PAST_LEN - ncache + np.arange(ncache + ss)
    qc, kc = q_pos // CHUNK, k_pos // CHUNK
    allowed = ((k_pos[None, :] >= 0) & (kc[None, :] <= qc[:, None])
               & (kc[None, :] >= qc[:, None] - BAND_CHUNKS))
    mask = None if allowed.all() else allowed
    ck = cache_k.reshape(bs, ncache * nh, HEAD_DIM)
    cv = cache_v.reshape(bs, ncache * nh, HEAD_DIM)
    blk0 = row0 // ss
    cblk = pl.BlockSpec((1, ncache * nh, HEAD_DIM), lambda b: (b, 0, 0))
    return pl.pallas_call(
        functools.partial(_attn_sample_kernel, seq=ss, ncache=ncache, nh=nh, mask=mask),
        out_shape=jax.ShapeDtypeStruct((m, d), BF16),
        grid=(bs,),
        in_specs=[pl.BlockSpec((ss, d3), lambda b: (b + blk0, 0)), cblk, cblk,
                  pl.BlockSpec((nh, 1, BIAS_LEN), lambda b: (0, 0, 0)),
                  pl.BlockSpec(memory_space=pl.ANY)],
        out_specs=pl.BlockSpec((ss, d), lambda b: (b + blk0, 0)),
        input_output_aliases={4: 0},
        compiler_params=_cparams(1),
        name="attn_sample",
    )(qkv, ck, cv, bias_vec, o_prev)


HGRN_HEADS_PER_STEP = 4


def _hgrn_kernel(*refs, seq, csz, has_state, aliased, hp):
    refs = list(refs)
    q_ref, f_ref, v_ref, z_ref, llb_ref, l1m_ref, gain_ref = refs[:7]
    pos = 7
    s0_ref = None
    if has_state:
        s0_ref = refs[pos]
        pos += 1
    if aliased:
        pos += 1
    o_ref, sn_ref, st_ref = refs[pos:pos + 3]
    kdim = HEAD_DIM
    scale = kdim ** -0.5
    dn_t = (((1,), (1,)), ((), ()))
    for hh in range(hp):
        if has_state:
            st_ref[hh] = s0_ref[0, hh].T
        else:
            st_ref[hh] = jnp.zeros((kdim, kdim), F32)
    row = lax.broadcasted_iota(jnp.int32, (csz, kdim), 0)
    tx = (lax.broadcasted_iota(jnp.int32, (csz, csz), 0)
          ^ lax.broadcasted_iota(jnp.int32, (csz, csz), 1))

    tri = jnp.where(lax.broadcasted_iota(jnp.int32, (csz, csz), 0)
                    >= lax.broadcasted_iota(jnp.int32, (csz, csz), 1), 1.0, 0.0).astype(BF16)

    def gates(r0, hh):
        sl = slice(hh * kdim, (hh + 1) * kdim)
        llb = llb_ref[:, sl]
        l1m = l1m_ref[:, sl]
        qz = q_ref[pl.ds(r0, csz), sl]
        fz = f_ref[pl.ds(r0, csz), sl]
        ls = jnp.minimum(fz, 0.0) - jnp.log(1.0 + jnp.exp(jnp.minimum(fz, -fz)))
        cc = l1m + ls
        dd = llb - cc
        g = jnp.maximum(llb, cc) + jnp.log(1.0 + jnp.exp(jnp.minimum(dd, -dd)))
        kk = jnp.exp(cc - fz)
        q = qz * _sigmoid(qz) * scale
        return g, kk, q

    def prefix_sums(gs):
        parts = []
        for g in gs:
            hi = g.astype(BF16)
            r1 = g - hi.astype(F32)
            mid = r1.astype(BF16)
            lo = (r1 - mid.astype(F32)).astype(BF16)
            parts += [hi, mid, lo]
        sums = jnp.dot(tri, jnp.concatenate(parts, axis=1), preferred_element_type=F32)
        out = []
        for i in range(len(gs)):
            c0 = 3 * i * kdim
            out.append(sums[:, c0:c0 + kdim] + sums[:, c0 + kdim:c0 + 2 * kdim]
                       + sums[:, c0 + 2 * kdim:c0 + 3 * kdim])
        return out

    def one_head(r0, hh, b, kk, q):
        sl = slice(hh * kdim, (hh + 1) * kdim)
        v = v_ref[pl.ds(r0, csz), sl]
        gz = z_ref[pl.ds(r0, csz), sl]
        v_bf = v.astype(BF16)
        sc = lax.dot_general(q.astype(BF16), kk.astype(BF16), dn_t, preferred_element_type=F32)
        scores = jnp.where(tx == 0, sc, 0.0)
        e = b
        w = 1
        while w < min(V7X_SUBLANES, csz):
            bit = (row & w) != 0
            e_prev = pltpu.roll(e, w, 0)
            ex = jnp.exp(jnp.where(bit, b - e_prev, e - b))
            qw = jnp.where(bit, q * ex, 0.0).astype(BF16)
            kw = jnp.where(bit, 0.0, kk * ex).astype(BF16)
            sc = lax.dot_general(qw, kw, dn_t, preferred_element_type=F32)
            scores = scores + jnp.where(tx < 2 * w, sc, 0.0)
            if 2 * w < V7X_SUBLANES:
                e = jnp.where(bit, e, pltpu.roll(e, csz - w, 0))
            w *= 2
        while w < csz:
            n2 = csz // (2 * w)
            b4 = b.reshape(n2, 2, w, kdim)
            bound = b4[:, 0, w - 1:w, :]
            kpart = kk.reshape(n2, 2, w, kdim)[:, 0] * jnp.exp(bound - b4[:, 0])
            qpart = q.reshape(n2, 2, w, kdim)[:, 1] * jnp.exp(b4[:, 1] - bound)
            zero = jnp.zeros((n2, 1, w, kdim), F32)
            kw = jnp.concatenate([kpart[:, None], zero], axis=1).reshape(csz, kdim).astype(BF16)
            qw = jnp.concatenate([zero, qpart[:, None]], axis=1).reshape(csz, kdim).astype(BF16)
            sc = lax.dot_general(qw, kw, dn_t, preferred_element_type=F32)
            if 2 * w < csz:
                scores = scores + jnp.where(tx < 2 * w, sc, 0.0)
            else:
                scores = scores + sc
            w *= 2
        o = jnp.dot(scores.astype(BF16), v_bf, preferred_element_type=F32)
        st = st_ref[hh]
        o = o + lax.dot_general((q * jnp.exp(b)).astype(BF16), st.astype(BF16), dn_t,
                                preferred_element_type=F32)
        b_last = b[csz - 1:csz]
        kd = (kk * jnp.exp(b_last - b)).astype(BF16)
        st_ref[hh] = jnp.exp(b_last) * st + lax.dot_general(
            v_bf, kd, (((0,), (0,)), ((), ())), preferred_element_type=F32)
        o = o * lax.rsqrt(jnp.mean(o * o, axis=-1, keepdims=True) + RMS_EPS)
        o = o * gain_ref[:, sl] * (gz * _sigmoid(gz))
        o_ref[pl.ds(r0, csz), sl] = o.astype(o_ref.dtype)

    def chunk(ci, _):
        r0 = pl.multiple_of(ci * csz, csz)
        gkq = [gates(r0, hh) for hh in range(hp)]
        bs = prefix_sums([g for g, _, _ in gkq])
        for hh in range(hp):
            one_head(r0, hh, bs[hh], gkq[hh][1], gkq[hh][2])
        return 0

    nchunks = seq // csz
    if nchunks == 1:
        chunk(0, 0)
    else:
        lax.fori_loop(0, nchunks, chunk, 0)
    for hh in range(hp):
        sn_ref[0, hh] = st_ref[hh].T


def hgrn_core(proj, log_lb, log_1m_lb, norm_gain, state, o_prev, *, row0, nb, seq):
    m, d4 = proj.shape
    d = d4 // 4
    nh = d // HEAD_DIM
    hp = HGRN_HEADS_PER_STEP if nh % HGRN_HEADS_PER_STEP == 0 else 1
    ng = nh // hp
    w = hp * HEAD_DIM
    csz = _pick(seq, (128, 64, 32, 16, 8))
    assert row0 % seq == 0
    blk0 = row0 // seq
    has_state = state is not None
    aliased = o_prev is not None
    blk = lambda off: pl.BlockSpec((seq, w), lambda b, h: (b + blk0, h + off * ng))
    vec = pl.BlockSpec((1, w), lambda b, h: (0, h))
    sblk = pl.BlockSpec((1, hp, HEAD_DIM, HEAD_DIM), lambda b, h: (b, h, 0, 0))
    in_specs = [blk(0), blk(1), blk(2), blk(3), vec, vec, vec]
    args = [proj, proj, proj, proj, log_lb.reshape(1, d), log_1m_lb.reshape(1, d),
            norm_gain.reshape(1, d).astype(F32)]
    if has_state:
        in_specs.append(sblk)
        args.append(state.astype(F32))
    aliases = {}
    if aliased:
        aliases = {len(args): 0}
        in_specs.append(pl.BlockSpec(memory_space=pl.ANY))
        args.append(o_prev)
    return pl.pallas_call(
        functools.partial(_hgrn_kernel, seq=seq, csz=csz, has_state=has_state, aliased=aliased, hp=hp),
        out_shape=(jax.ShapeDtypeStruct((m, d), BF16),
                   jax.ShapeDtypeStruct((nb, nh, HEAD_DIM, HEAD_DIM), F32)),
        grid=(nb, ng),
        in_specs=in_specs,
        out_specs=(pl.BlockSpec((seq, w), lambda b, h: (b + blk0, h)), sblk),
        scratch_shapes=[pltpu.VMEM((hp, HEAD_DIM, HEAD_DIM), F32)],
        input_output_aliases=aliases,
        compiler_params=_cparams(2),
        name="hgrn_core",
    )(*args)


def kernel(x_prompt, x_sample, state_s5_re, state_s5_im, cache_attn_k, cache_attn_v, state_hgrn,
           norm_mixer, norm_ffn, norm_final, ffn_w_gate_up, ffn_w_down,
           s5_lambda_re, s5_lambda_im, s5_log_step, s5_b_re, s5_b_im, s5_c_re, s5_c_im, s5_d, s5_w_glu,
           attn_w_qkv, attn_rel_bias, attn_w_o,
           hgrn_w_in, hgrn_lower_bounds, hgrn_norm, hgrn_w_o):
    bp, sp, d = x_prompt.shape
    bs, ss, _ = x_sample.shape
    depth = norm_mixer.shape[0]
    mp = bp * sp
    ms = bs * ss
    m = mp + ms
    d_ff = ffn_w_down.shape[1]
    nh = d // HEAD_DIM

    assert depth > 0
    x = None
    bm = _pick(m, (1536, 768, 512, 256, 128, 64, 32, 16, 8))
    bm_down = _pick(m, (512, 256, 128, 64, 32, 16, 8))
    bn = _pick(d, (256, 128))
    bn_proj = _pick(d, (512, 256, 128))
    bn_down = _pick(d, (512, 256, 128))
    bn_ff = _pick(d_ff, (256, 128))

    w_gate_up = ffn_w_gate_up
    w_qkv = attn_w_qkv
    w_hin = hgrn_w_in
    w_down = ffn_w_down.astype(BF16)
    w_glu = s5_w_glu.astype(BF16)
    w_ao = attn_w_o.astype(BF16)
    w_ho = hgrn_w_o.astype(BF16)

    lbs = jnp.cumsum(jax.nn.softmax(hgrn_lower_bounds.astype(F32), axis=0), axis=0)
    lbs = lbs - lbs[0]

    new_re_p, new_im_p, new_re_s, new_im_s = [], [], [], []
    new_k_p, new_v_p, new_k_s, new_v_s = [], [], [], []
    new_h_p, new_h_s = [], []

    for layer in range(depth):
        kind = layer % N_MIXERS
        j = layer // N_MIXERS
        if kind == 0:
            if layer == 0:
                hn, x = rmsnorm_concat([x_prompt.reshape(mp, d).astype(F32), x_sample.reshape(ms, d).astype(F32)],
                                       norm_mixer[layer])
            else:
                hn = rmsnorm(x, norm_mixer[layer], F32)
            ops = _s5_operators(s5_lambda_re[j], s5_lambda_im[j], s5_log_step[j],
                                s5_b_re[j], s5_b_im[j], s5_c_re[j], s5_c_im[j])
            z, re_p, im_p, re_s, im_s = s5_mixer_core(
                hn, ops, s5_d[j], state_s5_re[j], state_s5_im[j], bp=bp, sp=sp, bs=bs, ss=ss)
            new_re_p.append(re_p)
            new_im_p.append(im_p)
            new_re_s.append(re_s)
            new_im_s.append(im_s)
            x, xg, ssq = matmul(z, w_glu, j, n_out=d, w_col_offsets=(0, d), epi="glu_res",
                                out_dtype=F32, res=x, bm=bm, bn=bn, next_gain=norm_ffn[layer], name="s5_glu")
        elif kind == 1:
            qkv = matmul(xg, w_qkv, j, n_out=3 * d, w_col_offsets=(0,), epi="none",
                         out_dtype=F32, bm=bm, bn=bn_proj, ssq=ssq, name="attn_qkv")
            bias_vec = _attn_bias_vectors(attn_rel_bias[j])
            o, nk_p, nv_p = attn_prompt(qkv, bias_vec, bp=bp, sp=sp)
            o = attn_sample(qkv, cache_attn_k[j], cache_attn_v[j], bias_vec, o, row0=mp, bs=bs, ss=ss)
            qkv_s = qkv[mp:].reshape(bs, ss, 3 * d)
            new_k_p.append(nk_p.reshape(bp, -1, nh, HEAD_DIM))
            new_v_p.append(nv_p.reshape(bp, -1, nh, HEAD_DIM))
            k_all = jnp.concatenate([cache_attn_k[j], qkv_s[:, :, d:2 * d].reshape(bs, ss, nh, HEAD_DIM)], axis=1)
            v_all = jnp.concatenate([cache_attn_v[j], qkv_s[:, :, 2 * d:].reshape(bs, ss, nh, HEAD_DIM)], axis=1)
            new_k_s.append(k_all[:, ss:])
            new_v_s.append(v_all[:, ss:])
            x, xg, ssq = matmul(o, w_ao, j, n_out=d, w_col_offsets=(0,), epi="res",
                                out_dtype=F32, res=x, bm=bm, bn=bn, next_gain=norm_ffn[layer], name="attn_wo")
        else:
            proj = matmul(xg, w_hin, j, n_out=4 * d, w_col_offsets=(0,), epi="none",
                          out_dtype=F32, bm=bm, bn=bn_proj, ssq=ssq, name="hgrn_win")
            lb = lbs[layer]
            log_lb = jnp.log(lb)
            log_1m_lb = jnp.log1p(-lb)
            o, h_p = hgrn_core(proj, log_lb, log_1m_lb, hgrn_norm[j], None, None, row0=0, nb=bp, seq=sp)
            o, h_s = hgrn_core(proj, log_lb, log_1m_lb, hgrn_norm[j], state_hgrn[j], o, row0=mp, nb=bs, seq=ss)
            new_h_p.append(h_p)
            new_h_s.append(h_s)
            x, xg, ssq = matmul(o, w_ho, j, n_out=d, w_col_offsets=(0,), epi="res",
                                out_dtype=F32, res=x, bm=bm, bn=bn, next_gain=norm_ffn[layer], name="hgrn_wo")
        act = matmul(xg, w_gate_up, layer, n_out=d_ff, w_col_offsets=(0, d_ff),
                     epi="swiglu", out_dtype=BF16, bm=bm, bn=bn_ff, ssq=ssq, name="ffn_gate_up")
        fuse_next = layer + 1 < depth and (layer + 1) % N_MIXERS != 0
        if fuse_next:
            x, xg, ssq = matmul(act, w_down, layer, n_out=d, w_col_offsets=(0,), epi="res",
                                out_dtype=F32, res=x, bm=bm_down, bn=bn_down,
                                next_gain=norm_mixer[layer + 1], name="ffn_down")
        else:
            x = matmul(act, w_down, layer, n_out=d, w_col_offsets=(0,), epi="res",
                       out_dtype=F32, res=x, bm=bm_down, bn=bn_down, w_resident=True, name="ffn_down")

    y_p = rmsnorm(x, norm_final, F32, 0, mp).reshape(bp, sp, d)
    y_s = rmsnorm(x, norm_final, F32, mp, ms).reshape(bs, ss, d)
    return (y_p, y_s,
            jnp.stack(new_re_p), jnp.stack(new_im_p),
            jnp.stack(new_k_p), jnp.stack(new_v_p), jnp.stack(new_h_p),
            jnp.stack(new_re_s), jnp.stack(new_im_s),
            jnp.stack(new_k_s), jnp.stack(new_v_s), jnp.stack(new_h_s))
```

```python
import functools
import math

import numpy as np
import jax
import jax.numpy as jnp
from jax import lax
from jax.experimental import pallas as pl
from jax.experimental.pallas import tpu as pltpu

F32 = jnp.float32
BF16 = jnp.bfloat16

PAST_LEN = 2048
CHUNK = 64
BAND_CHUNKS = 8
WINDOW = BAND_CHUNKS * CHUNK
REL_CLIP = 128
N_MIXERS = 3
S5_GROUP = 16
S5_BLOCK = 16
HEAD_DIM = 128
RMS_EPS = 1e-6
NEG_INF = -1e30

V7X_LANES = 128
V7X_SUBLANES = 8
V7X_BF16_ROWS = 16
V7X_MXU = 256
V7X_VMEM_LIMIT = 56 * 1024 * 1024


def _pick(n, candidates):
    for c in candidates:
        if n % c == 0:
            return c
    return n


def _cparams(n_axes):
    return pltpu.CompilerParams(
        dimension_semantics=("parallel",) * n_axes,
        vmem_limit_bytes=V7X_VMEM_LIMIT)


def _sigmoid(x):
    return 1.0 / (1.0 + jnp.exp(-x))


def _gelu_tanh(x):
    c = math.sqrt(2.0 / math.pi)
    return 0.5 * x * (1.0 + jnp.tanh(c * (x + 0.044715 * (x * x * x))))


def _rmsnorm_kernel(x_ref, g_ref, o_ref):
    x = x_ref[...]
    ms = jnp.mean(x * x, axis=-1, keepdims=True)
    o_ref[...] = (x * lax.rsqrt(ms + RMS_EPS) * g_ref[...]).astype(o_ref.dtype)


def rmsnorm(x, gain, out_dtype, row0=0, rows=None):
    m, d = x.shape
    rows = m if rows is None else rows
    bm = _pick(math.gcd(rows, row0) if row0 else rows, (256, 128, 64, 32, 16, 8))
    blk0 = row0 // bm
    return pl.pallas_call(
        _rmsnorm_kernel,
        out_shape=jax.ShapeDtypeStruct((rows, d), out_dtype),
        grid=(rows // bm,),
        in_specs=[pl.BlockSpec((bm, d), lambda i: (i + blk0, 0)),
                  pl.BlockSpec((1, d), lambda i: (0, 0))],
        out_specs=pl.BlockSpec((bm, d), lambda i: (i, 0)),
        compiler_params=_cparams(1),
        name="rmsnorm",
    )(x, gain.reshape(1, d).astype(F32))


def _rmsnorm_copy_kernel(*refs):
    x_ref, g_ref = refs[:2]
    o_ref, c_ref = refs[-2:]
    x = x_ref[...]
    ms = jnp.mean(x * x, axis=-1, keepdims=True)
    o_ref[...] = x * lax.rsqrt(ms + RMS_EPS) * g_ref[...]
    c_ref[...] = x


def rmsnorm_concat(parts, gain):
    d = parts[0].shape[1]
    m = sum(p.shape[0] for p in parts)
    g2 = gain.reshape(1, d).astype(F32)
    outs = None
    row0 = 0
    for p in parts:
        rows = p.shape[0]
        bm = _pick(math.gcd(rows, row0) if row0 else rows, (256, 128, 64, 32, 16, 8))
        blk0 = row0 // bm
        row = pl.BlockSpec((bm, d), lambda i: (i, 0))
        orow = pl.BlockSpec((bm, d), functools.partial(lambda blk0, i: (i + blk0, 0), blk0))
        in_specs = [row, pl.BlockSpec((1, d), lambda i: (0, 0))]
        args = [p, g2]
        aliases = {}
        if outs is not None:
            in_specs += [pl.BlockSpec(memory_space=pl.ANY)] * 2
            args += list(outs)
            aliases = {2: 0, 3: 1}
        outs = pl.pallas_call(
            _rmsnorm_copy_kernel,
            out_shape=(jax.ShapeDtypeStruct((m, d), F32), jax.ShapeDtypeStruct((m, d), F32)),
            grid=(rows // bm,),
            in_specs=in_specs,
            out_specs=(orow, orow),
            input_output_aliases=aliases,
            compiler_params=_cparams(1),
            name="rmsnorm_concat",
        )(*args)
        row0 += rows
    return outs


MM_SUB_ROWS = 512
def _mm_kernel(*refs, n_w, epi, has_res, pre_norm, post_norm, k_dim, sub):
    refs = list(refs)
    a_ref = refs.pop(0)
    w_refs = [refs.pop(0) for _ in range(n_w)]
    res_ref = refs.pop(0) if has_res else None
    ssq_in_ref = refs.pop(0) if pre_norm else None
    gain_ref = refs.pop(0) if post_norm else None
    o_ref = refs.pop(0)
    bm, bn = o_ref.shape
    ws = [w if w.dtype == BF16 else w[...].astype(BF16) for w in w_refs]
    if post_norm:
        xg_ref, ssq_ref = refs

        @pl.when(pl.program_id(1) == 0)
        def _():
            ssq_ref[...] = jnp.zeros_like(ssq_ref)

    for r0 in range(0, bm, sub):
        rows = slice(r0, r0 + sub)
        a = a_ref[rows, :]
        vals = [jnp.dot(a, w[...], preferred_element_type=F32) for w in ws]
        if pre_norm:
            rstd = lax.rsqrt(ssq_in_ref[rows, :] * (1.0 / k_dim) + RMS_EPS)
            rstd = jnp.concatenate([rstd] * (bn // V7X_LANES), axis=1)
            vals = [v * rstd for v in vals]
        if epi == "none":
            o = vals[0]
        elif epi == "swiglu":
            o = vals[0] * _sigmoid(vals[0]) * vals[1]
        elif epi == "res":
            o = res_ref[rows, :] + vals[0]
        elif epi == "glu_res":
            o = res_ref[rows, :] + vals[0] * _sigmoid(vals[1])
        else:
            raise ValueError(epi)
        o_ref[rows, :] = o.astype(o_ref.dtype)
        if post_norm:
            xg_ref[rows, :] = (o * gain_ref[...]).astype(xg_ref.dtype)
            ssq_ref[rows, :] += jnp.broadcast_to(jnp.sum(o * o, axis=1, keepdims=True), (sub, V7X_LANES))


def matmul(a, w, layer, *, n_out, w_col_offsets, epi, out_dtype, res=None, bm, bn, w_resident=False,
           ssq=None, next_gain=None, name):
    m, k = a.shape
    n_w = len(w_col_offsets)
    post_norm = next_gain is not None
    assert m % bm == 0 and n_out % bn == 0 and bn % V7X_LANES == 0
    assert all(off % bn == 0 for off in w_col_offsets)
    assert not (w_resident and post_norm)
    if w_resident:
        grid = (n_out // bn, m // bm)
        ij = lambda g0, g1: (g1, g0)
    else:
        grid = (m // bm, n_out // bn)
        ij = lambda g0, g1: (g0, g1)

    def a_map(g0, g1):
        return (ij(g0, g1)[0], 0)

    def w_map(off_blocks):
        return lambda g0, g1: (layer, 0, ij(g0, g1)[1] + off_blocks)

    def o_map(g0, g1):
        return ij(g0, g1)

    in_specs = [pl.BlockSpec((bm, k), a_map)]
    args = [a]
    for off in w_col_offsets:
        in_specs.append(pl.BlockSpec((None, k, bn), w_map(off // bn)))
        args.append(w)
    if res is not None:
        in_specs.append(pl.BlockSpec((bm, bn), o_map))
        args.append(res)
    if ssq is not None:
        in_specs.append(pl.BlockSpec((bm, V7X_LANES), a_map))
        args.append(ssq)
    out_shape = [jax.ShapeDtypeStruct((m, n_out), out_dtype)]
    out_specs = [pl.BlockSpec((bm, bn), o_map)]
    if post_norm:
        in_specs.append(pl.BlockSpec((1, bn), lambda g0, g1: (0, ij(g0, g1)[1])))
        args.append(next_gain.reshape(1, n_out).astype(F32))
        out_shape += [jax.ShapeDtypeStruct((m, n_out), BF16), jax.ShapeDtypeStruct((m, V7X_LANES), F32)]
        out_specs += [pl.BlockSpec((bm, bn), o_map), pl.BlockSpec((bm, V7X_LANES), a_map)]
    sem = ("parallel", "arbitrary") if post_norm else ("parallel", "parallel")
    sub = bm
    if (epi == "swiglu" or post_norm) and bm % MM_SUB_ROWS == 0:
        sub = MM_SUB_ROWS
    out = pl.pallas_call(
        functools.partial(_mm_kernel, n_w=n_w, epi=epi, has_res=res is not None,
                          pre_norm=ssq is not None, post_norm=post_norm, k_dim=k, sub=sub),
        out_shape=tuple(out_shape),
        grid=grid,
        in_specs=in_specs,
        out_specs=tuple(out_specs),
        compiler_params=pltpu.CompilerParams(dimension_semantics=sem, vmem_limit_bytes=V7X_VMEM_LIMIT),
        name=name,
    )(*args)
    return out if post_norm else out[0]


S5_GPT = V7X_LANES // S5_GROUP


def _s5_operators(lam_re, lam_im, log_step, b_re, b_im, c_re, c_im):
    g, p = lam_re.shape
    j = b_re.shape[-1]
    t = S5_BLOCK
    nl = g // S5_GPT
    hp = lax.Precision.HIGHEST
    step = jnp.exp(log_step.astype(F32))[:, None]
    lam_re = lam_re.astype(F32)
    lam_im = lam_im.astype(F32)
    lr = lam_re * step
    li = lam_im * step
    tau = jnp.arange(t + 1, dtype=F32)[:, None, None]
    mag = jnp.exp(tau * lr[None])
    e_re = mag * jnp.cos(tau * li[None])
    e_im = mag * jnp.sin(tau * li[None])
    lbar_re, lbar_im = e_re[1], e_im[1]
    denom = lam_re * lam_re + lam_im * lam_im
    num_re = lbar_re - 1.0
    coef_re = (num_re * lam_re + lbar_im * lam_im) / denom
    coef_im = (lbar_im * lam_re - num_re * lam_im) / denom
    bt_re = jnp.swapaxes(b_re.astype(F32), 1, 2)
    bt_im = jnp.swapaxes(b_im.astype(F32), 1, 2)
    bb_re = coef_re[:, None, :] * bt_re - coef_im[:, None, :] * bt_im
    bb_im = coef_re[:, None, :] * bt_im + coef_im[:, None, :] * bt_re
    c_re = c_re.astype(F32)
    c_im = c_im.astype(F32)
    eye = jnp.eye(S5_GPT, dtype=F32)

    a_re = e_re[:, :, None, :] * bb_re[None] - e_im[:, :, None, :] * bb_im[None]
    a_im = e_re[:, :, None, :] * bb_im[None] + e_im[:, :, None, :] * bb_re[None]

    al_re = a_re[:t].reshape(t, nl, S5_GPT * j, p)
    al_im = a_im[:t].reshape(t, nl, S5_GPT * j, p)
    cl_re = c_re.reshape(nl, S5_GPT * j, p)
    cl_im = c_im.reshape(nl, S5_GPT * j, p)
    lagfull = (jnp.einsum("tlip,lop->ltio", al_re, cl_re, precision=hp)
               - jnp.einsum("tlip,lop->ltio", al_im, cl_im, precision=hp))
    blockmask = jnp.kron(eye, jnp.ones((j, j), F32))
    bd = (lagfull * blockmask).astype(BF16)

    parity = (jnp.arange(g) % 2)[None, :, None, None]

    def position(src_re, src_im):
        parts = [jnp.where(parity == h, src, 0.0) for src in (src_re, src_im) for h in (0, 1)]
        out = jnp.concatenate(parts, axis=-1).astype(BF16)
        out = out.reshape(t, nl, S5_GPT, j, 4 * p).transpose(1, 0, 2, 3, 4)
        return out.reshape(nl, t * S5_GPT * j, 4 * p)

    w_st = position(a_re[:t][::-1], a_im[:t][::-1])
    ce_re = c_re[None] * e_re[1:, :, None, :] - c_im[None] * e_im[1:, :, None, :]
    ce_im = c_re[None] * e_im[1:, :, None, :] + c_im[None] * e_re[1:, :, None, :]
    w_ro = position(ce_re, -ce_im)

    a_pow = jnp.stack([e_re[t].reshape(nl, S5_GPT * p), e_im[t].reshape(nl, S5_GPT * p)], axis=1)
    return bd, w_st, w_ro, a_pow


def _s5_kernel(x_ref, bd_ref, wstc_ref, wroc_ref, apow_ref, h0_ref, d_ref,
               z_ref, xfp_ref, xfs_ref,
               tbig_ref, wst_ref, wro_ref, ublk_ref, sst_ref, ybuf_ref, *,
               bp, sp, bs, ss, pitch):
    t = S5_BLOCK
    ln = V7X_LANES
    ncp = sp // t
    ncs = ss // t
    mp = bp * sp
    prow = bp * pitch
    rows = ublk_ref.shape[0]
    nslab = sst_ref.shape[0]
    half = nslab // 2

    @pl.when(pl.program_id(0) == 0)
    def _():
        wst_ref[...] = jnp.zeros_like(wst_ref)
        wro_ref[...] = jnp.zeros_like(wro_ref)

    pair_rows = 2 * S5_GROUP
    for src_ref, dst_ref in ((wstc_ref, wst_ref), (wroc_ref, wro_ref)):
        for s in range(t):
            for k in range(half):
                r0 = s * ln + k * pair_rows
                dst_ref[r0:r0 + pair_rows, k * ln:(k + 1) * ln] = src_ref[0, r0:r0 + pair_rows, 0:ln]
                dst_ref[r0:r0 + pair_rows, (half + k) * ln:(half + k + 1) * ln] = (
                    src_ref[0, r0:r0 + pair_rows, ln:2 * ln])

    for s in range(t):
        for tt in range(s, t):
            tbig_ref[s * ln:(s + 1) * ln, tt * ln:(tt + 1) * ln] = bd_ref[0, tt - s]
    for tt in range(0, t, 2):
        tbig_ref[(tt + 1) * ln:(tt + 2) * ln, tt * ln:(tt + 1) * ln] = jnp.zeros((ln, ln), BF16)

    def gather(b, _):
        r0 = pl.multiple_of(b * pitch, V7X_BF16_ROWS)
        for s in range(t):
            ublk_ref[pl.ds(r0, ncp), s * ln:(s + 1) * ln] = (
                x_ref[pl.ds(b * sp + s, ncp, stride=t), :].astype(BF16))
        ublk_ref[pl.ds(r0 + ncp, pitch - ncp), :] = jnp.zeros((pitch - ncp, t * ln), BF16)
        return 0

    lax.fori_loop(0, bp, gather, 0)
    for s in range(t):
        ublk_ref[prow:prow + bs * ncs, s * ln:(s + 1) * ln] = (
            x_ref[pl.ds(mp + s, bs * ncs, stride=t), :].astype(BF16))
    if rows > prow + bs * ncs:
        ublk_ref[prow + bs * ncs:rows, :] = jnp.zeros((rows - prow - bs * ncs, t * ln), BF16)

    rc = rows // 2
    for r0 in (0, rc):
        inj = jnp.dot(ublk_ref[r0:r0 + rc, :], wst_ref[...], preferred_element_type=F32)
        for k in range(nslab):
            sst_ref[k, r0:r0 + rc, :] = inj[:, k * ln:(k + 1) * ln]

    ar = [apow_ref[0, 0:1, k * ln:(k + 1) * ln] for k in range(half)]
    ai = [apow_ref[0, 1:2, k * ln:(k + 1) * ln] for k in range(half)]

    def advance(xr, xi, loc):
        nr, ni = [], []
        for k in range(half):
            s_r = sst_ref[k, loc, :]
            s_i = sst_ref[half + k, loc, :]
            sst_ref[k, loc, :] = xr[k]
            sst_ref[half + k, loc, :] = xi[k]
            nr.append(ar[k] * xr[k] - ai[k] * xi[k] + s_r)
            ni.append(ar[k] * xi[k] + ai[k] * xr[k] + s_i)
        return tuple(nr), tuple(ni)

    for b0 in range(0, bp, V7X_SUBLANES):
        def pstep(c, carry, b0=b0):
            return advance(carry[0], carry[1], pl.ds(b0 * pitch + c, V7X_SUBLANES, stride=pitch))

        zero = tuple(jnp.zeros((V7X_SUBLANES, ln), F32) for _ in range(half))
        xr, xi = lax.fori_loop(0, ncp, pstep, (zero, zero), unroll=2)
        for k in range(half):
            xfp_ref[0, b0:b0 + V7X_SUBLANES, k * ln:(k + 1) * ln] = xr[k]
            xfp_ref[0, b0:b0 + V7X_SUBLANES, (half + k) * ln:(half + k + 1) * ln] = xi[k]
    for b0 in range(0, bs, V7X_SUBLANES):
        xr = tuple(h0_ref[0, b0:b0 + V7X_SUBLANES, k * ln:(k + 1) * ln] for k in range(half))
        xi = tuple(h0_ref[0, b0:b0 + V7X_SUBLANES, (half + k) * ln:(half + k + 1) * ln] for k in range(half))
        for c in range(ncs):
            xr, xi = advance(xr, xi, pl.ds(prow + b0 * ncs + c, V7X_SUBLANES, stride=ncs))
        for k in range(half):
            xfs_ref[0, b0:b0 + V7X_SUBLANES, k * ln:(k + 1) * ln] = xr[k]
            xfs_ref[0, b0:b0 + V7X_SUBLANES, (half + k) * ln:(half + k + 1) * ln] = xi[k]

    d_skip = d_ref[...]
    npair = bp // 2
    for pr in range(npair):
        last = pr == npair - 1
        r0 = pr * 2 * pitch
        nr = (rows - r0) if last else 2 * pitch
        xs = jnp.concatenate([sst_ref[k, r0:r0 + nr, :] for k in range(nslab)], axis=1).astype(BF16)
        for tp in range(t * ln // V7X_MXU):
            c0 = tp * V7X_MXU
            kk = c0 + V7X_MXU
            yc = (jnp.dot(ublk_ref[r0:r0 + nr, 0:kk], tbig_ref[0:kk, c0:c0 + V7X_MXU],
                          preferred_element_type=F32)
                  + lax.dot_general(xs, wro_ref[c0:c0 + V7X_MXU, :], (((1,), (1,)), ((), ())),
                                    preferred_element_type=F32))
            for sub in range(V7X_MXU // ln):
                tok = tp * (V7X_MXU // ln) + sub
                for q in range(2):
                    ybuf_ref[pl.ds(q * sp + tok, ncp, stride=t), :] = (
                        yc[q * pitch:q * pitch + ncp, sub * ln:(sub + 1) * ln])
                if last:
                    ybuf_ref[pl.ds(2 * sp + tok, bs * ncs, stride=t), :] = (
                        yc[2 * pitch:2 * pitch + bs * ncs, sub * ln:(sub + 1) * ln])
        tok0 = pr * 2 * sp
        ntok = 2 * sp + (bs * ss if last else 0)
        z_ref[tok0:tok0 + ntok, :] = _gelu_tanh(
            ybuf_ref[0:ntok, :] + d_skip * x_ref[tok0:tok0 + ntok, :]).astype(z_ref.dtype)


def s5_mixer_core(hn, ops, d_skip, h0_re, h0_im, *, bp, sp, bs, ss):
    bd, w_st, w_ro, a_pow = ops
    m, d = hn.shape
    ln = V7X_LANES
    nl = d // ln
    t = S5_BLOCK
    p8 = a_pow.shape[-1]
    p = p8 // S5_GPT
    assert bp % V7X_SUBLANES == 0 and bs % V7X_SUBLANES == 0 and bp % 2 == 0
    assert sp % t == 0 and ss % t == 0 and m == bp * sp + bs * ss and 2 * p == ln
    ncp, ncs = sp // t, ss // t
    pitch = ncp + V7X_BF16_ROWS - ncp % V7X_BF16_ROWS
    rows = bp * pitch + bs * ncs
    rows += (-rows) % (2 * V7X_BF16_ROWS)
    def slabbed(a):
        return a.astype(F32).reshape(bs, nl, p8).transpose(1, 0, 2)
    h0 = jnp.concatenate([slabbed(h0_re), slabbed(h0_im)], axis=-1)
    nslab = 2 * p8 // ln
    z, xfp, xfs = pl.pallas_call(
        functools.partial(_s5_kernel, bp=bp, sp=sp, bs=bs, ss=ss, pitch=pitch),
        out_shape=(jax.ShapeDtypeStruct((m, d), BF16),
                   jax.ShapeDtypeStruct((nl, bp, 2 * p8), F32),
                   jax.ShapeDtypeStruct((nl, bs, 2 * p8), F32)),
        grid=(nl,),
        in_specs=[pl.BlockSpec((m, ln), lambda l: (0, l), pipeline_mode=pl.Buffered(1)),
                  pl.BlockSpec((1, t, ln, ln), lambda l: (l, 0, 0, 0)),
                  pl.BlockSpec((1, t * ln, 4 * p), lambda l: (l, 0, 0)),
                  pl.BlockSpec((1, t * ln, 4 * p), lambda l: (l, 0, 0)),
                  pl.BlockSpec((1, 2, p8), lambda l: (l, 0, 0)),
                  pl.BlockSpec((1, bs, 2 * p8), lambda l: (l, 0, 0)),
                  pl.BlockSpec((1, ln), lambda l: (0, l))],
        out_specs=(pl.BlockSpec((m, ln), lambda l: (0, l)),
                   pl.BlockSpec((1, bp, 2 * p8), lambda l: (l, 0, 0)),
                   pl.BlockSpec((1, bs, 2 * p8), lambda l: (l, 0, 0))),
        scratch_shapes=[pltpu.VMEM((t * ln, t * ln), BF16),
                        pltpu.VMEM((t * ln, 2 * p8), BF16),
                        pltpu.VMEM((t * ln, 2 * p8), BF16),
                        pltpu.VMEM((rows, t * ln), BF16),
                        pltpu.VMEM((nslab, rows, ln), F32),
                        pltpu.VMEM((2 * sp + bs * ss, ln), F32)],
        compiler_params=pltpu.CompilerParams(dimension_semantics=("arbitrary",),
                                             vmem_limit_bytes=V7X_VMEM_LIMIT),
        name="s5_core",
    )(hn, bd, w_st, w_ro, a_pow, h0, d_skip.reshape(1, d).astype(F32))

    def unslab(a, nb):
        re = a[..., :p8].transpose(1, 0, 2).reshape(nb, nl * S5_GPT, p)
        im = a[..., p8:].transpose(1, 0, 2).reshape(nb, nl * S5_GPT, p)
        return re, im

    return (z,) + unslab(xfp, bp) + unslab(xfs, bs)


QBLK = 2 * CHUNK
KWIN = WINDOW + QBLK
BIAS_W = KWIN + WINDOW
BIAS_LEN = BIAS_W + QBLK
ATTN_GROUP = 2
ATTN_HEAD_GROUP = 4


def _attn_bias_vectors(rel_bias):
    n = np.arange(BIAS_LEN)
    delta = np.where(n < BIAS_LEN - (QBLK - 1), n, n - BIAS_LEN)
    idx = np.clip(WINDOW - delta, -REL_CLIP, REL_CLIP) + REL_CLIP
    return rel_bias.astype(F32)[:, idx][:, None, :]


def _bias_tile(e_row, nq):
    return pltpu.roll(jnp.broadcast_to(e_row, (nq, BIAS_LEN)), 0, 1, stride=1, stride_axis=0)


def _attn_prompt_kernel(q_ref, k_ref, v_ref, e_ref, o_ref, nk_ref, nv_ref, bias_ref, kbf_ref, vbf_ref,
                        *, seq, keep):
    log2e = math.log2(math.e)
    scale = HEAD_DIM ** -0.5 * log2e
    qc = lax.broadcasted_iota(jnp.int32, (QBLK, BIAS_LEN), 0) // CHUNK
    kc = lax.broadcasted_iota(jnp.int32, (QBLK, BIAS_LEN), 1) // CHUNK
    tile = _bias_tile(e_ref[0], QBLK) * log2e
    bias_ref[...] = jnp.where(kc >= qc, jnp.where(kc <= qc + BAND_CHUNKS, tile, NEG_INF), NEG_INF)
    kbf_ref[...] = k_ref[...].astype(BF16)
    vbf_ref[...] = v_ref[...].astype(BF16)
    nblk = seq // QBLK

    def group_blocks(g0):
        blocks = []
        for i in range(g0, min(g0 + ATTN_GROUP, nblk)):
            q0 = i * QBLK
            start = max(q0 - WINDOW, 0)
            blocks.append((q0, start, min(KWIN, seq - start), start - (q0 - WINDOW)))
        return blocks

    def group_scores(blocks):
        ss = []
        for q0, start, kw, shift in blocks:
            q = q_ref[q0:q0 + QBLK, :].astype(BF16)
            k = kbf_ref[start:start + kw, :]
            s = lax.dot_general(q, k, (((1,), (1,)), ((), ())), preferred_element_type=F32)
            ss.append(s * scale + bias_ref[:, shift:shift + kw])
        return ss

    starts = list(range(0, nblk, ATTN_GROUP))
    ss_next = group_scores(group_blocks(starts[0]))
    for gi, g0 in enumerate(starts):
        blocks = group_blocks(g0)
        ss = ss_next
        if gi + 1 < len(starts):
            ss_next = group_scores(group_blocks(starts[gi + 1]))
        ms = [jnp.max(s, axis=-1, keepdims=True) for s in ss]
        es = [jnp.exp2(s - m) for s, m in zip(ss, ms)]
        ls = [jnp.sum(e, axis=-1, keepdims=True) for e in es]
        for (q0, start, kw, shift), e, l in zip(blocks, es, ls):
            v = vbf_ref[start:start + kw, :]
            o = jnp.dot(e.astype(BF16), v, preferred_element_type=F32) * (1.0 / l)
            o_ref[q0:q0 + QBLK, :] = o.astype(o_ref.dtype)
    nk_ref[0] = k_ref[seq - keep:seq, :]
    nv_ref[0] = v_ref[seq - keep:seq, :]


def attn_prompt(qkv, bias_vec, *, bp, sp):
    m, d3 = qkv.shape
    d = d3 // 3
    nh = d // HEAD_DIM
    keep = min(WINDOW, sp)
    assert sp % QBLK == 0
    blk = lambda off: pl.BlockSpec((sp, HEAD_DIM), lambda b, h: (b, h + off))
    tail = pl.BlockSpec((1, keep, HEAD_DIM), lambda b, h: (b, 0, h))
    return pl.pallas_call(
        functools.partial(_attn_prompt_kernel, seq=sp, keep=keep),
        out_shape=(jax.ShapeDtypeStruct((m, d), BF16),
                   jax.ShapeDtypeStruct((bp, keep, d), F32),
                   jax.ShapeDtypeStruct((bp, keep, d), F32)),
        grid=(bp, nh),
        in_specs=[blk(0), blk(nh), blk(2 * nh),
                  pl.BlockSpec((1, 1, BIAS_LEN), lambda b, h: (h, 0, 0))],
        out_specs=(pl.BlockSpec((sp, HEAD_DIM), lambda b, h: (b, h)), tail, tail),
        scratch_shapes=[pltpu.VMEM((QBLK, BIAS_LEN), F32),
                        pltpu.VMEM((sp, HEAD_DIM), BF16),
                        pltpu.VMEM((sp, HEAD_DIM), BF16)],
        compiler_params=_cparams(2),
        name="attn_prompt",
    )(qkv, qkv, qkv, bias_vec)


def _attn_sample_kernel(qkv_ref, ck_ref, cv_ref, e_ref, prev_ref, o_ref, *, seq, ncache, nh, mask):
    del prev_ref
    scale = HEAD_DIM ** -0.5
    d = nh * HEAD_DIM
    dn = (((1,), (1,)), ((), ()))
    off = WINDOW - ncache
    for h0 in range(0, nh, ATTN_HEAD_GROUP):
        heads = range(h0, min(h0 + ATTN_HEAD_GROUP, nh))
        s1s, s2s = [], []
        for h in heads:
            c0 = h * HEAD_DIM
            q = qkv_ref[:, c0:c0 + HEAD_DIM].astype(BF16)
            k = qkv_ref[:, d + c0:d + c0 + HEAD_DIM].astype(BF16)
            ck = ck_ref[0, pl.ds(h, ncache, stride=nh), :].astype(BF16)
            bias = _bias_tile(e_ref[h], seq)
            s1 = lax.dot_general(q, ck, dn, preferred_element_type=F32) * scale + bias[:, off:off + ncache]
            s2 = lax.dot_general(q, k, dn, preferred_element_type=F32) * scale + bias[:, WINDOW:WINDOW + seq]
            if mask is not None:
                s1 = jnp.where(jnp.asarray(mask[:, :ncache]), s1, NEG_INF)
                s2 = jnp.where(jnp.asarray(mask[:, ncache:]), s2, NEG_INF)
            s1s.append(s1)
            s2s.append(s2)
        mxs = [jnp.maximum(jnp.max(s1, axis=-1, keepdims=True), jnp.max(s2, axis=-1, keepdims=True))
               for s1, s2 in zip(s1s, s2s)]
        e1s = [jnp.exp(s1 - mx) for s1, mx in zip(s1s, mxs)]
        e2s = [jnp.exp(s2 - mx) for s2, mx in zip(s2s, mxs)]
        ls = [jnp.sum(e1, axis=-1, keepdims=True) + jnp.sum(e2, axis=-1, keepdims=True)
              for e1, e2 in zip(e1s, e2s)]
        for h, e1, e2, l in zip(heads, e1s, e2s, ls):
            c0 = h * HEAD_DIM
            v = qkv_ref[:, 2 * d + c0:2 * d + c0 + HEAD_DIM].astype(BF16)
            cv = cv_ref[0, pl.ds(h, ncache, stride=nh), :].astype(BF16)
            o = (jnp.dot(e1.astype(BF16), cv, preferred_element_type=F32)
                 + jnp.dot(e2.astype(BF16), v, preferred_element_type=F32)) * (1.0 / l)
            o_ref[:, c0:c0 + HEAD_DIM] = o.astype(o_ref.dtype)


def attn_sample(qkv, cache_k, cache_v, bias_vec, o_prev, *, row0, bs, ss):
    m, d3 = qkv.shape
    d = d3 // 3
    nh = d // HEAD_DIM
    ncache = cache_k.shape[1]
    assert ss <= QBLK and ss % V7X_SUBLANES == 0 and ncache <= WINDOW and row0 % ss == 0
    q_pos = PAST_LEN + np.arange(ss)
    k_pos = PAST_LEN - ncache + np.arange(ncache + ss)
    qc, kc = q_pos // CHUNK, k_pos // CHUNK
    allowed = ((k_pos[None, :] >= 0) & (kc[None, :] <= qc[:, None])
               & (kc[None, :] >= qc[:, None] - BAND_CHUNKS))
    mask = None if allowed.all() else allowed
    ck = cache_k.reshape(bs, ncache * nh, HEAD_DIM)
    cv = cache_v.reshape(bs, ncache * nh, HEAD_DIM)
    blk0 = row0 // ss
    cblk = pl.BlockSpec((1, ncache * nh, HEAD_DIM), lambda b: (b, 0, 0))
    return pl.pallas_call(
        functools.partial(_attn_sample_kernel, seq=ss, ncache=ncache, nh=nh, mask=mask),
        out_shape=jax.ShapeDtypeStruct((m, d), BF16),
        grid=(bs,),
        in_specs=[pl.BlockSpec((ss, d3), lambda b: (b + blk0, 0)), cblk, cblk,
                  pl.BlockSpec((nh, 1, BIAS_LEN), lambda b: (0, 0, 0)),
                  pl.BlockSpec(memory_space=pl.ANY)],
        out_specs=pl.BlockSpec((ss, d), lambda b: (b + blk0, 0)),
        input_output_aliases={4: 0},
        compiler_params=_cparams(1),
        name="attn_sample",
    )(qkv, ck, cv, bias_vec, o_prev)


HGRN_HEADS_PER_STEP = 4


def _hgrn_kernel(*refs, seq, csz, has_state, aliased, hp):
    refs = list(refs)
    q_ref, f_ref, v_ref, z_ref, llb_ref, l1m_ref, gain_ref = refs[:7]
    pos = 7
    s0_ref = None
    if has_state:
        s0_ref = refs[pos]
        pos += 1
    if aliased:
        pos += 1
    o_ref, sn_ref, st_ref = refs[pos:pos + 3]
    kdim = HEAD_DIM
    scale = kdim ** -0.5
    dn_t = (((1,), (1,)), ((), ()))
    for hh in range(hp):
        if has_state:
            st_ref[hh] = s0_ref[0, hh].T
        else:
            st_ref[hh] = jnp.zeros((kdim, kdim), F32)
    row = lax.broadcasted_iota(jnp.int32, (csz, kdim), 0)
    ti = lax.broadcasted_iota(jnp.int32, (csz, csz), 0)
    si = lax.broadcasted_iota(jnp.int32, (csz, csz), 1)
    tx = ti ^ si
    lvl = jnp.where(ti > si, 31 - lax.clz(tx), -1)

    tri = jnp.where(lax.broadcasted_iota(jnp.int32, (csz, csz), 0)
                    >= lax.broadcasted_iota(jnp.int32, (csz, csz), 1), 1.0, 0.0).astype(BF16)

    def gates(r0, hh):
        sl = slice(hh * kdim, (hh + 1) * kdim)
        llb = llb_ref[:, sl]
        l1m = l1m_ref[:, sl]
        qz = q_ref[pl.ds(r0, csz), sl]
        fz = f_ref[pl.ds(r0, csz), sl]
        ls = jnp.minimum(fz, 0.0) - jnp.log(1.0 + jnp.exp(jnp.minimum(fz, -fz)))
        cc = l1m + ls
        dd = llb - cc
        g = jnp.maximum(llb, cc) + jnp.log(1.0 + jnp.exp(jnp.minimum(dd, -dd)))
        kk = jnp.exp(cc - fz)
        q = qz * _sigmoid(qz) * scale
        return g, kk, q

    def prefix_sums(gs):
        parts = []
        for g in gs:
            hi = g.astype(BF16)
            r1 = g - hi.astype(F32)
            mid = r1.astype(BF16)
            lo = (r1 - mid.astype(F32)).astype(BF16)
            parts += [hi, mid, lo]
        sums = jnp.dot(tri, jnp.concatenate(parts, axis=1), preferred_element_type=F32)
        out = []
        for i in range(len(gs)):
            c0 = 3 * i * kdim
            out.append(sums[:, c0:c0 + kdim] + sums[:, c0 + kdim:c0 + 2 * kdim]
                       + sums[:, c0 + 2 * kdim:c0 + 3 * kdim])
        return out

    def one_head(r0, hh, b, kk, q):
        sl = slice(hh * kdim, (hh + 1) * kdim)
        v = v_ref[pl.ds(r0, csz), sl]
        gz = z_ref[pl.ds(r0, csz), sl]
        v_bf = v.astype(BF16)
        sc = lax.dot_general(q.astype(BF16), kk.astype(BF16), dn_t, preferred_element_type=F32)
        scores = jnp.where(tx == 0, sc, 0.0)
        e = b
        w = 1
        while w < min(V7X_SUBLANES, csz):
            bit = (row & w) != 0
            e_prev = pltpu.roll(e, w, 0)
            ex = jnp.exp(jnp.where(bit, b - e_prev, e - b))
            qk = (jnp.where(bit, q, kk) * ex).astype(BF16)
            sc = lax.dot_general(qk, qk, dn_t, preferred_element_type=F32)
            scores = scores + jnp.where(lvl == int(math.log2(w)), sc, 0.0)
            if 2 * w < V7X_SUBLANES:
                e = jnp.where(bit, e, pltpu.roll(e, csz - w, 0))
            w *= 2
        while w < csz:
            n2 = csz // (2 * w)
            b4 = b.reshape(n2, 2, w, kdim)
            bound = b4[:, 0, w - 1:w, :]
            kpart = kk.reshape(n2, 2, w, kdim)[:, 0] * jnp.exp(bound - b4[:, 0])
            qpart = q.reshape(n2, 2, w, kdim)[:, 1] * jnp.exp(b4[:, 1] - bound)
            zero = jnp.zeros((n2, 1, w, kdim), F32)
            kw = jnp.concatenate([kpart[:, None], zero], axis=1).reshape(csz, kdim).astype(BF16)
            qw = jnp.concatenate([zero, qpart[:, None]], axis=1).reshape(csz, kdim).astype(BF16)
            sc = lax.dot_general(qw, kw, dn_t, preferred_element_type=F32)
            if 2 * w < csz:
                scores = scores + jnp.where(tx < 2 * w, sc, 0.0)
            else:
                scores = scores + sc
            w *= 2
        o = jnp.dot(scores.astype(BF16), v_bf, preferred_element_type=F32)
        st = st_ref[hh]
        o = o + lax.dot_general((q * jnp.exp(b)).astype(BF16), st.astype(BF16), dn_t,
                                preferred_element_type=F32)
        b_last = b[csz - 1:csz]
        kd = (kk * jnp.exp(b_last - b)).astype(BF16)
        st_ref[hh] = jnp.exp(b_last) * st + lax.dot_general(
            v_bf, kd, (((0,), (0,)), ((), ())), preferred_element_type=F32)
        o = o * lax.rsqrt(jnp.mean(o * o, axis=-1, keepdims=True) + RMS_EPS)
        o = o * gain_ref[:, sl] * (gz * _sigmoid(gz))
        o_ref[pl.ds(r0, csz), sl] = o.astype(o_ref.dtype)

    def chunk(ci, _):
        r0 = pl.multiple_of(ci * csz, csz)
        gkq = [gates(r0, hh) for hh in range(hp)]
        bs = prefix_sums([g for g, _, _ in gkq])
        for hh in range(hp):
            one_head(r0, hh, bs[hh], gkq[hh][1], gkq[hh][2])
        return 0

    nchunks = seq // csz
    if nchunks == 1:
        chunk(0, 0)
    else:
        lax.fori_loop(0, nchunks, chunk, 0)
    for hh in range(hp):
        sn_ref[0, hh] = st_ref[hh].T


def hgrn_core(proj, log_lb, log_1m_lb, norm_gain, state, o_prev, *, row0, nb, seq):
    m, d4 = proj.shape
    d = d4 // 4
    nh = d // HEAD_DIM
    csz = _pick(seq, (128, 64, 32, 16, 8))
    hp = HGRN_HEADS_PER_STEP * max(1, V7X_LANES // (2 * csz))
    while nh % hp:
        hp //= 2
    ng = nh // hp
    w = hp * HEAD_DIM
    assert row0 % seq == 0
    blk0 = row0 // seq
    has_state = state is not None
    aliased = o_prev is not None
    blk = lambda off: pl.BlockSpec((seq, w), lambda b, h: (b + blk0, h + off * ng))
    vec = pl.BlockSpec((1, w), lambda b, h: (0, h))
    sblk = pl.BlockSpec((1, hp, HEAD_DIM, HEAD_DIM), lambda b, h: (b, h, 0, 0))
    in_specs = [blk(0), blk(1), blk(2), blk(3), vec, vec, vec]
    args = [proj, proj, proj, proj, log_lb.reshape(1, d), log_1m_lb.reshape(1, d),
            norm_gain.reshape(1, d).astype(F32)]
    if has_state:
        in_specs.append(sblk)
        args.append(state.astype(F32))
    aliases = {}
    if aliased:
        aliases = {len(args): 0}
        in_specs.append(pl.BlockSpec(memory_space=pl.ANY))
        args.append(o_prev)
    return pl.pallas_call(
        functools.partial(_hgrn_kernel, seq=seq, csz=csz, has_state=has_state, aliased=aliased, hp=hp),
        out_shape=(jax.ShapeDtypeStruct((m, d), BF16),
                   jax.ShapeDtypeStruct((nb, nh, HEAD_DIM, HEAD_DIM), F32)),
        grid=(nb, ng),
        in_specs=in_specs,
        out_specs=(pl.BlockSpec((seq, w), lambda b, h: (b + blk0, h)), sblk),
        scratch_shapes=[pltpu.VMEM((hp, HEAD_DIM, HEAD_DIM), F32)],
        input_output_aliases=aliases,
        compiler_params=_cparams(2),
        name="hgrn_core",
    )(*args)


def kernel(x_prompt, x_sample, state_s5_re, state_s5_im, cache_attn_k, cache_attn_v, state_hgrn,
           norm_mixer, norm_ffn, norm_final, ffn_w_gate_up, ffn_w_down,
           s5_lambda_re, s5_lambda_im, s5_log_step, s5_b_re, s5_b_im, s5_c_re, s5_c_im, s5_d, s5_w_glu,
           attn_w_qkv, attn_rel_bias, attn_w_o,
           hgrn_w_in, hgrn_lower_bounds, hgrn_norm, hgrn_w_o):
    bp, sp, d = x_prompt.shape
    bs, ss, _ = x_sample.shape
    depth = norm_mixer.shape[0]
    mp = bp * sp
    ms = bs * ss
    m = mp + ms
    d_ff = ffn_w_down.shape[1]
    nh = d // HEAD_DIM

    assert depth > 0
    x = None
    bm = _pick(m, (1536, 768, 512, 256, 128, 64, 32, 16, 8))
    bm_down = _pick(m, (512, 256, 128, 64, 32, 16, 8))
    bn = _pick(d, (256, 128))
    bn_proj = _pick(d, (512, 256, 128))
    bn_down = _pick(d, (512, 256, 128))
    bn_ff = _pick(d_ff, (256, 128))

    w_gate_up = ffn_w_gate_up
    w_qkv = attn_w_qkv
    w_hin = hgrn_w_in
    w_down = ffn_w_down.astype(BF16)
    w_glu = s5_w_glu.astype(BF16)
    w_ao = attn_w_o.astype(BF16)
    w_ho = hgrn_w_o.astype(BF16)

    lbs = jnp.cumsum(jax.nn.softmax(hgrn_lower_bounds.astype(F32), axis=0), axis=0)
    lbs = lbs - lbs[0]

    new_re_p, new_im_p, new_re_s, new_im_s = [], [], [], []
    new_k_p, new_v_p, new_k_s, new_v_s = [], [], [], []
    new_h_p, new_h_s = [], []

    for layer in range(depth):
        kind = layer % N_MIXERS
        j = layer // N_MIXERS
        if kind == 0:
            if layer == 0:
                hn, x = rmsnorm_concat([x_prompt.reshape(mp, d).astype(F32), x_sample.reshape(ms, d).astype(F32)],
                                       norm_mixer[layer])
            else:
                hn = rmsnorm(x, norm_mixer[layer], F32)
            ops = _s5_operators(s5_lambda_re[j], s5_lambda_im[j], s5_log_step[j],
                                s5_b_re[j], s5_b_im[j], s5_c_re[j], s5_c_im[j])
            z, re_p, im_p, re_s, im_s = s5_mixer_core(
                hn, ops, s5_d[j], state_s5_re[j], state_s5_im[j], bp=bp, sp=sp, bs=bs, ss=ss)
            new_re_p.append(re_p)
            new_im_p.append(im_p)
            new_re_s.append(re_s)
            new_im_s.append(im_s)
            x, xg, ssq = matmul(z, w_glu, j, n_out=d, w_col_offsets=(0, d), epi="glu_res",
                                out_dtype=F32, res=x, bm=bm, bn=bn, next_gain=norm_ffn[layer], name="s5_glu")
        elif kind == 1:
            qkv = matmul(xg, w_qkv, j, n_out=3 * d, w_col_offsets=(0,), epi="none",
                         out_dtype=F32, bm=bm, bn=bn_proj, ssq=ssq, name="attn_qkv")
            bias_vec = _attn_bias_vectors(attn_rel_bias[j])
            o, nk_p, nv_p = attn_prompt(qkv, bias_vec, bp=bp, sp=sp)
            o = attn_sample(qkv, cache_attn_k[j], cache_attn_v[j], bias_vec, o, row0=mp, bs=bs, ss=ss)
            qkv_s = qkv[mp:].reshape(bs, ss, 3 * d)
            new_k_p.append(nk_p.reshape(bp, -1, nh, HEAD_DIM))
            new_v_p.append(nv_p.reshape(bp, -1, nh, HEAD_DIM))
            k_all = jnp.concatenate([cache_attn_k[j], qkv_s[:, :, d:2 * d].reshape(bs, ss, nh, HEAD_DIM)], axis=1)
            v_all = jnp.concatenate([cache_attn_v[j], qkv_s[:, :, 2 * d:].reshape(bs, ss, nh, HEAD_DIM)], axis=1)
            new_k_s.append(k_all[:, ss:])
            new_v_s.append(v_all[:, ss:])
            x, xg, ssq = matmul(o, w_ao, j, n_out=d, w_col_offsets=(0,), epi="res",
                                out_dtype=F32, res=x, bm=bm, bn=bn_proj, next_gain=norm_ffn[layer], name="attn_wo")
        else:
            proj = matmul(xg, w_hin, j, n_out=4 * d, w_col_offsets=(0,), epi="none",
                          out_dtype=F32, bm=bm, bn=bn_proj, ssq=ssq, name="hgrn_win")
            lb = lbs[layer]
            log_lb = jnp.log(lb)
            log_1m_lb = jnp.log1p(-lb)
            o, h_p = hgrn_core(proj, log_lb, log_1m_lb, hgrn_norm[j], None, None, row0=0, nb=bp, seq=sp)
            o, h_s = hgrn_core(proj, log_lb, log_1m_lb, hgrn_norm[j], state_hgrn[j], o, row0=mp, nb=bs, seq=ss)
            new_h_p.append(h_p)
            new_h_s.append(h_s)
            x, xg, ssq = matmul(o, w_ho, j, n_out=d, w_col_offsets=(0,), epi="res",
                                out_dtype=F32, res=x, bm=bm, bn=bn_proj, next_gain=norm_ffn[layer], name="hgrn_wo")
        act = matmul(xg, w_gate_up, layer, n_out=d_ff, w_col_offsets=(0, d_ff),
                     epi="swiglu", out_dtype=BF16, bm=bm, bn=bn_ff, ssq=ssq, name="ffn_gate_up")
        fuse_next = layer + 1 < depth and (layer + 1) % N_MIXERS != 0
        if fuse_next:
            x, xg, ssq = matmul(act, w_down, layer, n_out=d, w_col_offsets=(0,), epi="res",
                                out_dtype=F32, res=x, bm=bm_down, bn=bn_down,
                                next_gain=norm_mixer[layer + 1], name="ffn_down")
        else:
            x = matmul(act, w_down, layer, n_out=d, w_col_offsets=(0,), epi="res",
                       out_dtype=F32, res=x, bm=bm_down, bn=bn_down, w_resident=True, name="ffn_down")

    y_p = rmsnorm(x, norm_final, F32, 0, mp).reshape(bp, sp, d)
    y_s = rmsnorm(x, norm_final, F32, mp, ms).reshape(bs, ss, d)
    return (y_p, y_s,
            jnp.stack(new_re_p), jnp.stack(new_im_p),
            jnp.stack(new_k_p), jnp.stack(new_v_p), jnp.stack(new_h_p),
            jnp.stack(new_re_s), jnp.stack(new_im_s),
            jnp.stack(new_k_s), jnp.stack(new_v_s), jnp.stack(new_h_s))
```

```python
import functools
import math

import numpy as np
import jax
import jax.numpy as jnp
from jax import lax
from jax.experimental import pallas as pl
from jax.experimental.pallas import tpu as pltpu

F32 = jnp.float32
BF16 = jnp.bfloat16

PAST_LEN = 2048
CHUNK = 64
BAND_CHUNKS = 8
WINDOW = BAND_CHUNKS * CHUNK
REL_CLIP = 128
N_MIXERS = 3
S5_GROUP = 16
S5_BLOCK = 16
HEAD_DIM = 128
RMS_EPS = 1e-6
NEG_INF = -1e30

V7X_LANES = 128
V7X_SUBLANES = 8
V7X_BF16_ROWS = 16
V7X_MXU = 256
V7X_VMEM_LIMIT = 56 * 1024 * 1024


def _pick(n, candidates):
    for c in candidates:
        if n % c == 0:
            return c
    return n


def _cparams(n_axes):
    return pltpu.CompilerParams(
        dimension_semantics=("parallel",) * n_axes,
        vmem_limit_bytes=V7X_VMEM_LIMIT)


def _sigmoid(x):
    return 1.0 / (1.0 + jnp.exp(-x))


def _gelu_tanh(x):
    c = math.sqrt(2.0 / math.pi)
    return 0.5 * x * (1.0 + jnp.tanh(c * (x + 0.044715 * (x * x * x))))


def _rmsnorm_kernel(x_ref, g_ref, o_ref):
    x = x_ref[...]
    ms = jnp.mean(x * x, axis=-1, keepdims=True)
    o_ref[...] = (x * lax.rsqrt(ms + RMS_EPS) * g_ref[...]).astype(o_ref.dtype)


def rmsnorm(x, gain, out_dtype, row0=0, rows=None):
    m, d = x.shape
    rows = m if rows is None else rows
    bm = _pick(math.gcd(rows, row0) if row0 else rows, (256, 128, 64, 32, 16, 8))
    blk0 = row0 // bm
    return pl.pallas_call(
        _rmsnorm_kernel,
        out_shape=jax.ShapeDtypeStruct((rows, d), out_dtype),
        grid=(rows // bm,),
        in_specs=[pl.BlockSpec((bm, d), lambda i: (i + blk0, 0)),
                  pl.BlockSpec((1, d), lambda i: (0, 0))],
        out_specs=pl.BlockSpec((bm, d), lambda i: (i, 0)),
        compiler_params=_cparams(1),
        name="rmsnorm",
    )(x, gain.reshape(1, d).astype(F32))


def _rmsnorm_copy_kernel(*refs):
    x_ref, g_ref = refs[:2]
    o_ref, c_ref = refs[-2:]
    x = x_ref[...]
    ms = jnp.mean(x * x, axis=-1, keepdims=True)
    o_ref[...] = x * lax.rsqrt(ms + RMS_EPS) * g_ref[...]
    c_ref[...] = x


def rmsnorm_concat(parts, gain):
    d = parts[0].shape[1]
    m = sum(p.shape[0] for p in parts)
    g2 = gain.reshape(1, d).astype(F32)
    outs = None
    row0 = 0
    for p in parts:
        rows = p.shape[0]
        bm = _pick(math.gcd(rows, row0) if row0 else rows, (256, 128, 64, 32, 16, 8))
        blk0 = row0 // bm
        row = pl.BlockSpec((bm, d), lambda i: (i, 0))
        orow = pl.BlockSpec((bm, d), functools.partial(lambda blk0, i: (i + blk0, 0), blk0))
        in_specs = [row, pl.BlockSpec((1, d), lambda i: (0, 0))]
        args = [p, g2]
        aliases = {}
        if outs is not None:
            in_specs += [pl.BlockSpec(memory_space=pl.ANY)] * 2
            args += list(outs)
            aliases = {2: 0, 3: 1}
        outs = pl.pallas_call(
            _rmsnorm_copy_kernel,
            out_shape=(jax.ShapeDtypeStruct((m, d), F32), jax.ShapeDtypeStruct((m, d), F32)),
            grid=(rows // bm,),
            in_specs=in_specs,
            out_specs=(orow, orow),
            input_output_aliases=aliases,
            compiler_params=_cparams(1),
            name="rmsnorm_concat",
        )(*args)
        row0 += rows
    return outs


MM_SUB_ROWS = 512
def _mm_kernel(*refs, n_w, epi, has_res, pre_norm, post_norm, ssq_partial, k_dim, sub):
    refs = list(refs)
    a_ref = refs.pop(0)
    w_refs = [refs.pop(0) for _ in range(n_w)]
    res_ref = refs.pop(0) if has_res else None
    ssq_in_ref = refs.pop(0) if pre_norm else None
    gain_ref = refs.pop(0) if post_norm else None
    o_ref = refs.pop(0)
    bm, bn = o_ref.shape
    ws = [w if w.dtype == BF16 else w[...].astype(BF16) for w in w_refs]
    if post_norm:
        xg_ref, ssq_ref = refs
        if not ssq_partial:
            @pl.when(pl.program_id(1) == 0)
            def _():
                ssq_ref[...] = jnp.zeros_like(ssq_ref)

    for r0 in range(0, bm, sub):
        rows = slice(r0, r0 + sub)
        a = a_ref[rows, :]
        vals = [jnp.dot(a, w[...], preferred_element_type=F32) for w in ws]
        if pre_norm:
            rstd = lax.rsqrt(ssq_in_ref[rows, :] * (1.0 / k_dim) + RMS_EPS)
            rstd = jnp.concatenate([rstd] * (bn // V7X_LANES), axis=1)
            vals = [v * rstd for v in vals]
        if epi == "none":
            o = vals[0]
        elif epi == "swiglu":
            o = vals[0] * _sigmoid(vals[0]) * vals[1]
        elif epi == "res":
            o = res_ref[rows, :] + vals[0]
        elif epi == "glu_res":
            o = res_ref[rows, :] + vals[0] * _sigmoid(vals[1])
        else:
            raise ValueError(epi)
        o_ref[rows, :] = o.astype(o_ref.dtype)
        if post_norm:
            xg_ref[rows, :] = (o * gain_ref[...]).astype(xg_ref.dtype)
            part = jnp.broadcast_to(jnp.sum(o * o, axis=1, keepdims=True), (sub, V7X_LANES))
            if ssq_partial:
                ssq_ref[0, rows, :] = part
            else:
                ssq_ref[rows, :] += part


def matmul(a, w, layer, *, n_out, w_col_offsets, epi, out_dtype, res=None, bm, bn, w_resident=False,
           ssq=None, next_gain=None, name):
    m, k = a.shape
    n_w = len(w_col_offsets)
    post_norm = next_gain is not None
    assert m % bm == 0 and n_out % bn == 0 and bn % V7X_LANES == 0
    assert all(off % bn == 0 for off in w_col_offsets)
    ssq_partial = w_resident and post_norm
    if w_resident:
        grid = (n_out // bn, m // bm)
        ij = lambda g0, g1: (g1, g0)
    else:
        grid = (m // bm, n_out // bn)
        ij = lambda g0, g1: (g0, g1)

    def a_map(g0, g1):
        return (ij(g0, g1)[0], 0)

    def w_map(off_blocks):
        return lambda g0, g1: (layer, 0, ij(g0, g1)[1] + off_blocks)

    def o_map(g0, g1):
        return ij(g0, g1)

    in_specs = [pl.BlockSpec((bm, k), a_map)]
    args = [a]
    for off in w_col_offsets:
        in_specs.append(pl.BlockSpec((None, k, bn), w_map(off // bn)))
        args.append(w)
    if res is not None:
        in_specs.append(pl.BlockSpec((bm, bn), o_map))
        args.append(res)
    if ssq is not None:
        in_specs.append(pl.BlockSpec((bm, V7X_LANES), a_map))
        args.append(ssq)
    out_shape = [jax.ShapeDtypeStruct((m, n_out), out_dtype)]
    out_specs = [pl.BlockSpec((bm, bn), o_map)]
    if post_norm:
        in_specs.append(pl.BlockSpec((1, bn), lambda g0, g1: (0, ij(g0, g1)[1])))
        args.append(next_gain.reshape(1, n_out).astype(F32))
        out_shape.append(jax.ShapeDtypeStruct((m, n_out), BF16))
        out_specs.append(pl.BlockSpec((bm, bn), o_map))
        if ssq_partial:
            out_shape.append(jax.ShapeDtypeStruct((n_out // bn, m, V7X_LANES), F32))
            out_specs.append(pl.BlockSpec((1, bm, V7X_LANES), lambda g0, g1: (g0, g1, 0)))
        else:
            out_shape.append(jax.ShapeDtypeStruct((m, V7X_LANES), F32))
            out_specs.append(pl.BlockSpec((bm, V7X_LANES), a_map))
    sem = ("parallel", "arbitrary") if post_norm and not ssq_partial else ("parallel", "parallel")
    sub = bm
    if (epi == "swiglu" or post_norm) and bm % MM_SUB_ROWS == 0:
        sub = MM_SUB_ROWS
    out = pl.pallas_call(
        functools.partial(_mm_kernel, n_w=n_w, epi=epi, has_res=res is not None,
                          pre_norm=ssq is not None, post_norm=post_norm, ssq_partial=ssq_partial,
                          k_dim=k, sub=sub),
        out_shape=tuple(out_shape),
        grid=grid,
        in_specs=in_specs,
        out_specs=tuple(out_specs),
        compiler_params=pltpu.CompilerParams(dimension_semantics=sem, vmem_limit_bytes=V7X_VMEM_LIMIT),
        name=name,
    )(*args)
    if ssq_partial:
        return out[0], out[1], _sum_slabs(out[2])
    return out if post_norm else out[0]


def _sum_slabs_kernel(p_ref, o_ref):
    o_ref[...] = jnp.sum(p_ref[...], axis=0)


def _sum_slabs(parts):
    ns, m, ln = parts.shape
    bm = _pick(m, (1056, 512, 256, 128, 64, 32, 16, 8))
    return pl.pallas_call(
        _sum_slabs_kernel,
        out_shape=jax.ShapeDtypeStruct((m, ln), F32),
        grid=(m // bm,),
        in_specs=[pl.BlockSpec((ns, bm, ln), lambda i: (0, i, 0))],
        out_specs=pl.BlockSpec((bm, ln), lambda i: (i, 0)),
        compiler_params=_cparams(1),
        name="sum_slabs",
    )(parts)


S5_GPT = V7X_LANES // S5_GROUP


def _s5_operators(lam_re, lam_im, log_step, b_re, b_im, c_re, c_im):
    g, p = lam_re.shape
    j = b_re.shape[-1]
    t = S5_BLOCK
    nl = g // S5_GPT
    hp = lax.Precision.HIGHEST
    step = jnp.exp(log_step.astype(F32))[:, None]
    lam_re = lam_re.astype(F32)
    lam_im = lam_im.astype(F32)
    lr = lam_re * step
    li = lam_im * step
    tau = jnp.arange(t + 1, dtype=F32)[:, None, None]
    mag = jnp.exp(tau * lr[None])
    e_re = mag * jnp.cos(tau * li[None])
    e_im = mag * jnp.sin(tau * li[None])
    lbar_re, lbar_im = e_re[1], e_im[1]
    denom = lam_re * lam_re + lam_im * lam_im
    num_re = lbar_re - 1.0
    coef_re = (num_re * lam_re + lbar_im * lam_im) / denom
    coef_im = (lbar_im * lam_re - num_re * lam_im) / denom
    bt_re = jnp.swapaxes(b_re.astype(F32), 1, 2)
    bt_im = jnp.swapaxes(b_im.astype(F32), 1, 2)
    bb_re = coef_re[:, None, :] * bt_re - coef_im[:, None, :] * bt_im
    bb_im = coef_re[:, None, :] * bt_im + coef_im[:, None, :] * bt_re
    c_re = c_re.astype(F32)
    c_im = c_im.astype(F32)
    eye = jnp.eye(S5_GPT, dtype=F32)

    a_re = e_re[:, :, None, :] * bb_re[None] - e_im[:, :, None, :] * bb_im[None]
    a_im = e_re[:, :, None, :] * bb_im[None] + e_im[:, :, None, :] * bb_re[None]

    al_re = a_re[:t].reshape(t, nl, S5_GPT * j, p)
    al_im = a_im[:t].reshape(t, nl, S5_GPT * j, p)
    cl_re = c_re.reshape(nl, S5_GPT * j, p)
    cl_im = c_im.reshape(nl, S5_GPT * j, p)
    lagfull = (jnp.einsum("tlip,lop->ltio", al_re, cl_re, precision=hp)
               - jnp.einsum("tlip,lop->ltio", al_im, cl_im, precision=hp))
    blockmask = jnp.kron(eye, jnp.ones((j, j), F32))
    bd = (lagfull * blockmask).astype(BF16)

    parity = (jnp.arange(g) % 2)[None, :, None, None]

    def position(src_re, src_im):
        parts = [jnp.where(parity == h, src, 0.0) for src in (src_re, src_im) for h in (0, 1)]
        out = jnp.concatenate(parts, axis=-1).astype(BF16)
        out = out.reshape(t, nl, S5_GPT, j, 4 * p).transpose(1, 0, 2, 3, 4)
        return out.reshape(nl, t * S5_GPT * j, 4 * p)

    w_st = position(a_re[:t][::-1], a_im[:t][::-1])
    ce_re = c_re[None] * e_re[1:, :, None, :] - c_im[None] * e_im[1:, :, None, :]
    ce_im = c_re[None] * e_im[1:, :, None, :] + c_im[None] * e_re[1:, :, None, :]
    w_ro = position(ce_re, -ce_im)

    a_pow = jnp.stack([e_re[t].reshape(nl, S5_GPT * p), e_im[t].reshape(nl, S5_GPT * p)], axis=1)
    return bd, w_st, w_ro, a_pow


def _s5_kernel(x_ref, bd_ref, wstc_ref, wroc_ref, apow_ref, h0_ref, d_ref,
               z_ref, xfp_ref, xfs_ref,
               tbig_ref, wst_ref, wro_ref, ublk_ref, sst_ref, ybuf_ref, *,
               bp, sp, bs, ss, pitch):
    t = S5_BLOCK
    ln = V7X_LANES
    ncp = sp // t
    ncs = ss // t
    mp = bp * sp
    prow = bp * pitch
    rows = ublk_ref.shape[0]
    nslab = sst_ref.shape[0]
    half = nslab // 2

    @pl.when(pl.program_id(0) == 0)
    def _():
        wst_ref[...] = jnp.zeros_like(wst_ref)
        wro_ref[...] = jnp.zeros_like(wro_ref)

    pair_rows = 2 * S5_GROUP
    for src_ref, dst_ref in ((wstc_ref, wst_ref), (wroc_ref, wro_ref)):
        for s in range(t):
            for k in range(half):
                r0 = s * ln + k * pair_rows
                dst_ref[r0:r0 + pair_rows, k * ln:(k + 1) * ln] = src_ref[0, r0:r0 + pair_rows, 0:ln]
                dst_ref[r0:r0 + pair_rows, (half + k) * ln:(half + k + 1) * ln] = (
                    src_ref[0, r0:r0 + pair_rows, ln:2 * ln])

    for s in range(t):
        for tt in range(s, t):
            tbig_ref[s * ln:(s + 1) * ln, tt * ln:(tt + 1) * ln] = bd_ref[0, tt - s]
    for tt in range(0, t, 2):
        tbig_ref[(tt + 1) * ln:(tt + 2) * ln, tt * ln:(tt + 1) * ln] = jnp.zeros((ln, ln), BF16)

    def gather(b, _):
        r0 = pl.multiple_of(b * pitch, V7X_BF16_ROWS)
        for s in range(t):
            ublk_ref[pl.ds(r0, ncp), s * ln:(s + 1) * ln] = (
                x_ref[pl.ds(b * sp + s, ncp, stride=t), :].astype(BF16))
        ublk_ref[pl.ds(r0 + ncp, pitch - ncp), :] = jnp.zeros((pitch - ncp, t * ln), BF16)
        return 0

    lax.fori_loop(0, bp, gather, 0)
    for s in range(t):
        ublk_ref[prow:prow + bs * ncs, s * ln:(s + 1) * ln] = (
            x_ref[pl.ds(mp + s, bs * ncs, stride=t), :].astype(BF16))
    if rows > prow + bs * ncs:
        ublk_ref[prow + bs * ncs:rows, :] = jnp.zeros((rows - prow - bs * ncs, t * ln), BF16)

    rc = rows // 2
    for r0 in (0, rc):
        inj = jnp.dot(ublk_ref[r0:r0 + rc, :], wst_ref[...], preferred_element_type=F32)
        for k in range(nslab):
            sst_ref[k, r0:r0 + rc, :] = inj[:, k * ln:(k + 1) * ln]

    ar = [apow_ref[0, 0:1, k * ln:(k + 1) * ln] for k in range(half)]
    ai = [apow_ref[0, 1:2, k * ln:(k + 1) * ln] for k in range(half)]

    def advance(xr, xi, loc):
        nr, ni = [], []
        for k in range(half):
            s_r = sst_ref[k, loc, :]
            s_i = sst_ref[half + k, loc, :]
            sst_ref[k, loc, :] = xr[k]
            sst_ref[half + k, loc, :] = xi[k]
            nr.append(ar[k] * xr[k] - ai[k] * xi[k] + s_r)
            ni.append(ar[k] * xi[k] + ai[k] * xr[k] + s_i)
        return tuple(nr), tuple(ni)

    for b0 in range(0, bp, V7X_SUBLANES):
        def pstep(c, carry, b0=b0):
            return advance(carry[0], carry[1], pl.ds(b0 * pitch + c, V7X_SUBLANES, stride=pitch))

        zero = tuple(jnp.zeros((V7X_SUBLANES, ln), F32) for _ in range(half))
        xr, xi = lax.fori_loop(0, ncp, pstep, (zero, zero), unroll=2)
        for k in range(half):
            xfp_ref[0, b0:b0 + V7X_SUBLANES, k * ln:(k + 1) * ln] = xr[k]
            xfp_ref[0, b0:b0 + V7X_SUBLANES, (half + k) * ln:(half + k + 1) * ln] = xi[k]
    for b0 in range(0, bs, V7X_SUBLANES):
        xr = tuple(h0_ref[0, b0:b0 + V7X_SUBLANES, k * ln:(k + 1) * ln] for k in range(half))
        xi = tuple(h0_ref[0, b0:b0 + V7X_SUBLANES, (half + k) * ln:(half + k + 1) * ln] for k in range(half))
        for c in range(ncs):
            xr, xi = advance(xr, xi, pl.ds(prow + b0 * ncs + c, V7X_SUBLANES, stride=ncs))
        for k in range(half):
            xfs_ref[0, b0:b0 + V7X_SUBLANES, k * ln:(k + 1) * ln] = xr[k]
            xfs_ref[0, b0:b0 + V7X_SUBLANES, (half + k) * ln:(half + k + 1) * ln] = xi[k]

    d_skip = d_ref[...]
    npair = bp // 2
    for pr in range(npair):
        last = pr == npair - 1
        r0 = pr * 2 * pitch
        nr = (rows - r0) if last else 2 * pitch
        xs = jnp.concatenate([sst_ref[k, r0:r0 + nr, :] for k in range(nslab)], axis=1).astype(BF16)
        for tp in range(t * ln // V7X_MXU):
            c0 = tp * V7X_MXU
            kk = c0 + V7X_MXU
            yc = (jnp.dot(ublk_ref[r0:r0 + nr, 0:kk], tbig_ref[0:kk, c0:c0 + V7X_MXU],
                          preferred_element_type=F32)
                  + lax.dot_general(xs, wro_ref[c0:c0 + V7X_MXU, :], (((1,), (1,)), ((), ())),
                                    preferred_element_type=F32))
            for sub in range(V7X_MXU // ln):
                tok = tp * (V7X_MXU // ln) + sub
                for q in range(2):
                    ybuf_ref[pl.ds(q * sp + tok, ncp, stride=t), :] = (
                        yc[q * pitch:q * pitch + ncp, sub * ln:(sub + 1) * ln])
                if last:
                    ybuf_ref[pl.ds(2 * sp + tok, bs * ncs, stride=t), :] = (
                        yc[2 * pitch:2 * pitch + bs * ncs, sub * ln:(sub + 1) * ln])
        tok0 = pr * 2 * sp
        ntok = 2 * sp + (bs * ss if last else 0)
        z_ref[tok0:tok0 + ntok, :] = _gelu_tanh(
            ybuf_ref[0:ntok, :] + d_skip * x_ref[tok0:tok0 + ntok, :]).astype(z_ref.dtype)


def s5_mixer_core(hn, ops, d_skip, h0_re, h0_im, *, bp, sp, bs, ss):
    bd, w_st, w_ro, a_pow = ops
    m, d = hn.shape
    ln = V7X_LANES
    nl = d // ln
    t = S5_BLOCK
    p8 = a_pow.shape[-1]
    p = p8 // S5_GPT
    assert bp % V7X_SUBLANES == 0 and bs % V7X_SUBLANES == 0 and bp % 2 == 0
    assert sp % t == 0 and ss % t == 0 and m == bp * sp + bs * ss and 2 * p == ln
    ncp, ncs = sp // t, ss // t
    pitch = ncp + V7X_BF16_ROWS - ncp % V7X_BF16_ROWS
    rows = bp * pitch + bs * ncs
    rows += (-rows) % (2 * V7X_BF16_ROWS)
    def slabbed(a):
        return a.astype(F32).reshape(bs, nl, p8).transpose(1, 0, 2)
    h0 = jnp.concatenate([slabbed(h0_re), slabbed(h0_im)], axis=-1)
    nslab = 2 * p8 // ln
    z, xfp, xfs = pl.pallas_call(
        functools.partial(_s5_kernel, bp=bp, sp=sp, bs=bs, ss=ss, pitch=pitch),
        out_shape=(jax.ShapeDtypeStruct((m, d), BF16),
                   jax.ShapeDtypeStruct((nl, bp, 2 * p8), F32),
                   jax.ShapeDtypeStruct((nl, bs, 2 * p8), F32)),
        grid=(nl,),
        in_specs=[pl.BlockSpec((m, ln), lambda l: (0, l), pipeline_mode=pl.Buffered(1)),
                  pl.BlockSpec((1, t, ln, ln), lambda l: (l, 0, 0, 0)),
                  pl.BlockSpec((1, t * ln, 4 * p), lambda l: (l, 0, 0)),
                  pl.BlockSpec((1, t * ln, 4 * p), lambda l: (l, 0, 0)),
                  pl.BlockSpec((1, 2, p8), lambda l: (l, 0, 0)),
                  pl.BlockSpec((1, bs, 2 * p8), lambda l: (l, 0, 0)),
                  pl.BlockSpec((1, ln), lambda l: (0, l))],
        out_specs=(pl.BlockSpec((m, ln), lambda l: (0, l)),
                   pl.BlockSpec((1, bp, 2 * p8), lambda l: (l, 0, 0)),
                   pl.BlockSpec((1, bs, 2 * p8), lambda l: (l, 0, 0))),
        scratch_shapes=[pltpu.VMEM((t * ln, t * ln), BF16),
                        pltpu.VMEM((t * ln, 2 * p8), BF16),
                        pltpu.VMEM((t * ln, 2 * p8), BF16),
                        pltpu.VMEM((rows, t * ln), BF16),
                        pltpu.VMEM((nslab, rows, ln), F32),
                        pltpu.VMEM((2 * sp + bs * ss, ln), F32)],
        compiler_params=pltpu.CompilerParams(dimension_semantics=("arbitrary",),
                                             vmem_limit_bytes=V7X_VMEM_LIMIT),
        name="s5_core",
    )(hn, bd, w_st, w_ro, a_pow, h0, d_skip.reshape(1, d).astype(F32))

    def unslab(a, nb):
        re = a[..., :p8].transpose(1, 0, 2).reshape(nb, nl * S5_GPT, p)
        im = a[..., p8:].transpose(1, 0, 2).reshape(nb, nl * S5_GPT, p)
        return re, im

    return (z,) + unslab(xfp, bp) + unslab(xfs, bs)


QBLK = 2 * CHUNK
KWIN = WINDOW + QBLK
BIAS_W = KWIN + WINDOW
BIAS_LEN = BIAS_W + QBLK
ATTN_GROUP = 2
ATTN_HEAD_GROUP = 4


def _attn_bias_vectors(rel_bias):
    n = np.arange(BIAS_LEN)
    delta = np.where(n < BIAS_LEN - (QBLK - 1), n, n - BIAS_LEN)
    idx = np.clip(WINDOW - delta, -REL_CLIP, REL_CLIP) + REL_CLIP
    return rel_bias.astype(F32)[:, idx][:, None, :]


def _bias_tile(e_row, nq):
    return pltpu.roll(jnp.broadcast_to(e_row, (nq, BIAS_LEN)), 0, 1, stride=1, stride_axis=0)


def _attn_prompt_kernel(q_ref, k_ref, v_ref, e_ref, o_ref, nk_ref, nv_ref, bias_ref, kbf_ref, vbf_ref,
                        *, seq, keep):
    log2e = math.log2(math.e)
    scale = HEAD_DIM ** -0.5 * log2e
    qc = lax.broadcasted_iota(jnp.int32, (QBLK, BIAS_LEN), 0) // CHUNK
    kc = lax.broadcasted_iota(jnp.int32, (QBLK, BIAS_LEN), 1) // CHUNK
    tile = _bias_tile(e_ref[0], QBLK) * log2e
    bias_ref[...] = jnp.where(kc >= qc, jnp.where(kc <= qc + BAND_CHUNKS, tile, NEG_INF), NEG_INF)
    kbf_ref[...] = k_ref[...].astype(BF16)
    vbf_ref[...] = v_ref[...].astype(BF16)
    nblk = seq // QBLK

    def group_blocks(g0):
        blocks = []
        for i in range(g0, min(g0 + ATTN_GROUP, nblk)):
            q0 = i * QBLK
            start = max(q0 - WINDOW, 0)
            blocks.append((q0, start, min(KWIN, seq - start), start - (q0 - WINDOW)))
        return blocks

    def group_scores(blocks):
        ss = []
        for q0, start, kw, shift in blocks:
            q = q_ref[q0:q0 + QBLK, :].astype(BF16)
            k = kbf_ref[start:start + kw, :]
            s = lax.dot_general(q, k, (((1,), (1,)), ((), ())), preferred_element_type=F32)
            ss.append(s * scale + bias_ref[:, shift:shift + kw])
        return ss

    starts = list(range(0, nblk, ATTN_GROUP))
    ss_next = group_scores(group_blocks(starts[0]))
    for gi, g0 in enumerate(starts):
        blocks = group_blocks(g0)
        ss = ss_next
        if gi + 1 < len(starts):
            ss_next = group_scores(group_blocks(starts[gi + 1]))
        ms = [jnp.max(s, axis=-1, keepdims=True) for s in ss]
        es = [jnp.exp2(s - m) for s, m in zip(ss, ms)]
        ls = [jnp.sum(e, axis=-1, keepdims=True) for e in es]
        for (q0, start, kw, shift), e, l in zip(blocks, es, ls):
            v = vbf_ref[start:start + kw, :]
            o = jnp.dot(e.astype(BF16), v, preferred_element_type=F32) * (1.0 / l)
            o_ref[q0:q0 + QBLK, :] = o.astype(o_ref.dtype)
    nk_ref[0] = k_ref[seq - keep:seq, :]
    nv_ref[0] = v_ref[seq - keep:seq, :]


def attn_prompt(qkv, bias_vec, *, bp, sp):
    m, d3 = qkv.shape
    d = d3 // 3
    nh = d // HEAD_DIM
    keep = min(WINDOW, sp)
    assert sp % QBLK == 0
    blk = lambda off: pl.BlockSpec((sp, HEAD_DIM), lambda b, h: (b, h + off))
    tail = pl.BlockSpec((1, keep, HEAD_DIM), lambda b, h: (b, 0, h))
    return pl.pallas_call(
        functools.partial(_attn_prompt_kernel, seq=sp, keep=keep),
        out_shape=(jax.ShapeDtypeStruct((m, d), BF16),
                   jax.ShapeDtypeStruct((bp, keep, d), F32),
                   jax.ShapeDtypeStruct((bp, keep, d), F32)),
        grid=(bp, nh),
        in_specs=[blk(0), blk(nh), blk(2 * nh),
                  pl.BlockSpec((1, 1, BIAS_LEN), lambda b, h: (h, 0, 0))],
        out_specs=(pl.BlockSpec((sp, HEAD_DIM), lambda b, h: (b, h)), tail, tail),
        scratch_shapes=[pltpu.VMEM((QBLK, BIAS_LEN), F32),
                        pltpu.VMEM((sp, HEAD_DIM), BF16),
                        pltpu.VMEM((sp, HEAD_DIM), BF16)],
        compiler_params=_cparams(2),
        name="attn_prompt",
    )(qkv, qkv, qkv, bias_vec)


def _attn_sample_kernel(qkv_ref, ck_ref, cv_ref, e_ref, prev_ref, o_ref, *, seq, ncache, nh, mask):
    del prev_ref
    scale = HEAD_DIM ** -0.5
    d = nh * HEAD_DIM
    dn = (((1,), (1,)), ((), ()))
    off = WINDOW - ncache
    for h0 in range(0, nh, ATTN_HEAD_GROUP):
        heads = range(h0, min(h0 + ATTN_HEAD_GROUP, nh))
        s1s, s2s = [], []
        for h in heads:
            c0 = h * HEAD_DIM
            q = qkv_ref[:, c0:c0 + HEAD_DIM].astype(BF16)
            k = qkv_ref[:, d + c0:d + c0 + HEAD_DIM].astype(BF16)
            ck = ck_ref[0, pl.ds(h, ncache, stride=nh), :].astype(BF16)
            bias = _bias_tile(e_ref[h], seq)
            s1 = lax.dot_general(q, ck, dn, preferred_element_type=F32) * scale + bias[:, off:off + ncache]
            s2 = lax.dot_general(q, k, dn, preferred_element_type=F32) * scale + bias[:, WINDOW:WINDOW + seq]
            if mask is not None:
                s1 = jnp.where(jnp.asarray(mask[:, :ncache]), s1, NEG_INF)
                s2 = jnp.where(jnp.asarray(mask[:, ncache:]), s2, NEG_INF)
            s1s.append(s1)
            s2s.append(s2)
        mxs = [jnp.maximum(jnp.max(s1, axis=-1, keepdims=True), jnp.max(s2, axis=-1, keepdims=True))
               for s1, s2 in zip(s1s, s2s)]
        e1s = [jnp.exp(s1 - mx) for s1, mx in zip(s1s, mxs)]
        e2s = [jnp.exp(s2 - mx) for s2, mx in zip(s2s, mxs)]
        ls = [jnp.sum(e1, axis=-1, keepdims=True) + jnp.sum(e2, axis=-1, keepdims=True)
              for e1, e2 in zip(e1s, e2s)]
        for h, e1, e2, l in zip(heads, e1s, e2s, ls):
            c0 = h * HEAD_DIM
            v = qkv_ref[:, 2 * d + c0:2 * d + c0 + HEAD_DIM].astype(BF16)
            cv = cv_ref[0, pl.ds(h, ncache, stride=nh), :].astype(BF16)
            o = (jnp.dot(e1.astype(BF16), cv, preferred_element_type=F32)
                 + jnp.dot(e2.astype(BF16), v, preferred_element_type=F32)) * (1.0 / l)
            o_ref[:, c0:c0 + HEAD_DIM] = o.astype(o_ref.dtype)


def attn_sample(qkv, cache_k, cache_v, bias_vec, o_prev, *, row0, bs, ss):
    m, d3 = qkv.shape
    d = d3 // 3
    nh = d // HEAD_DIM
    ncache = cache_k.shape[1]
    assert ss <= QBLK and ss % V7X_SUBLANES == 0 and ncache <= WINDOW and row0 % ss == 0
    q_pos = PAST_LEN + np.arange(ss)
    k_pos = PAST_LEN - ncache + np.arange(ncache + ss)
    qc, kc = q_pos // CHUNK, k_pos // CHUNK
    allowed = ((k_pos[None, :] >= 0) & (kc[None, :] <= qc[:, None])
               & (kc[None, :] >= qc[:, None] - BAND_CHUNKS))
    mask = None if allowed.all() else allowed
    ck = cache_k.reshape(bs, ncache * nh, HEAD_DIM)
    cv = cache_v.reshape(bs, ncache * nh, HEAD_DIM)
    blk0 = row0 // ss
    cblk = pl.BlockSpec((1, ncache * nh, HEAD_DIM), lambda b: (b, 0, 0))
    return pl.pallas_call(
        functools.partial(_attn_sample_kernel, seq=ss, ncache=ncache, nh=nh, mask=mask),
        out_shape=jax.ShapeDtypeStruct((m, d), BF16),
        grid=(bs,),
        in_specs=[pl.BlockSpec((ss, d3), lambda b: (b + blk0, 0)), cblk, cblk,
                  pl.BlockSpec((nh, 1, BIAS_LEN), lambda b: (0, 0, 0)),
                  pl.BlockSpec(memory_space=pl.ANY)],
        out_specs=pl.BlockSpec((ss, d), lambda b: (b + blk0, 0)),
        input_output_aliases={4: 0},
        compiler_params=_cparams(1),
        name="attn_sample",
    )(qkv, ck, cv, bias_vec, o_prev)


HGRN_HEADS_PER_STEP = 4


def _hgrn_kernel(*refs, seq, csz, has_state, aliased, hp):
    refs = list(refs)
    q_ref, f_ref, v_ref, z_ref, llb_ref, l1m_ref, gain_ref = refs[:7]
    pos = 7
    s0_ref = None
    if has_state:
        s0_ref = refs[pos]
        pos += 1
    if aliased:
        pos += 1
    o_ref, sn_ref, st_ref = refs[pos:pos + 3]
    kdim = HEAD_DIM
    scale = kdim ** -0.5
    dn_t = (((1,), (1,)), ((), ()))
    for hh in range(hp):
        if has_state:
            st_ref[hh] = s0_ref[0, hh].T
        else:
            st_ref[hh] = jnp.zeros((kdim, kdim), F32)
    row = lax.broadcasted_iota(jnp.int32, (csz, kdim), 0)
    ti = lax.broadcasted_iota(jnp.int32, (csz, csz), 0)
    si = lax.broadcasted_iota(jnp.int32, (csz, csz), 1)
    tx = ti ^ si
    lvl = jnp.where(ti > si, 31 - lax.clz(tx), -1)

    tri = jnp.where(lax.broadcasted_iota(jnp.int32, (csz, csz), 0)
                    >= lax.broadcasted_iota(jnp.int32, (csz, csz), 1), 1.0, 0.0).astype(BF16)

    def gates(r0, hh):
        sl = slice(hh * kdim, (hh + 1) * kdim)
        llb = llb_ref[:, sl]
        l1m = l1m_ref[:, sl]
        qz = q_ref[pl.ds(r0, csz), sl]
        fz = f_ref[pl.ds(r0, csz), sl]
        ls = jnp.minimum(fz, 0.0) - jnp.log(1.0 + jnp.exp(jnp.minimum(fz, -fz)))
        cc = l1m + ls
        dd = llb - cc
        g = jnp.maximum(llb, cc) + jnp.log(1.0 + jnp.exp(jnp.minimum(dd, -dd)))
        kk = jnp.exp(cc - fz)
        q = qz * _sigmoid(qz) * scale
        return g, kk, q

    def prefix_sums(gs):
        parts = []
        for g in gs:
            hi = g.astype(BF16)
            r1 = g - hi.astype(F32)
            mid = r1.astype(BF16)
            lo = (r1 - mid.astype(F32)).astype(BF16)
            parts += [hi, mid, lo]
        sums = jnp.dot(tri, jnp.concatenate(parts, axis=1), preferred_element_type=F32)
        out = []
        for i in range(len(gs)):
            c0 = 3 * i * kdim
            out.append(sums[:, c0:c0 + kdim] + sums[:, c0 + kdim:c0 + 2 * kdim]
                       + sums[:, c0 + 2 * kdim:c0 + 3 * kdim])
        return out

    def one_head(r0, hh, b, kk, q):
        sl = slice(hh * kdim, (hh + 1) * kdim)
        v = v_ref[pl.ds(r0, csz), sl]
        gz = z_ref[pl.ds(r0, csz), sl]
        v_bf = v.astype(BF16)
        sc = lax.dot_general(q.astype(BF16), kk.astype(BF16), dn_t, preferred_element_type=F32)
        scores = jnp.where(tx == 0, sc, 0.0)
        e = b
        w = 1
        while w < min(V7X_SUBLANES, csz):
            bit = (row & w) != 0
            e_prev = pltpu.roll(e, w, 0)
            ex = jnp.exp(jnp.where(bit, b - e_prev, e - b))
            qk = (jnp.where(bit, q, kk) * ex).astype(BF16)
            sc = lax.dot_general(qk, qk, dn_t, preferred_element_type=F32)
            scores = scores + jnp.where(lvl == int(math.log2(w)), sc, 0.0)
            if 2 * w < V7X_SUBLANES:
                e = jnp.where(bit, e, pltpu.roll(e, csz - w, 0))
            w *= 2
        while w < csz:
            n2 = csz // (2 * w)
            b4 = b.reshape(n2, 2, w, kdim)
            bound = b4[:, 0, w - 1:w, :]
            kpart = kk.reshape(n2, 2, w, kdim)[:, 0] * jnp.exp(bound - b4[:, 0])
            qpart = q.reshape(n2, 2, w, kdim)[:, 1] * jnp.exp(b4[:, 1] - bound)
            zero = jnp.zeros((n2, 1, w, kdim), F32)
            kw = jnp.concatenate([kpart[:, None], zero], axis=1).reshape(csz, kdim).astype(BF16)
            qw = jnp.concatenate([zero, qpart[:, None]], axis=1).reshape(csz, kdim).astype(BF16)
            sc = lax.dot_general(qw, kw, dn_t, preferred_element_type=F32)
            if 2 * w < csz:
                scores = scores + jnp.where(tx < 2 * w, sc, 0.0)
            else:
                scores = scores + sc
            w *= 2
        o = jnp.dot(scores.astype(BF16), v_bf, preferred_element_type=F32)
        st = st_ref[hh]
        o = o + lax.dot_general((q * jnp.exp(b)).astype(BF16), st.astype(BF16), dn_t,
                                preferred_element_type=F32)
        b_last = b[csz - 1:csz]
        kd = (kk * jnp.exp(b_last - b)).astype(BF16)
        st_ref[hh] = jnp.exp(b_last) * st + lax.dot_general(
            v_bf, kd, (((0,), (0,)), ((), ())), preferred_element_type=F32)
        o = o * lax.rsqrt(jnp.mean(o * o, axis=-1, keepdims=True) + RMS_EPS)
        o = o * gain_ref[:, sl] * (gz * _sigmoid(gz))
        o_ref[pl.ds(r0, csz), sl] = o.astype(o_ref.dtype)

    def chunk(ci, _):
        r0 = pl.multiple_of(ci * csz, csz)
        gkq = [gates(r0, hh) for hh in range(hp)]
        bs = prefix_sums([g for g, _, _ in gkq])
        for hh in range(hp):
            one_head(r0, hh, bs[hh], gkq[hh][1], gkq[hh][2])
        return 0

    nchunks = seq // csz
    if nchunks == 1:
        chunk(0, 0)
    else:
        lax.fori_loop(0, nchunks, chunk, 0)
    for hh in range(hp):
        sn_ref[0, hh] = st_ref[hh].T


def hgrn_core(proj, log_lb, log_1m_lb, norm_gain, state, o_prev, *, row0, nb, seq):
    m, d4 = proj.shape
    d = d4 // 4
    nh = d // HEAD_DIM
    csz = _pick(seq, (128, 64, 32, 16, 8))
    hp = HGRN_HEADS_PER_STEP * max(1, V7X_LANES // (2 * csz))
    while nh % hp:
        hp //= 2
    ng = nh // hp
    w = hp * HEAD_DIM
    assert row0 % seq == 0
    blk0 = row0 // seq
    has_state = state is not None
    aliased = o_prev is not None
    blk = lambda off: pl.BlockSpec((seq, w), lambda b, h: (b + blk0, h + off * ng))
    vec = pl.BlockSpec((1, w), lambda b, h: (0, h))
    sblk = pl.BlockSpec((1, hp, HEAD_DIM, HEAD_DIM), lambda b, h: (b, h, 0, 0))
    in_specs = [blk(0), blk(1), blk(2), blk(3), vec, vec, vec]
    args = [proj, proj, proj, proj, log_lb.reshape(1, d), log_1m_lb.reshape(1, d),
            norm_gain.reshape(1, d).astype(F32)]
    if has_state:
        in_specs.append(sblk)
        args.append(state.astype(F32))
    aliases = {}
    if aliased:
        aliases = {len(args): 0}
        in_specs.append(pl.BlockSpec(memory_space=pl.ANY))
        args.append(o_prev)
    return pl.pallas_call(
        functools.partial(_hgrn_kernel, seq=seq, csz=csz, has_state=has_state, aliased=aliased, hp=hp),
        out_shape=(jax.ShapeDtypeStruct((m, d), BF16),
                   jax.ShapeDtypeStruct((nb, nh, HEAD_DIM, HEAD_DIM), F32)),
        grid=(nb, ng),
        in_specs=in_specs,
        out_specs=(pl.BlockSpec((seq, w), lambda b, h: (b + blk0, h)), sblk),
        scratch_shapes=[pltpu.VMEM((hp, HEAD_DIM, HEAD_DIM), F32)],
        input_output_aliases=aliases,
        compiler_params=_cparams(2),
        name="hgrn_core",
    )(*args)


def kernel(x_prompt, x_sample, state_s5_re, state_s5_im, cache_attn_k, cache_attn_v, state_hgrn,
           norm_mixer, norm_ffn, norm_final, ffn_w_gate_up, ffn_w_down,
           s5_lambda_re, s5_lambda_im, s5_log_step, s5_b_re, s5_b_im, s5_c_re, s5_c_im, s5_d, s5_w_glu,
           attn_w_qkv, attn_rel_bias, attn_w_o,
           hgrn_w_in, hgrn_lower_bounds, hgrn_norm, hgrn_w_o):
    bp, sp, d = x_prompt.shape
    bs, ss, _ = x_sample.shape
    depth = norm_mixer.shape[0]
    mp = bp * sp
    ms = bs * ss
    m = mp + ms
    d_ff = ffn_w_down.shape[1]
    nh = d // HEAD_DIM

    assert depth > 0
    x = None
    bm = _pick(m, (1536, 768, 512, 256, 128, 64, 32, 16, 8))
    bm_down = _pick(m, (512, 256, 128, 64, 32, 16, 8))
    bn = _pick(d, (256, 128))
    bn_proj = _pick(d, (512, 256, 128))
    bn_down = _pick(d, (512, 256, 128))
    bn_ff = _pick(d_ff, (256, 128))

    w_gate_up = ffn_w_gate_up
    w_qkv = attn_w_qkv
    w_hin = hgrn_w_in
    w_down = ffn_w_down.astype(BF16)
    w_glu = s5_w_glu.astype(BF16)
    w_ao = attn_w_o.astype(BF16)
    w_ho = hgrn_w_o.astype(BF16)

    lbs = jnp.cumsum(jax.nn.softmax(hgrn_lower_bounds.astype(F32), axis=0), axis=0)
    lbs = lbs - lbs[0]

    new_re_p, new_im_p, new_re_s, new_im_s = [], [], [], []
    new_k_p, new_v_p, new_k_s, new_v_s = [], [], [], []
    new_h_p, new_h_s = [], []

    for layer in range(depth):
        kind = layer % N_MIXERS
        j = layer // N_MIXERS
        if kind == 0:
            if layer == 0:
                hn, x = rmsnorm_concat([x_prompt.reshape(mp, d).astype(F32), x_sample.reshape(ms, d).astype(F32)],
                                       norm_mixer[layer])
            else:
                hn = rmsnorm(x, norm_mixer[layer], F32)
            ops = _s5_operators(s5_lambda_re[j], s5_lambda_im[j], s5_log_step[j],
                                s5_b_re[j], s5_b_im[j], s5_c_re[j], s5_c_im[j])
            z, re_p, im_p, re_s, im_s = s5_mixer_core(
                hn, ops, s5_d[j], state_s5_re[j], state_s5_im[j], bp=bp, sp=sp, bs=bs, ss=ss)
            new_re_p.append(re_p)
            new_im_p.append(im_p)
            new_re_s.append(re_s)
            new_im_s.append(im_s)
            x, xg, ssq = matmul(z, w_glu, j, n_out=d, w_col_offsets=(0, d), epi="glu_res",
                                out_dtype=F32, res=x, bm=bm, bn=bn, next_gain=norm_ffn[layer], name="s5_glu")
        elif kind == 1:
            qkv = matmul(xg, w_qkv, j, n_out=3 * d, w_col_offsets=(0,), epi="none",
                         out_dtype=F32, bm=bm, bn=bn_proj, ssq=ssq, name="attn_qkv")
            bias_vec = _attn_bias_vectors(attn_rel_bias[j])
            o, nk_p, nv_p = attn_prompt(qkv, bias_vec, bp=bp, sp=sp)
            o = attn_sample(qkv, cache_attn_k[j], cache_attn_v[j], bias_vec, o, row0=mp, bs=bs, ss=ss)
            qkv_s = qkv[mp:].reshape(bs, ss, 3 * d)
            new_k_p.append(nk_p.reshape(bp, -1, nh, HEAD_DIM))
            new_v_p.append(nv_p.reshape(bp, -1, nh, HEAD_DIM))
            k_all = jnp.concatenate([cache_attn_k[j], qkv_s[:, :, d:2 * d].reshape(bs, ss, nh, HEAD_DIM)], axis=1)
            v_all = jnp.concatenate([cache_attn_v[j], qkv_s[:, :, 2 * d:].reshape(bs, ss, nh, HEAD_DIM)], axis=1)
            new_k_s.append(k_all[:, ss:])
            new_v_s.append(v_all[:, ss:])
            x, xg, ssq = matmul(o, w_ao, j, n_out=d, w_col_offsets=(0,), epi="res",
                                out_dtype=F32, res=x, bm=bm, bn=bn_proj, next_gain=norm_ffn[layer], name="attn_wo")
        else:
            proj = matmul(xg, w_hin, j, n_out=4 * d, w_col_offsets=(0,), epi="none",
                          out_dtype=F32, bm=bm, bn=bn_proj, ssq=ssq, name="hgrn_win")
            lb = lbs[layer]
            log_lb = jnp.log(lb)
            log_1m_lb = jnp.log1p(-lb)
            o, h_p = hgrn_core(proj, log_lb, log_1m_lb, hgrn_norm[j], None, None, row0=0, nb=bp, seq=sp)
            o, h_s = hgrn_core(proj, log_lb, log_1m_lb, hgrn_norm[j], state_hgrn[j], o, row0=mp, nb=bs, seq=ss)
            new_h_p.append(h_p)
            new_h_s.append(h_s)
            x, xg, ssq = matmul(o, w_ho, j, n_out=d, w_col_offsets=(0,), epi="res",
                                out_dtype=F32, res=x, bm=bm, bn=bn_proj, next_gain=norm_ffn[layer], name="hgrn_wo")
        act = matmul(xg, w_gate_up, layer, n_out=d_ff, w_col_offsets=(0, d_ff),
                     epi="swiglu", out_dtype=BF16, bm=bm, bn=bn_ff, ssq=ssq, name="ffn_gate_up")
        fuse_next = layer + 1 < depth and (layer + 1) % N_MIXERS != 0
        if fuse_next:
            x, xg, ssq = matmul(act, w_down, layer, n_out=d, w_col_offsets=(0,), epi="res",
                                out_dtype=F32, res=x, bm=bm_down, bn=bn_down, w_resident=True,
                                next_gain=norm_mixer[layer + 1], name="ffn_down")
        else:
            x = matmul(act, w_down, layer, n_out=d, w_col_offsets=(0,), epi="res",
                       out_dtype=F32, res=x, bm=bm_down, bn=bn_down, w_resident=True, name="ffn_down")

    y_p = rmsnorm(x, norm_final, F32, 0, mp).reshape(bp, sp, d)
    y_s = rmsnorm(x, norm_final, F32, mp, ms).reshape(bs, ss, d)
    return (y_p, y_s,
            jnp.stack(new_re_p), jnp.stack(new_im_p),
            jnp.stack(new_k_p), jnp.stack(new_v_p), jnp.stack(new_h_p),
            jnp.stack(new_re_s), jnp.stack(new_im_s),
            jnp.stack(new_k_s), jnp.stack(new_v_s), jnp.stack(new_h_s))
```

```python
import functools
import math

import numpy as np
import jax
import jax.numpy as jnp
from jax import lax
from jax.experimental import pallas as pl
from jax.experimental.pallas import tpu as pltpu

F32 = jnp.float32
BF16 = jnp.bfloat16

PAST_LEN = 2048
CHUNK = 64
BAND_CHUNKS = 8
WINDOW = BAND_CHUNKS * CHUNK
REL_CLIP = 128
N_MIXERS = 3
S5_GROUP = 16
S5_BLOCK = 16
HEAD_DIM = 128
RMS_EPS = 1e-6
NEG_INF = -1e30

V7X_LANES = 128
V7X_SUBLANES = 8
V7X_BF16_ROWS = 16
V7X_MXU = 256
V7X_VMEM_LIMIT = 56 * 1024 * 1024


def _pick(n, candidates):
    for c in candidates:
        if n % c == 0:
            return c
    return n


def _cparams(n_axes):
    return pltpu.CompilerParams(
        dimension_semantics=("parallel",) * n_axes,
        vmem_limit_bytes=V7X_VMEM_LIMIT)


def _sigmoid(x):
    return 1.0 / (1.0 + jnp.exp(-x))


def _gelu_tanh(x):
    c = math.sqrt(2.0 / math.pi)
    return 0.5 * x * (1.0 + jnp.tanh(c * (x + 0.044715 * (x * x * x))))


def _rmsnorm_kernel(x_ref, g_ref, o_ref):
    x = x_ref[...]
    ms = jnp.mean(x * x, axis=-1, keepdims=True)
    o_ref[...] = (x * lax.rsqrt(ms + RMS_EPS) * g_ref[...]).astype(o_ref.dtype)


def rmsnorm(x, gain, out_dtype, row0=0, rows=None):
    m, d = x.shape
    rows = m if rows is None else rows
    bm = _pick(math.gcd(rows, row0) if row0 else rows, (256, 128, 64, 32, 16, 8))
    blk0 = row0 // bm
    return pl.pallas_call(
        _rmsnorm_kernel,
        out_shape=jax.ShapeDtypeStruct((rows, d), out_dtype),
        grid=(rows // bm,),
        in_specs=[pl.BlockSpec((bm, d), lambda i: (i + blk0, 0)),
                  pl.BlockSpec((1, d), lambda i: (0, 0))],
        out_specs=pl.BlockSpec((bm, d), lambda i: (i, 0)),
        compiler_params=_cparams(1),
        name="rmsnorm",
    )(x, gain.reshape(1, d).astype(F32))


def _rmsnorm_copy_kernel(*refs):
    x_ref, g_ref = refs[:2]
    o_ref, c_ref = refs[-2:]
    x = x_ref[...]
    ms = jnp.mean(x * x, axis=-1, keepdims=True)
    o_ref[...] = x * lax.rsqrt(ms + RMS_EPS) * g_ref[...]
    c_ref[...] = x


def rmsnorm_concat(parts, gain):
    d = parts[0].shape[1]
    m = sum(p.shape[0] for p in parts)
    g2 = gain.reshape(1, d).astype(F32)
    outs = None
    row0 = 0
    for p in parts:
        rows = p.shape[0]
        bm = _pick(math.gcd(rows, row0) if row0 else rows, (256, 128, 64, 32, 16, 8))
        blk0 = row0 // bm
        row = pl.BlockSpec((bm, d), lambda i: (i, 0))
        orow = pl.BlockSpec((bm, d), functools.partial(lambda blk0, i: (i + blk0, 0), blk0))
        in_specs = [row, pl.BlockSpec((1, d), lambda i: (0, 0))]
        args = [p, g2]
        aliases = {}
        if outs is not None:
            in_specs += [pl.BlockSpec(memory_space=pl.ANY)] * 2
            args += list(outs)
            aliases = {2: 0, 3: 1}
        outs = pl.pallas_call(
            _rmsnorm_copy_kernel,
            out_shape=(jax.ShapeDtypeStruct((m, d), F32), jax.ShapeDtypeStruct((m, d), F32)),
            grid=(rows // bm,),
            in_specs=in_specs,
            out_specs=(orow, orow),
            input_output_aliases=aliases,
            compiler_params=_cparams(1),
            name="rmsnorm_concat",
        )(*args)
        row0 += rows
    return outs


MM_SUB_ROWS = 512
def _mm_kernel(*refs, n_w, epi, has_res, pre_norm, post_norm, k_dim, sub):
    refs = list(refs)
    a_ref = refs.pop(0)
    w_refs = [refs.pop(0) for _ in range(n_w)]
    res_ref = refs.pop(0) if has_res else None
    ssq_in_ref = refs.pop(0) if pre_norm else None
    gain_ref = refs.pop(0) if post_norm else None
    o_ref = refs.pop(0)
    bm, bn = o_ref.shape
    ws = [w if w.dtype == BF16 else w[...].astype(BF16) for w in w_refs]
    if post_norm:
        xg_ref, ssq_ref = refs

        @pl.when(pl.program_id(1) == 0)
        def _():
            ssq_ref[...] = jnp.zeros_like(ssq_ref)

    for r0 in range(0, bm, sub):
        rows = slice(r0, r0 + sub)
        a = a_ref[rows, :]
        vals = [jnp.dot(a, w[...], preferred_element_type=F32) for w in ws]
        if pre_norm:
            rstd = lax.rsqrt(ssq_in_ref[rows, :] * (1.0 / k_dim) + RMS_EPS)
            rstd = jnp.concatenate([rstd] * (bn // V7X_LANES), axis=1)
            vals = [v * rstd for v in vals]
        if epi == "none":
            o = vals[0]
        elif epi == "swiglu":
            o = vals[0] * _sigmoid(vals[0]) * vals[1]
        elif epi == "res":
            o = res_ref[rows, :] + vals[0]
        elif epi == "glu_res":
            o = res_ref[rows, :] + vals[0] * _sigmoid(vals[1])
        else:
            raise ValueError(epi)
        o_ref[rows, :] = o.astype(o_ref.dtype)
        if post_norm:
            xg_ref[rows, :] = (o * gain_ref[...]).astype(xg_ref.dtype)
            ssq_ref[rows, :] += jnp.broadcast_to(jnp.sum(o * o, axis=1, keepdims=True), (sub, V7X_LANES))


def matmul(a, w, layer, *, n_out, w_col_offsets, epi, out_dtype, res=None, bm, bn, w_resident=False,
           ssq=None, next_gain=None, name):
    m, k = a.shape
    n_w = len(w_col_offsets)
    post_norm = next_gain is not None
    assert m % bm == 0 and n_out % bn == 0 and bn % V7X_LANES == 0
    assert all(off % bn == 0 for off in w_col_offsets)
    assert not (w_resident and post_norm)
    if w_resident:
        grid = (n_out // bn, m // bm)
        ij = lambda g0, g1: (g1, g0)
    else:
        grid = (m // bm, n_out // bn)
        ij = lambda g0, g1: (g0, g1)

    def a_map(g0, g1):
        return (ij(g0, g1)[0], 0)

    def w_map(off_blocks):
        return lambda g0, g1: (layer, 0, ij(g0, g1)[1] + off_blocks)

    def o_map(g0, g1):
        return ij(g0, g1)

    in_specs = [pl.BlockSpec((bm, k), a_map)]
    args = [a]
    for off in w_col_offsets:
        in_specs.append(pl.BlockSpec((None, k, bn), w_map(off // bn)))
        args.append(w)
    if res is not None:
        in_specs.append(pl.BlockSpec((bm, bn), o_map))
        args.append(res)
    if ssq is not None:
        in_specs.append(pl.BlockSpec((bm, V7X_LANES), a_map))
        args.append(ssq)
    out_shape = [jax.ShapeDtypeStruct((m, n_out), out_dtype)]
    out_specs = [pl.BlockSpec((bm, bn), o_map)]
    if post_norm:
        in_specs.append(pl.BlockSpec((1, bn), lambda g0, g1: (0, ij(g0, g1)[1])))
        args.append(next_gain.reshape(1, n_out).astype(F32))
        out_shape += [jax.ShapeDtypeStruct((m, n_out), BF16), jax.ShapeDtypeStruct((m, V7X_LANES), F32)]
        out_specs += [pl.BlockSpec((bm, bn), o_map), pl.BlockSpec((bm, V7X_LANES), a_map)]
    sem = ("parallel", "arbitrary") if post_norm else ("parallel", "parallel")
    sub = bm
    if (epi == "swiglu" or post_norm) and bm % MM_SUB_ROWS == 0:
        sub = MM_SUB_ROWS
    out = pl.pallas_call(
        functools.partial(_mm_kernel, n_w=n_w, epi=epi, has_res=res is not None,
                          pre_norm=ssq is not None, post_norm=post_norm, k_dim=k, sub=sub),
        out_shape=tuple(out_shape),
        grid=grid,
        in_specs=in_specs,
        out_specs=tuple(out_specs),
        compiler_params=pltpu.CompilerParams(dimension_semantics=sem, vmem_limit_bytes=V7X_VMEM_LIMIT),
        name=name,
    )(*args)
    return out if post_norm else out[0]


S5_GPT = V7X_LANES // S5_GROUP


def _s5_operators(lam_re, lam_im, log_step, b_re, b_im, c_re, c_im):
    g, p = lam_re.shape
    j = b_re.shape[-1]
    t = S5_BLOCK
    nl = g // S5_GPT
    hp = lax.Precision.HIGHEST
    step = jnp.exp(log_step.astype(F32))[:, None]
    lam_re = lam_re.astype(F32)
    lam_im = lam_im.astype(F32)
    lr = lam_re * step
    li = lam_im * step
    tau = jnp.arange(t + 1, dtype=F32)[:, None, None]
    mag = jnp.exp(tau * lr[None])
    e_re = mag * jnp.cos(tau * li[None])
    e_im = mag * jnp.sin(tau * li[None])
    lbar_re, lbar_im = e_re[1], e_im[1]
    denom = lam_re * lam_re + lam_im * lam_im
    num_re = lbar_re - 1.0
    coef_re = (num_re * lam_re + lbar_im * lam_im) / denom
    coef_im = (lbar_im * lam_re - num_re * lam_im) / denom
    bt_re = jnp.swapaxes(b_re.astype(F32), 1, 2)
    bt_im = jnp.swapaxes(b_im.astype(F32), 1, 2)
    bb_re = coef_re[:, None, :] * bt_re - coef_im[:, None, :] * bt_im
    bb_im = coef_re[:, None, :] * bt_im + coef_im[:, None, :] * bt_re
    c_re = c_re.astype(F32)
    c_im = c_im.astype(F32)
    eye = jnp.eye(S5_GPT, dtype=F32)

    a_re = e_re[:, :, None, :] * bb_re[None] - e_im[:, :, None, :] * bb_im[None]
    a_im = e_re[:, :, None, :] * bb_im[None] + e_im[:, :, None, :] * bb_re[None]

    al_re = a_re[:t].reshape(t, nl, S5_GPT * j, p)
    al_im = a_im[:t].reshape(t, nl, S5_GPT * j, p)
    cl_re = c_re.reshape(nl, S5_GPT * j, p)
    cl_im = c_im.reshape(nl, S5_GPT * j, p)
    lagfull = (jnp.einsum("tlip,lop->ltio", al_re, cl_re, precision=hp)
               - jnp.einsum("tlip,lop->ltio", al_im, cl_im, precision=hp))
    blockmask = jnp.kron(eye, jnp.ones((j, j), F32))
    bd = (lagfull * blockmask).astype(BF16)

    parity = (jnp.arange(g) % 2)[None, :, None, None]

    def position(src_re, src_im):
        parts = [jnp.where(parity == h, src, 0.0) for src in (src_re, src_im) for h in (0, 1)]
        out = jnp.concatenate(parts, axis=-1).astype(BF16)
        out = out.reshape(t, nl, S5_GPT, j, 4 * p).transpose(1, 0, 2, 3, 4)
        return out.reshape(nl, t * S5_GPT * j, 4 * p)

    w_st = position(a_re[:t][::-1], a_im[:t][::-1])
    ce_re = c_re[None] * e_re[1:, :, None, :] - c_im[None] * e_im[1:, :, None, :]
    ce_im = c_re[None] * e_im[1:, :, None, :] + c_im[None] * e_re[1:, :, None, :]
    w_ro = position(ce_re, -ce_im)

    a_pow = jnp.stack([e_re[t].reshape(nl, S5_GPT * p), e_im[t].reshape(nl, S5_GPT * p)], axis=1)
    return bd, w_st, w_ro, a_pow


def _s5_kernel(x_ref, bd_ref, wstc_ref, wroc_ref, apow_ref, h0_ref, d_ref,
               z_ref, xfp_ref, xfs_ref,
               tbig_ref, wst_ref, wro_ref, ublk_ref, sst_ref, ybuf_ref, *,
               bp, sp, bs, ss, pitch):
    t = S5_BLOCK
    ln = V7X_LANES
    ncp = sp // t
    ncs = ss // t
    mp = bp * sp
    prow = bp * pitch
    rows = ublk_ref.shape[0]
    nslab = sst_ref.shape[0]
    half = nslab // 2

    @pl.when(pl.program_id(0) == 0)
    def _():
        wst_ref[...] = jnp.zeros_like(wst_ref)
        wro_ref[...] = jnp.zeros_like(wro_ref)

    pair_rows = 2 * S5_GROUP
    for src_ref, dst_ref in ((wstc_ref, wst_ref), (wroc_ref, wro_ref)):
        for s in range(t):
            for k in range(half):
                r0 = s * ln + k * pair_rows
                dst_ref[r0:r0 + pair_rows, k * ln:(k + 1) * ln] = src_ref[0, r0:r0 + pair_rows, 0:ln]
                dst_ref[r0:r0 + pair_rows, (half + k) * ln:(half + k + 1) * ln] = (
                    src_ref[0, r0:r0 + pair_rows, ln:2 * ln])

    for s in range(t):
        for tt in range(s, t):
            tbig_ref[s * ln:(s + 1) * ln, tt * ln:(tt + 1) * ln] = bd_ref[0, tt - s]
    for tt in range(0, t, 2):
        tbig_ref[(tt + 1) * ln:(tt + 2) * ln, tt * ln:(tt + 1) * ln] = jnp.zeros((ln, ln), BF16)

    def gather(b, _):
        r0 = pl.multiple_of(b * pitch, V7X_BF16_ROWS)
        for s in range(t):
            ublk_ref[pl.ds(r0, ncp), s * ln:(s + 1) * ln] = (
                x_ref[pl.ds(b * sp + s, ncp, stride=t), :].astype(BF16))
        ublk_ref[pl.ds(r0 + ncp, pitch - ncp), :] = jnp.zeros((pitch - ncp, t * ln), BF16)
        return 0

    lax.fori_loop(0, bp, gather, 0)
    for s in range(t):
        ublk_ref[prow:prow + bs * ncs, s * ln:(s + 1) * ln] = (
            x_ref[pl.ds(mp + s, bs * ncs, stride=t), :].astype(BF16))
    if rows > prow + bs * ncs:
        ublk_ref[prow + bs * ncs:rows, :] = jnp.zeros((rows - prow - bs * ncs, t * ln), BF16)

    rc = rows // 2
    for r0 in (0, rc):
        inj = jnp.dot(ublk_ref[r0:r0 + rc, :], wst_ref[...], preferred_element_type=F32)
        for k in range(nslab):
            sst_ref[k, r0:r0 + rc, :] = inj[:, k * ln:(k + 1) * ln]

    ar = [apow_ref[0, 0:1, k * ln:(k + 1) * ln] for k in range(half)]
    ai = [apow_ref[0, 1:2, k * ln:(k + 1) * ln] for k in range(half)]

    def advance(xr, xi, loc):
        nr, ni = [], []
        for k in range(half):
            s_r = sst_ref[k, loc, :]
            s_i = sst_ref[half + k, loc, :]
            sst_ref[k, loc, :] = xr[k]
            sst_ref[half + k, loc, :] = xi[k]
            nr.append(ar[k] * xr[k] - ai[k] * xi[k] + s_r)
            ni.append(ar[k] * xi[k] + ai[k] * xr[k] + s_i)
        return tuple(nr), tuple(ni)

    for b0 in range(0, bp, V7X_SUBLANES):
        def pstep(c, carry, b0=b0):
            return advance(carry[0], carry[1], pl.ds(b0 * pitch + c, V7X_SUBLANES, stride=pitch))

        zero = tuple(jnp.zeros((V7X_SUBLANES, ln), F32) for _ in range(half))
        xr, xi = lax.fori_loop(0, ncp, pstep, (zero, zero), unroll=2)
        for k in range(half):
            xfp_ref[0, b0:b0 + V7X_SUBLANES, k * ln:(k + 1) * ln] = xr[k]
            xfp_ref[0, b0:b0 + V7X_SUBLANES, (half + k) * ln:(half + k + 1) * ln] = xi[k]
    for b0 in range(0, bs, V7X_SUBLANES):
        xr = tuple(h0_ref[0, b0:b0 + V7X_SUBLANES, k * ln:(k + 1) * ln] for k in range(half))
        xi = tuple(h0_ref[0, b0:b0 + V7X_SUBLANES, (half + k) * ln:(half + k + 1) * ln] for k in range(half))
        for c in range(ncs):
            xr, xi = advance(xr, xi, pl.ds(prow + b0 * ncs + c, V7X_SUBLANES, stride=ncs))
        for k in range(half):
            xfs_ref[0, b0:b0 + V7X_SUBLANES, k * ln:(k + 1) * ln] = xr[k]
            xfs_ref[0, b0:b0 + V7X_SUBLANES, (half + k) * ln:(half + k + 1) * ln] = xi[k]

    d_skip = d_ref[...]
    npair = bp // 2
    for pr in range(npair):
        last = pr == npair - 1
        r0 = pr * 2 * pitch
        nr = (rows - r0) if last else 2 * pitch
        xs = jnp.concatenate([sst_ref[k, r0:r0 + nr, :] for k in range(nslab)], axis=1).astype(BF16)
        for tp in range(t * ln // V7X_MXU):
            c0 = tp * V7X_MXU
            kk = c0 + V7X_MXU
            yc = (jnp.dot(ublk_ref[r0:r0 + nr, 0:kk], tbig_ref[0:kk, c0:c0 + V7X_MXU],
                          preferred_element_type=F32)
                  + lax.dot_general(xs, wro_ref[c0:c0 + V7X_MXU, :], (((1,), (1,)), ((), ())),
                                    preferred_element_type=F32))
            for sub in range(V7X_MXU // ln):
                tok = tp * (V7X_MXU // ln) + sub
                for q in range(2):
                    ybuf_ref[pl.ds(q * sp + tok, ncp, stride=t), :] = (
                        yc[q * pitch:q * pitch + ncp, sub * ln:(sub + 1) * ln])
                if last:
                    ybuf_ref[pl.ds(2 * sp + tok, bs * ncs, stride=t), :] = (
                        yc[2 * pitch:2 * pitch + bs * ncs, sub * ln:(sub + 1) * ln])
        tok0 = pr * 2 * sp
        ntok = 2 * sp + (bs * ss if last else 0)
        z_ref[tok0:tok0 + ntok, :] = _gelu_tanh(
            ybuf_ref[0:ntok, :] + d_skip * x_ref[tok0:tok0 + ntok, :]).astype(z_ref.dtype)


def s5_mixer_core(hn, ops, d_skip, h0_re, h0_im, *, bp, sp, bs, ss):
    bd, w_st, w_ro, a_pow = ops
    m, d = hn.shape
    ln = V7X_LANES
    nl = d // ln
    t = S5_BLOCK
    p8 = a_pow.shape[-1]
    p = p8 // S5_GPT
    assert bp % V7X_SUBLANES == 0 and bs % V7X_SUBLANES == 0 and bp % 2 == 0
    assert sp % t == 0 and ss % t == 0 and m == bp * sp + bs * ss and 2 * p == ln
    ncp, ncs = sp // t, ss // t
    pitch = ncp + V7X_BF16_ROWS - ncp % V7X_BF16_ROWS
    rows = bp * pitch + bs * ncs
    rows += (-rows) % (2 * V7X_BF16_ROWS)
    def slabbed(a):
        return a.astype(F32).reshape(bs, nl, p8).transpose(1, 0, 2)
    h0 = jnp.concatenate([slabbed(h0_re), slabbed(h0_im)], axis=-1)
    nslab = 2 * p8 // ln
    z, xfp, xfs = pl.pallas_call(
        functools.partial(_s5_kernel, bp=bp, sp=sp, bs=bs, ss=ss, pitch=pitch),
        out_shape=(jax.ShapeDtypeStruct((m, d), BF16),
                   jax.ShapeDtypeStruct((nl, bp, 2 * p8), F32),
                   jax.ShapeDtypeStruct((nl, bs, 2 * p8), F32)),
        grid=(nl,),
        in_specs=[pl.BlockSpec((m, ln), lambda l: (0, l), pipeline_mode=pl.Buffered(1)),
                  pl.BlockSpec((1, t, ln, ln), lambda l: (l, 0, 0, 0)),
                  pl.BlockSpec((1, t * ln, 4 * p), lambda l: (l, 0, 0)),
                  pl.BlockSpec((1, t * ln, 4 * p), lambda l: (l, 0, 0)),
                  pl.BlockSpec((1, 2, p8), lambda l: (l, 0, 0)),
                  pl.BlockSpec((1, bs, 2 * p8), lambda l: (l, 0, 0)),
                  pl.BlockSpec((1, ln), lambda l: (0, l))],
        out_specs=(pl.BlockSpec((m, ln), lambda l: (0, l)),
                   pl.BlockSpec((1, bp, 2 * p8), lambda l: (l, 0, 0)),
                   pl.BlockSpec((1, bs, 2 * p8), lambda l: (l, 0, 0))),
        scratch_shapes=[pltpu.VMEM((t * ln, t * ln), BF16),
                        pltpu.VMEM((t * ln, 2 * p8), BF16),
                        pltpu.VMEM((t * ln, 2 * p8), BF16),
                        pltpu.VMEM((rows, t * ln), BF16),
                        pltpu.VMEM((nslab, rows, ln), F32),
                        pltpu.VMEM((2 * sp + bs * ss, ln), F32)],
        compiler_params=pltpu.CompilerParams(dimension_semantics=("arbitrary",),
                                             vmem_limit_bytes=V7X_VMEM_LIMIT),
        name="s5_core",
    )(hn, bd, w_st, w_ro, a_pow, h0, d_skip.reshape(1, d).astype(F32))

    def unslab(a, nb):
        re = a[..., :p8].transpose(1, 0, 2).reshape(nb, nl * S5_GPT, p)
        im = a[..., p8:].transpose(1, 0, 2).reshape(nb, nl * S5_GPT, p)
        return re, im

    return (z,) + unslab(xfp, bp) + unslab(xfs, bs)


QBLK = 2 * CHUNK
KWIN = WINDOW + QBLK
BIAS_W = KWIN + WINDOW
BIAS_LEN = BIAS_W + QBLK
ATTN_GROUP = 2
ATTN_HEADS_PER_STEP = 2
ATTN_HEAD_GROUP = 4


def _attn_bias_vectors(rel_bias):
    n = np.arange(BIAS_LEN)
    delta = np.where(n < BIAS_LEN - (QBLK - 1), n, n - BIAS_LEN)
    idx = np.clip(WINDOW - delta, -REL_CLIP, REL_CLIP) + REL_CLIP
    return rel_bias.astype(F32)[:, idx][:, None, :]


def _bias_tile(e_row, nq):
    return pltpu.roll(jnp.broadcast_to(e_row, (nq, BIAS_LEN)), 0, 1, stride=1, stride_axis=0)


def _attn_prompt_kernel(q_ref, k_ref, v_ref, e_ref, o_ref, nk_ref, nv_ref, bias_ref, kbf_ref, vbf_ref,
                        *, seq, keep, hps):
    log2e = math.log2(math.e)
    scale = HEAD_DIM ** -0.5 * log2e
    qc = lax.broadcasted_iota(jnp.int32, (QBLK, BIAS_LEN), 0) // CHUNK
    kc = lax.broadcasted_iota(jnp.int32, (QBLK, BIAS_LEN), 1) // CHUNK
    for hh in range(hps):
        tile = _bias_tile(e_ref[hh], QBLK) * log2e
        bias_ref[hh] = jnp.where(kc >= qc, jnp.where(kc <= qc + BAND_CHUNKS, tile, NEG_INF), NEG_INF)
    kbf_ref[...] = k_ref[...].astype(BF16)
    vbf_ref[...] = v_ref[...].astype(BF16)
    nblk = seq // QBLK

    items = []
    for hh in range(hps):
        for i in range(nblk):
            q0 = i * QBLK
            start = max(q0 - WINDOW, 0)
            items.append((hh, q0, start, min(KWIN, seq - start), start - (q0 - WINDOW)))

    def group_scores(group):
        ss = []
        for hh, q0, start, kw, shift in group:
            sl = slice(hh * HEAD_DIM, (hh + 1) * HEAD_DIM)
            q = q_ref[q0:q0 + QBLK, sl].astype(BF16)
            k = kbf_ref[start:start + kw, sl]
            s = lax.dot_general(q, k, (((1,), (1,)), ((), ())), preferred_element_type=F32)
            ss.append(s * scale + bias_ref[hh, :, shift:shift + kw])
        return ss

    groups = [items[g0:g0 + ATTN_GROUP] for g0 in range(0, len(items), ATTN_GROUP)]
    ss_next = group_scores(groups[0])
    for gi, group in enumerate(groups):
        ss = ss_next
        if gi + 1 < len(groups):
            ss_next = group_scores(groups[gi + 1])
        ms = [jnp.max(s, axis=-1, keepdims=True) for s in ss]
        es = [jnp.exp2(s - m) for s, m in zip(ss, ms)]
        ls = [jnp.sum(e, axis=-1, keepdims=True) for e in es]
        for (hh, q0, start, kw, shift), e, l in zip(group, es, ls):
            sl = slice(hh * HEAD_DIM, (hh + 1) * HEAD_DIM)
            v = vbf_ref[start:start + kw, sl]
            o = jnp.dot(e.astype(BF16), v, preferred_element_type=F32) * (1.0 / l)
            o_ref[q0:q0 + QBLK, sl] = o.astype(o_ref.dtype)
    nk_ref[0] = k_ref[seq - keep:seq, :]
    nv_ref[0] = v_ref[seq - keep:seq, :]


def attn_prompt(qkv, bias_vec, *, bp, sp):
    m, d3 = qkv.shape
    d = d3 // 3
    nh = d // HEAD_DIM
    keep = min(WINDOW, sp)
    assert sp % QBLK == 0
    hps = ATTN_HEADS_PER_STEP if nh % ATTN_HEADS_PER_STEP == 0 else 1
    ng = nh // hps
    w = hps * HEAD_DIM
    blk = lambda off: pl.BlockSpec((sp, w), lambda b, h: (b, h + off * ng))
    tail = pl.BlockSpec((1, keep, w), lambda b, h: (b, 0, h))
    return pl.pallas_call(
        functools.partial(_attn_prompt_kernel, seq=sp, keep=keep, hps=hps),
        out_shape=(jax.ShapeDtypeStruct((m, d), BF16),
                   jax.ShapeDtypeStruct((bp, keep, d), F32),
                   jax.ShapeDtypeStruct((bp, keep, d), F32)),
        grid=(bp, ng),
        in_specs=[blk(0), blk(1), blk(2),
                  pl.BlockSpec((hps, 1, BIAS_LEN), lambda b, h: (h, 0, 0))],
        out_specs=(pl.BlockSpec((sp, w), lambda b, h: (b, h)), tail, tail),
        scratch_shapes=[pltpu.VMEM((hps, QBLK, BIAS_LEN), F32),
                        pltpu.VMEM((sp, w), BF16),
                        pltpu.VMEM((sp, w), BF16)],
        compiler_params=_cparams(2),
        name="attn_prompt",
    )(qkv, qkv, qkv, bias_vec)


def _attn_sample_kernel(qkv_ref, ck_ref, cv_ref, e_ref, prev_ref, o_ref, *, seq, ncache, nh, mask):
    del prev_ref
    scale = HEAD_DIM ** -0.5
    d = nh * HEAD_DIM
    dn = (((1,), (1,)), ((), ()))
    off = WINDOW - ncache
    for h0 in range(0, nh, ATTN_HEAD_GROUP):
        heads = range(h0, min(h0 + ATTN_HEAD_GROUP, nh))
        s1s, s2s = [], []
        for h in heads:
            c0 = h * HEAD_DIM
            q = qkv_ref[:, c0:c0 + HEAD_DIM].astype(BF16)
            k = qkv_ref[:, d + c0:d + c0 + HEAD_DIM].astype(BF16)
            ck = ck_ref[0, pl.ds(h, ncache, stride=nh), :].astype(BF16)
            bias = _bias_tile(e_ref[h], seq)
            s1 = lax.dot_general(q, ck, dn, preferred_element_type=F32) * scale + bias[:, off:off + ncache]
            s2 = lax.dot_general(q, k, dn, preferred_element_type=F32) * scale + bias[:, WINDOW:WINDOW + seq]
            if mask is not None:
                s1 = jnp.where(jnp.asarray(mask[:, :ncache]), s1, NEG_INF)
                s2 = jnp.where(jnp.asarray(mask[:, ncache:]), s2, NEG_INF)
            s1s.append(s1)
            s2s.append(s2)
        mxs = [jnp.maximum(jnp.max(s1, axis=-1, keepdims=True), jnp.max(s2, axis=-1, keepdims=True))
               for s1, s2 in zip(s1s, s2s)]
        e1s = [jnp.exp(s1 - mx) for s1, mx in zip(s1s, mxs)]
        e2s = [jnp.exp(s2 - mx) for s2, mx in zip(s2s, mxs)]
        ls = [jnp.sum(e1, axis=-1, keepdims=True) + jnp.sum(e2, axis=-1, keepdims=True)
              for e1, e2 in zip(e1s, e2s)]
        for h, e1, e2, l in zip(heads, e1s, e2s, ls):
            c0 = h * HEAD_DIM
            v = qkv_ref[:, 2 * d + c0:2 * d + c0 + HEAD_DIM].astype(BF16)
            cv = cv_ref[0, pl.ds(h, ncache, stride=nh), :].astype(BF16)
            o = (jnp.dot(e1.astype(BF16), cv, preferred_element_type=F32)
                 + jnp.dot(e2.astype(BF16), v, preferred_element_type=F32)) * (1.0 / l)
            o_ref[:, c0:c0 + HEAD_DIM] = o.astype(o_ref.dtype)


def attn_sample(qkv, cache_k, cache_v, bias_vec, o_prev, *, row0, bs, ss):
    m, d3 = qkv.shape
    d = d3 // 3
    nh = d // HEAD_DIM
    ncache = cache_k.shape[1]
    assert ss <= QBLK and ss % V7X_SUBLANES == 0 and ncache <= WINDOW and row0 % ss == 0
    q_pos = PAST_LEN + np.arange(ss)
    k_pos = PAST_LEN - ncache + np.arange(ncache + ss)
    qc, kc = q_pos // CHUNK, k_pos // CHUNK
    allowed = ((k_pos[None, :] >= 0) & (kc[None, :] <= qc[:, None])
               & (kc[None, :] >= qc[:, None] - BAND_CHUNKS))
    mask = None if allowed.all() else allowed
    ck = cache_k.reshape(bs, ncache * nh, HEAD_DIM)
    cv = cache_v.reshape(bs, ncache * nh, HEAD_DIM)
    blk0 = row0 // ss
    cblk = pl.BlockSpec((1, ncache * nh, HEAD_DIM), lambda b: (b, 0, 0))
    return pl.pallas_call(
        functools.partial(_attn_sample_kernel, seq=ss, ncache=ncache, nh=nh, mask=mask),
        out_shape=jax.ShapeDtypeStruct((m, d), BF16),
        grid=(bs,),
        in_specs=[pl.BlockSpec((ss, d3), lambda b: (b + blk0, 0)), cblk, cblk,
                  pl.BlockSpec((nh, 1, BIAS_LEN), lambda b: (0, 0, 0)),
                  pl.BlockSpec(memory_space=pl.ANY)],
        out_specs=pl.BlockSpec((ss, d), lambda b: (b + blk0, 0)),
        input_output_aliases={4: 0},
        compiler_params=_cparams(1),
        name="attn_sample",
    )(qkv, ck, cv, bias_vec, o_prev)


HGRN_HEADS_PER_STEP = 4


def _hgrn_kernel(*refs, seq, csz, has_state, aliased, hp):
    refs = list(refs)
    q_ref, f_ref, v_ref, z_ref, llb_ref, l1m_ref, gain_ref = refs[:7]
    pos = 7
    s0_ref = None
    if has_state:
        s0_ref = refs[pos]
        pos += 1
    if aliased:
        pos += 1
    o_ref, sn_ref, st_ref = refs[pos:pos + 3]
    kdim = HEAD_DIM
    scale = kdim ** -0.5
    dn_t = (((1,), (1,)), ((), ()))
    for hh in range(hp):
        if has_state:
            st_ref[hh] = s0_ref[0, hh].T
        else:
            st_ref[hh] = jnp.zeros((kdim, kdim), F32)
    row = lax.broadcasted_iota(jnp.int32, (csz, kdim), 0)
    ti = lax.broadcasted_iota(jnp.int32, (csz, csz), 0)
    si = lax.broadcasted_iota(jnp.int32, (csz, csz), 1)
    tx = ti ^ si
    lvl = jnp.where(ti > si, 31 - lax.clz(tx), -1)

    tri = jnp.where(lax.broadcasted_iota(jnp.int32, (csz, csz), 0)
                    >= lax.broadcasted_iota(jnp.int32, (csz, csz), 1), 1.0, 0.0).astype(BF16)

    def gates(r0, hh):
        sl = slice(hh * kdim, (hh + 1) * kdim)
        llb = llb_ref[:, sl]
        l1m = l1m_ref[:, sl]
        qz = q_ref[pl.ds(r0, csz), sl]
        fz = f_ref[pl.ds(r0, csz), sl]
        ls = jnp.minimum(fz, 0.0) - jnp.log(1.0 + jnp.exp(jnp.minimum(fz, -fz)))
        cc = l1m + ls
        dd = llb - cc
        g = jnp.maximum(llb, cc) + jnp.log(1.0 + jnp.exp(jnp.minimum(dd, -dd)))
        kk = jnp.exp(cc - fz)
        q = qz * _sigmoid(qz) * scale
        return g, kk, q

    def prefix_sums(gs):
        parts = []
        for g in gs:
            hi = g.astype(BF16)
            r1 = g - hi.astype(F32)
            mid = r1.astype(BF16)
            lo = (r1 - mid.astype(F32)).astype(BF16)
            parts += [hi, mid, lo]
        sums = jnp.dot(tri, jnp.concatenate(parts, axis=1), preferred_element_type=F32)
        out = []
        for i in range(len(gs)):
            c0 = 3 * i * kdim
            out.append(sums[:, c0:c0 + kdim] + sums[:, c0 + kdim:c0 + 2 * kdim]
                       + sums[:, c0 + 2 * kdim:c0 + 3 * kdim])
        return out

    def one_head(r0, hh, b, kk, q):
        sl = slice(hh * kdim, (hh + 1) * kdim)
        v = v_ref[pl.ds(r0, csz), sl]
        gz = z_ref[pl.ds(r0, csz), sl]
        v_bf = v.astype(BF16)
        sc = lax.dot_general(q.astype(BF16), kk.astype(BF16), dn_t, preferred_element_type=F32)
        scores = jnp.where(tx == 0, sc, 0.0)
        e = b
        w = 1
        while w < min(V7X_SUBLANES, csz):
            bit = (row & w) != 0
            e_prev = pltpu.roll(e, w, 0)
            ex = jnp.exp(jnp.where(bit, b - e_prev, e - b))
            qk = (jnp.where(bit, q, kk) * ex).astype(BF16)
            sc = lax.dot_general(qk, qk, dn_t, preferred_element_type=F32)
            scores = scores + jnp.where(lvl == int(math.log2(w)), sc, 0.0)
            if 2 * w < V7X_SUBLANES:
                e = jnp.where(bit, e, pltpu.roll(e, csz - w, 0))
            w *= 2
        while w < csz:
            n2 = csz // (2 * w)
            b4 = b.reshape(n2, 2, w, kdim)
            bound = b4[:, 0, w - 1:w, :]
            kpart = kk.reshape(n2, 2, w, kdim)[:, 0] * jnp.exp(bound - b4[:, 0])
            qpart = q.reshape(n2, 2, w, kdim)[:, 1] * jnp.exp(b4[:, 1] - bound)
            zero = jnp.zeros((n2, 1, w, kdim), F32)
            kw = jnp.concatenate([kpart[:, None], zero], axis=1).reshape(csz, kdim).astype(BF16)
            qw = jnp.concatenate([zero, qpart[:, None]], axis=1).reshape(csz, kdim).astype(BF16)
            sc = lax.dot_general(qw, kw, dn_t, preferred_element_type=F32)
            if 2 * w < csz:
                scores = scores + jnp.where(tx < 2 * w, sc, 0.0)
            else:
                scores = scores + sc
            w *= 2
        o = jnp.dot(scores.astype(BF16), v_bf, preferred_element_type=F32)
        st = st_ref[hh]
        o = o + lax.dot_general((q * jnp.exp(b)).astype(BF16), st.astype(BF16), dn_t,
                                preferred_element_type=F32)
        b_last = b[csz - 1:csz]
        kd = (kk * jnp.exp(b_last - b)).astype(BF16)
        st_ref[hh] = jnp.exp(b_last) * st + lax.dot_general(
            v_bf, kd, (((0,), (0,)), ((), ())), preferred_element_type=F32)
        o = o * lax.rsqrt(jnp.mean(o * o, axis=-1, keepdims=True) + RMS_EPS)
        o = o * gain_ref[:, sl] * (gz * _sigmoid(gz))
        o_ref[pl.ds(r0, csz), sl] = o.astype(o_ref.dtype)

    def chunk(ci, _):
        r0 = pl.multiple_of(ci * csz, csz)
        gkq = [gates(r0, hh) for hh in range(hp)]
        bs = prefix_sums([g for g, _, _ in gkq])
        for hh in range(hp):
            one_head(r0, hh, bs[hh], gkq[hh][1], gkq[hh][2])
        return 0

    nchunks = seq // csz
    if nchunks == 1:
        chunk(0, 0)
    else:
        lax.fori_loop(0, nchunks, chunk, 0)
    for hh in range(hp):
        sn_ref[0, hh] = st_ref[hh].T


def hgrn_core(proj, log_lb, log_1m_lb, norm_gain, state, o_prev, *, row0, nb, seq):
    m, d4 = proj.shape
    d = d4 // 4
    nh = d // HEAD_DIM
    csz = _pick(seq, (128, 64, 32, 16, 8))
    hp = HGRN_HEADS_PER_STEP * max(1, V7X_LANES // (2 * csz))
    while nh % hp:
        hp //= 2
    ng = nh // hp
    w = hp * HEAD_DIM
    assert row0 % seq == 0
    blk0 = row0 // seq
    has_state = state is not None
    aliased = o_prev is not None
    blk = lambda off: pl.BlockSpec((seq, w), lambda b, h: (b + blk0, h + off * ng))
    vec = pl.BlockSpec((1, w), lambda b, h: (0, h))
    sblk = pl.BlockSpec((1, hp, HEAD_DIM, HEAD_DIM), lambda b, h: (b, h, 0, 0))
    in_specs = [blk(0), blk(1), blk(2), blk(3), vec, vec, vec]
    args = [proj, proj, proj, proj, log_lb.reshape(1, d), log_1m_lb.reshape(1, d),
            norm_gain.reshape(1, d).astype(F32)]
    if has_state:
        in_specs.append(sblk)
        args.append(state.astype(F32))
    aliases = {}
    if aliased:
        aliases = {len(args): 0}
        in_specs.append(pl.BlockSpec(memory_space=pl.ANY))
        args.append(o_prev)
    return pl.pallas_call(
        functools.partial(_hgrn_kernel, seq=seq, csz=csz, has_state=has_state, aliased=aliased, hp=hp),
        out_shape=(jax.ShapeDtypeStruct((m, d), BF16),
                   jax.ShapeDtypeStruct((nb, nh, HEAD_DIM, HEAD_DIM), F32)),
        grid=(nb, ng),
        in_specs=in_specs,
        out_specs=(pl.BlockSpec((seq, w), lambda b, h: (b + blk0, h)), sblk),
        scratch_shapes=[pltpu.VMEM((hp, HEAD_DIM, HEAD_DIM), F32)],
        input_output_aliases=aliases,
        compiler_params=_cparams(2),
        name="hgrn_core",
    )(*args)


def kernel(x_prompt, x_sample, state_s5_re, state_s5_im, cache_attn_k, cache_attn_v, state_hgrn,
           norm_mixer, norm_ffn, norm_final, ffn_w_gate_up, ffn_w_down,
           s5_lambda_re, s5_lambda_im, s5_log_step, s5_b_re, s5_b_im, s5_c_re, s5_c_im, s5_d, s5_w_glu,
           attn_w_qkv, attn_rel_bias, attn_w_o,
           hgrn_w_in, hgrn_lower_bounds, hgrn_norm, hgrn_w_o):
    bp, sp, d = x_prompt.shape
    bs, ss, _ = x_sample.shape
    depth = norm_mixer.shape[0]
    mp = bp * sp
    ms = bs * ss
    m = mp + ms
    d_ff = ffn_w_down.shape[1]
    nh = d // HEAD_DIM

    assert depth > 0
    x = None
    bm = _pick(m, (1536, 768, 512, 256, 128, 64, 32, 16, 8))
    bm_down = _pick(m, (512, 256, 128, 64, 32, 16, 8))
    bn = _pick(d, (256, 128))
    bn_proj = _pick(d, (512, 256, 128))
    bn_down = _pick(d, (512, 256, 128))
    bn_ff = _pick(d_ff, (256, 128))

    w_gate_up = ffn_w_gate_up
    w_qkv = attn_w_qkv
    w_hin = hgrn_w_in
    w_down = ffn_w_down.astype(BF16)
    w_glu = s5_w_glu.astype(BF16)
    w_ao = attn_w_o.astype(BF16)
    w_ho = hgrn_w_o.astype(BF16)

    lbs = jnp.cumsum(jax.nn.softmax(hgrn_lower_bounds.astype(F32), axis=0), axis=0)
    lbs = lbs - lbs[0]

    new_re_p, new_im_p, new_re_s, new_im_s = [], [], [], []
    new_k_p, new_v_p, new_k_s, new_v_s = [], [], [], []
    new_h_p, new_h_s = [], []

    for layer in range(depth):
        kind = layer % N_MIXERS
        j = layer // N_MIXERS
        if kind == 0:
            if layer == 0:
                hn, x = rmsnorm_concat([x_prompt.reshape(mp, d).astype(F32), x_sample.reshape(ms, d).astype(F32)],
                                       norm_mixer[layer])
            else:
                hn = rmsnorm(x, norm_mixer[layer], F32)
            ops = _s5_operators(s5_lambda_re[j], s5_lambda_im[j], s5_log_step[j],
                                s5_b_re[j], s5_b_im[j], s5_c_re[j], s5_c_im[j])
            z, re_p, im_p, re_s, im_s = s5_mixer_core(
                hn, ops, s5_d[j], state_s5_re[j], state_s5_im[j], bp=bp, sp=sp, bs=bs, ss=ss)
            new_re_p.append(re_p)
            new_im_p.append(im_p)
            new_re_s.append(re_s)
            new_im_s.append(im_s)
            x, xg, ssq = matmul(z, w_glu, j, n_out=d, w_col_offsets=(0, d), epi="glu_res",
                                out_dtype=F32, res=x, bm=bm, bn=bn, next_gain=norm_ffn[layer], name="s5_glu")
        elif kind == 1:
            qkv = matmul(xg, w_qkv, j, n_out=3 * d, w_col_offsets=(0,), epi="none",
                         out_dtype=F32, bm=bm, bn=bn_proj, ssq=ssq, name="attn_qkv")
            bias_vec = _attn_bias_vectors(attn_rel_bias[j])
            o, nk_p, nv_p = attn_prompt(qkv, bias_vec, bp=bp, sp=sp)
            o = attn_sample(qkv, cache_attn_k[j], cache_attn_v[j], bias_vec, o, row0=mp, bs=bs, ss=ss)
            qkv_s = qkv[mp:].reshape(bs, ss, 3 * d)
            new_k_p.append(nk_p.reshape(bp, -1, nh, HEAD_DIM))
            new_v_p.append(nv_p.reshape(bp, -1, nh, HEAD_DIM))
            k_all = jnp.concatenate([cache_attn_k[j], qkv_s[:, :, d:2 * d].reshape(bs, ss, nh, HEAD_DIM)], axis=1)
            v_all = jnp.concatenate([cache_attn_v[j], qkv_s[:, :, 2 * d:].reshape(bs, ss, nh, HEAD_DIM)], axis=1)
            new_k_s.append(k_all[:, ss:])
            new_v_s.append(v_all[:, ss:])
            x, xg, ssq = matmul(o, w_ao, j, n_out=d, w_col_offsets=(0,), epi="res",
                                out_dtype=F32, res=x, bm=bm, bn=bn_proj, next_gain=norm_ffn[layer], name="attn_wo")
        else:
            proj = matmul(xg, w_hin, j, n_out=4 * d, w_col_offsets=(0,), epi="none",
                          out_dtype=F32, bm=bm, bn=bn_proj, ssq=ssq, name="hgrn_win")
            lb = lbs[layer]
            log_lb = jnp.log(lb)
            log_1m_lb = jnp.log1p(-lb)
            o, h_p = hgrn_core(proj, log_lb, log_1m_lb, hgrn_norm[j], None, None, row0=0, nb=bp, seq=sp)
            o, h_s = hgrn_core(proj, log_lb, log_1m_lb, hgrn_norm[j], state_hgrn[j], o, row0=mp, nb=bs, seq=ss)
            new_h_p.append(h_p)
            new_h_s.append(h_s)
            x, xg, ssq = matmul(o, w_ho, j, n_out=d, w_col_offsets=(0,), epi="res",
                                out_dtype=F32, res=x, bm=bm, bn=bn_proj, next_gain=norm_ffn[layer], name="hgrn_wo")
        act = matmul(xg, w_gate_up, layer, n_out=d_ff, w_col_offsets=(0, d_ff),
                     epi="swiglu", out_dtype=BF16, bm=bm, bn=bn_ff, ssq=ssq, name="ffn_gate_up")
        fuse_next = layer + 1 < depth and (layer + 1) % N_MIXERS != 0
        if fuse_next:
            x, xg, ssq = matmul(act, w_down, layer, n_out=d, w_col_offsets=(0,), epi="res",
                                out_dtype=F32, res=x, bm=bm_down, bn=bn_down,
                                next_gain=norm_mixer[layer + 1], name="ffn_down")
        else:
            x = matmul(act, w_down, layer, n_out=d, w_col_offsets=(0,), epi="res",
                       out_dtype=F32, res=x, bm=bm_down, bn=bn_down, w_resident=True, name="ffn_down")

    y_p = rmsnorm(x, norm_final, F32, 0, mp).reshape(bp, sp, d)
    y_s = rmsnorm(x, norm_final, F32, mp, ms).reshape(bs, ss, d)
    return (y_p, y_s,
            jnp.stack(new_re_p), jnp.stack(new_im_p),
            jnp.stack(new_k_p), jnp.stack(new_v_p), jnp.stack(new_h_p),
            jnp.stack(new_re_s), jnp.stack(new_im_s),
            jnp.stack(new_k_s), jnp.stack(new_v_s), jnp.stack(new_h_s))
```
